```python
import math
import jax, jax.numpy as jnp
from jax import lax
import numpy as np

D_MODEL = 1024
BATCH = 8
SEQ = 16384
DEPTH = 4

HEAD_DIM = 64
N_MIX_HEADS = 8
N_MEM_HEADS = 4
MIX_WIDTH = N_MIX_HEADS * HEAD_DIM
MEM_WIDTH = N_MEM_HEADS * HEAD_DIM
MERGED_WIDTH = MIX_WIDTH + MEM_WIDTH
N_MEM = 256
D_FF = 4 * D_MODEL
BLOCK_Q = 128
GROUP_Q = 1024
N_A = DEPTH // 2
N_B = DEPTH - N_A
W_IN_A = 3 * MIX_WIDTH + MEM_WIDTH
W_IN_B = MIX_WIDTH + MEM_WIDTH
W_KV_SHARED = 2 * MIX_WIDTH + N_MIX_HEADS
EPS = 1e-6
NEG_INF = -1e30
FORGET_BIAS_INIT = 2.0

kernel_name = "yoco_stickbreak_fox_hybrid"


def rms_norm(x, g):
    xf = x.astype(jnp.float32)
    y = xf * lax.rsqrt(jnp.mean(xf * xf, axis=-1, keepdims=True) + EPS) * g.astype(jnp.float32)
    return y.astype(x.dtype)


def split_heads(t, n_heads):
    b, s, _ = t.shape
    return t.reshape(b, s, n_heads, HEAD_DIM).transpose(0, 2, 1, 3)


def merge_heads(t):
    b, h, s, d = t.shape
    return t.transpose(0, 2, 1, 3).reshape(b, s, h * d)


def to_blocks(t):
    b, h, s = t.shape[:3]
    t = t.reshape((b, h, s // BLOCK_Q, BLOCK_Q) + t.shape[3:])
    return jnp.moveaxis(t, 2, 0)


def from_blocks(t):
    nb, b, h, blk, d = t.shape
    return jnp.moveaxis(t, 0, 2).reshape(b, h, nb * blk, d)


def causal_sweep(block_fn, q_side, kv_side):
    s_len = q_side[0].shape[2]
    outs = []
    for g0 in range(0, s_len, GROUP_Q):
        g1 = min(g0 + GROUP_Q, s_len)
        kv_g = tuple(t[:, :, :g1] for t in kv_side)
        q_blocks = tuple(to_blocks(t[:, :, g0:g1]) for t in q_side)
        starts = g0 + jnp.arange((g1 - g0) // BLOCK_Q) * BLOCK_Q

        def body(args, kv_g=kv_g):
            return block_fn(*args, *kv_g)

        outs.append(from_blocks(lax.map(body, q_blocks + (starts,))))
    return jnp.concatenate(outs, axis=2)


def _stick_breaking_block(q_blk, t0, k, v):
    scale = 1.0 / math.sqrt(HEAD_DIM)
    z = jnp.einsum('bhqd,bhkd->bhqk', q_blk, k).astype(jnp.float32) * scale
    key_pos = jnp.arange(k.shape[2])
    t_pos = t0 + jnp.arange(BLOCK_Q)
    mask = key_pos[None, :] < t_pos[:, None]
    log_one_minus = jnp.where(mask, jax.nn.log_sigmoid(-z), 0.0)
    between = lax.cumsum(log_one_minus, axis=3, reverse=True) - log_one_minus
    w = jnp.where(mask, jnp.exp(jax.nn.log_sigmoid(z) + between), 0.0)
    return jnp.einsum('bhqk,bhkd->bhqd', w.astype(v.dtype), v)


def stick_breaking_attention(q, k, v):
    return causal_sweep(_stick_breaking_block, (q,), (k, v))


def _forgetting_block(q_blk, c_blk, t0, k, v, c_k):
    scale = 1.0 / math.sqrt(HEAD_DIM)
    z = jnp.einsum('bhqd,bhkd->bhqk', q_blk, k).astype(jnp.float32) * scale
    z = z + c_blk[..., :, None] - c_k[:, :, None, :]
    key_pos = jnp.arange(k.shape[2])
    t_pos = t0 + jnp.arange(BLOCK_Q)
    mask = key_pos[None, :] <= t_pos[:, None]
    p = jax.nn.softmax(jnp.where(mask, z, NEG_INF), axis=-1)
    return jnp.einsum('bhqk,bhkd->bhqd', p.astype(v.dtype), v)


def forgetting_attention(q, k, v, log_f_cum):
    return causal_sweep(_forgetting_block, (q, log_f_cum), (k, v, log_f_cum))


def memory_attention(q_mem, mem_k, mem_v):
    scale = 1.0 / math.sqrt(HEAD_DIM)
    s = jnp.einsum('bshd,bmhd->bhsm', q_mem, mem_k).astype(jnp.float32) * scale
    p = jax.nn.softmax(s, axis=-1)
    o = jnp.einsum('bhsm,bmhd->bshd', p.astype(mem_v.dtype), mem_v)
    b, sl = q_mem.shape[:2]
    return o.reshape(b, sl, MEM_WIDTH)


def squared_relu_mlp(x, w1, w2):
    h = jnp.square(jax.nn.relu(x @ w1))
    return h @ w2


def _fwd_setup_inputs(seed: int = 0) -> dict:
    key = jax.random.key(seed)
    ks = jax.random.split(key, 16)
    f32 = jnp.float32
    nrm = lambda k, shape, s: (jax.random.normal(k, shape, f32) * s).astype(f32)
    x = jax.random.normal(ks[0], (BATCH, SEQ, D_MODEL), f32)
    mem = jax.random.normal(ks[1], (BATCH, N_MEM, D_MODEL), f32)
    norm1_g = 1.0 + nrm(ks[2], (DEPTH, D_MODEL), 0.02)
    w_in_a = nrm(ks[3], (N_A, D_MODEL, W_IN_A), D_MODEL ** -0.5)
    w_in_b = nrm(ks[4], (N_B, D_MODEL, W_IN_B), D_MODEL ** -0.5)
    w_mem_kv = nrm(ks[5], (DEPTH, D_MODEL, 2 * MEM_WIDTH), D_MODEL ** -0.5)
    mem_norm_g = 1.0 + nrm(ks[6], (DEPTH, D_MODEL), 0.02)
    w_o = nrm(ks[7], (DEPTH, MERGED_WIDTH, D_MODEL), MERGED_WIDTH ** -0.5)
    norm2_g = 1.0 + nrm(ks[8], (DEPTH, D_MODEL), 0.02)
    w_mlp1 = nrm(ks[9], (DEPTH, D_MODEL, D_FF), D_MODEL ** -0.5)
    w_mlp2 = nrm(ks[10], (DEPTH, D_FF, D_MODEL), D_FF ** -0.5)
    kv_norm_g = 1.0 + nrm(ks[11], (D_MODEL,), 0.02)
    w_kv_shared = nrm(ks[12], (D_MODEL, W_KV_SHARED), D_MODEL ** -0.5)
    b_f = FORGET_BIAS_INIT + nrm(ks[13], (N_MIX_HEADS,), 0.1)
    final_norm_g = 1.0 + nrm(ks[14], (D_MODEL,), 0.02)
    return {"x": x, "mem": mem, "norm1_g": norm1_g, "w_in_a": w_in_a, "w_in_b": w_in_b,
            "w_mem_kv": w_mem_kv, "mem_norm_g": mem_norm_g, "w_o": w_o, "norm2_g": norm2_g,
            "w_mlp1": w_mlp1, "w_mlp2": w_mlp2, "kv_norm_g": kv_norm_g,
            "w_kv_shared": w_kv_shared, "b_f": b_f, "final_norm_g": final_norm_g}


def _fwd_reference(x, mem, norm1_g, w_in_a, w_in_b, w_mem_kv, mem_norm_g, w_o, norm2_g,
              w_mlp1, w_mlp2, kv_norm_g, w_kv_shared, b_f, final_norm_g):
    b, s_len, _ = x.shape
    m_len = mem.shape[1]
    h = x
    k_sh = v_sh = log_f_cum = None
    for l in range(DEPTH):
        if l == N_A:
            hs = rms_norm(h, kv_norm_g)
            kvf = hs @ w_kv_shared
            k_sh = split_heads(kvf[..., :MIX_WIDTH], N_MIX_HEADS)
            v_sh = split_heads(kvf[..., MIX_WIDTH:2 * MIX_WIDTH], N_MIX_HEADS)
            f_logit = kvf[..., 2 * MIX_WIDTH:].astype(jnp.float32) + b_f.astype(jnp.float32)
            log_f_cum = jnp.moveaxis(lax.cumsum(jax.nn.log_sigmoid(f_logit), axis=1), 1, 2)

        hn = rms_norm(h, norm1_g[l])
        mkv = rms_norm(mem, mem_norm_g[l]) @ w_mem_kv[l]
        mem_k = mkv[..., :MEM_WIDTH].reshape(b, m_len, N_MEM_HEADS, HEAD_DIM)
        mem_v = mkv[..., MEM_WIDTH:].reshape(b, m_len, N_MEM_HEADS, HEAD_DIM)

        if l < N_A:
            proj = hn @ w_in_a[l]
            q = split_heads(proj[..., :MIX_WIDTH], N_MIX_HEADS)
            k = split_heads(proj[..., MIX_WIDTH:2 * MIX_WIDTH], N_MIX_HEADS)
            v = split_heads(proj[..., 2 * MIX_WIDTH:3 * MIX_WIDTH], N_MIX_HEADS)
            q_mem = proj[..., 3 * MIX_WIDTH:]
            mix = stick_breaking_attention(q, k, v)
        else:
            proj = hn @ w_in_b[l - N_A]
            q = split_heads(proj[..., :MIX_WIDTH], N_MIX_HEADS)
            q_mem = proj[..., MIX_WIDTH:]
            mix = forgetting_attention(q, k_sh, v_sh, log_f_cum)

        mem_out = memory_attention(q_mem.reshape(b, s_len, N_MEM_HEADS, HEAD_DIM), mem_k, mem_v)
        merged = jnp.concatenate([merge_heads(mix), mem_out], axis=-1)
        h = h + merged @ w_o[l]
        h = h + squared_relu_mlp(rms_norm(h, norm2_g[l]), w_mlp1[l], w_mlp2[l])
    return rms_norm(h, final_norm_g)


import jax as _jax
import jax.numpy as _jnp

TWIN_FORMAT = 'train_step'
FWD_PARAMS = ['x', 'mem', 'norm1_g', 'w_in_a', 'w_in_b', 'w_mem_kv', 'mem_norm_g', 'w_o', 'norm2_g', 'w_mlp1', 'w_mlp2', 'kv_norm_g', 'w_kv_shared', 'b_f', 'final_norm_g']
TWIN_WEIGHTS = ['norm1_g', 'w_in_a', 'w_in_b', 'w_mem_kv', 'mem_norm_g', 'w_o', 'norm2_g', 'w_mlp1', 'w_mlp2', 'kv_norm_g', 'w_kv_shared', 'b_f', 'final_norm_g']
TWIN_DIFF_INPUT = 'x'
TWIN_INPUTS = ['x', 'mem', 'norm1_g', 'w_in_a', 'w_in_b', 'w_mem_kv', 'mem_norm_g', 'w_o', 'norm2_g', 'w_mlp1', 'w_mlp2', 'kv_norm_g', 'w_kv_shared', 'b_f', 'final_norm_g', 'loss_target', 'm_norm1_g', 'm_w_in_a', 'm_w_in_b', 'm_w_mem_kv', 'm_mem_norm_g', 'm_w_o', 'm_norm2_g', 'm_w_mlp1', 'm_w_mlp2', 'm_kv_norm_g', 'm_w_kv_shared', 'm_b_f', 'm_final_norm_g', 'v_norm1_g', 'v_w_in_a', 'v_w_in_b', 'v_w_mem_kv', 'v_mem_norm_g', 'v_w_o', 'v_norm2_g', 'v_w_mlp1', 'v_w_mlp2', 'v_kv_norm_g', 'v_w_kv_shared', 'v_b_f', 'v_final_norm_g']
TWIN_OUTPUTS = ['loss', 'grad_x', 'grad_norm1_g', 'grad_w_in_a', 'grad_w_in_b', 'grad_w_mem_kv', 'grad_mem_norm_g', 'grad_w_o', 'grad_norm2_g', 'grad_w_mlp1', 'grad_w_mlp2', 'grad_kv_norm_g', 'grad_w_kv_shared', 'grad_b_f', 'grad_final_norm_g', 'delta_norm1_g', 'delta_w_in_a', 'delta_w_in_b', 'delta_w_mem_kv', 'delta_mem_norm_g', 'delta_w_o', 'delta_norm2_g', 'delta_w_mlp1', 'delta_w_mlp2', 'delta_kv_norm_g', 'delta_w_kv_shared', 'delta_b_f', 'delta_final_norm_g', 'new_m_norm1_g', 'new_m_w_in_a', 'new_m_w_in_b', 'new_m_w_mem_kv', 'new_m_mem_norm_g', 'new_m_w_o', 'new_m_norm2_g', 'new_m_w_mlp1', 'new_m_w_mlp2', 'new_m_kv_norm_g', 'new_m_w_kv_shared', 'new_m_b_f', 'new_m_final_norm_g', 'new_v_norm1_g', 'new_v_w_in_a', 'new_v_w_in_b', 'new_v_w_mem_kv', 'new_v_mem_norm_g', 'new_v_w_o', 'new_v_norm2_g', 'new_v_w_mlp1', 'new_v_w_mlp2', 'new_v_kv_norm_g', 'new_v_w_kv_shared', 'new_v_b_f', 'new_v_final_norm_g']
TWIN_LEAF_KINDS = {'loss': 'loss', 'grad_x': 'grad_x', 'grad_norm1_g': 'grad_w', 'grad_w_in_a': 'grad_w', 'grad_w_in_b': 'grad_w', 'grad_w_mem_kv': 'grad_w', 'grad_mem_norm_g': 'grad_w', 'grad_w_o': 'grad_w', 'grad_norm2_g': 'grad_w', 'grad_w_mlp1': 'grad_w', 'grad_w_mlp2': 'grad_w', 'grad_kv_norm_g': 'grad_w', 'grad_w_kv_shared': 'grad_w', 'grad_b_f': 'grad_w', 'grad_final_norm_g': 'grad_w', 'delta_norm1_g': 'delta_w', 'delta_w_in_a': 'delta_w', 'delta_w_in_b': 'delta_w', 'delta_w_mem_kv': 'delta_w', 'delta_mem_norm_g': 'delta_w', 'delta_w_o': 'delta_w', 'delta_norm2_g': 'delta_w', 'delta_w_mlp1': 'delta_w', 'delta_w_mlp2': 'delta_w', 'delta_kv_norm_g': 'delta_w', 'delta_w_kv_shared': 'delta_w', 'delta_b_f': 'delta_w', 'delta_final_norm_g': 'delta_w', 'new_m_norm1_g': 'new_m', 'new_m_w_in_a': 'new_m', 'new_m_w_in_b': 'new_m', 'new_m_w_mem_kv': 'new_m', 'new_m_mem_norm_g': 'new_m', 'new_m_w_o': 'new_m', 'new_m_norm2_g': 'new_m', 'new_m_w_mlp1': 'new_m', 'new_m_w_mlp2': 'new_m', 'new_m_kv_norm_g': 'new_m', 'new_m_w_kv_shared': 'new_m', 'new_m_b_f': 'new_m', 'new_m_final_norm_g': 'new_m', 'new_v_norm1_g': 'new_v', 'new_v_w_in_a': 'new_v', 'new_v_w_in_b': 'new_v', 'new_v_w_mem_kv': 'new_v', 'new_v_mem_norm_g': 'new_v', 'new_v_w_o': 'new_v', 'new_v_norm2_g': 'new_v', 'new_v_w_mlp1': 'new_v', 'new_v_w_mlp2': 'new_v', 'new_v_kv_norm_g': 'new_v', 'new_v_w_kv_shared': 'new_v', 'new_v_b_f': 'new_v', 'new_v_final_norm_g': 'new_v'}


def _forward(args):
    return _fwd_reference(*[args[k] for k in FWD_PARAMS])


def _output_shape():
    def fwd():
        inp = _fwd_setup_inputs(0)
        return _fwd_reference(*[inp[k] for k in FWD_PARAMS])
    out = _jax.eval_shape(fwd)
    return out.shape, out.dtype

N_MICROBATCH = 1
ADAM_LR = 0.001
ADAM_B1 = 0.9
ADAM_B2 = 0.999
ADAM_EPS = 1e-08
ADAM_WD = 0.01
ADAM_STEP = 10
PER_EXAMPLE_BATCH_AXIS = {'x': 0, 'mem': 0, 'loss_target': 0}
SHARED_INPUTS = []
_WEIGHT_DTYPES = {'norm1_g': _jnp.float32, 'w_in_a': _jnp.float32, 'w_in_b': _jnp.float32, 'w_mem_kv': _jnp.float32, 'mem_norm_g': _jnp.float32, 'w_o': _jnp.float32, 'norm2_g': _jnp.float32, 'w_mlp1': _jnp.float32, 'w_mlp2': _jnp.float32, 'kv_norm_g': _jnp.float32, 'w_kv_shared': _jnp.float32, 'b_f': _jnp.float32, 'final_norm_g': _jnp.float32}
MOMENT_SCALE = {'norm1_g': 1.693294e-01, 'w_in_a': 1.723409e-01, 'w_in_b': 4.059413e-02, 'w_mem_kv': 3.289336e-02, 'mem_norm_g': 2.411420e-02, 'w_o': 1.667895e-01, 'norm2_g': 2.962764e-01, 'w_mlp1': 1.372638e-01, 'w_mlp2': 3.525003e-01, 'kv_norm_g': 2.004352e-01, 'w_kv_shared': 2.090374e-01, 'b_f': 1.010793e+00, 'final_norm_g': 1.320284e+02}


def _to_microbatches(a, axis):
    t = _jnp.moveaxis(a, axis, 0)
    t = t.reshape((N_MICROBATCH, t.shape[0] // N_MICROBATCH) + t.shape[1:])
    return _jnp.moveaxis(t, 1, axis + 1)


def setup_inputs(seed: int = 0) -> dict:
    inp = _fwd_setup_inputs(seed)
    key = _jax.random.fold_in(_jax.random.key(seed), 7919)
    shape, _ = _output_shape()
    out = dict(inp)
    out["loss_target"] = _jax.random.normal(_jax.random.fold_in(key, 0), shape, _jnp.float32)
    for i, name in enumerate(TWIN_WEIGHTS):
        w = inp[name].astype(_jnp.float32)
        if MOMENT_SCALE is None:
            s = _jnp.sqrt(_jnp.mean(_jnp.square(w)) + 1e-30)
        else:
            s = MOMENT_SCALE[name]
        km, kv = _jax.random.split(_jax.random.fold_in(key, i + 1))
        out[name] = w
        out["m_" + name] = s * _jax.random.normal(km, w.shape, _jnp.float32)
        out["v_" + name] = (s * s) * _jax.random.uniform(kv, w.shape, _jnp.float32, 0.5, 1.5)
    if N_MICROBATCH > 1:
        for name, axis in PER_EXAMPLE_BATCH_AXIS.items():
            out[name] = _to_microbatches(out[name], axis)
    return {'x': out['x'], 'mem': out['mem'], 'norm1_g': out['norm1_g'], 'w_in_a': out['w_in_a'], 'w_in_b': out['w_in_b'], 'w_mem_kv': out['w_mem_kv'], 'mem_norm_g': out['mem_norm_g'], 'w_o': out['w_o'], 'norm2_g': out['norm2_g'], 'w_mlp1': out['w_mlp1'], 'w_mlp2': out['w_mlp2'], 'kv_norm_g': out['kv_norm_g'], 'w_kv_shared': out['w_kv_shared'], 'b_f': out['b_f'], 'final_norm_g': out['final_norm_g'], 'loss_target': out['loss_target'], 'm_norm1_g': out['m_norm1_g'], 'm_w_in_a': out['m_w_in_a'], 'm_w_in_b': out['m_w_in_b'], 'm_w_mem_kv': out['m_w_mem_kv'], 'm_mem_norm_g': out['m_mem_norm_g'], 'm_w_o': out['m_w_o'], 'm_norm2_g': out['m_norm2_g'], 'm_w_mlp1': out['m_w_mlp1'], 'm_w_mlp2': out['m_w_mlp2'], 'm_kv_norm_g': out['m_kv_norm_g'], 'm_w_kv_shared': out['m_w_kv_shared'], 'm_b_f': out['m_b_f'], 'm_final_norm_g': out['m_final_norm_g'], 'v_norm1_g': out['v_norm1_g'], 'v_w_in_a': out['v_w_in_a'], 'v_w_in_b': out['v_w_in_b'], 'v_w_mem_kv': out['v_w_mem_kv'], 'v_mem_norm_g': out['v_mem_norm_g'], 'v_w_o': out['v_w_o'], 'v_norm2_g': out['v_norm2_g'], 'v_w_mlp1': out['v_w_mlp1'], 'v_w_mlp2': out['v_w_mlp2'], 'v_kv_norm_g': out['v_kv_norm_g'], 'v_w_kv_shared': out['v_w_kv_shared'], 'v_b_f': out['v_b_f'], 'v_final_norm_g': out['v_final_norm_g']}


def _loss(weights, diff, rest, loss_target):
    with _jax.named_scope("forward"):
        args = {**rest, TWIN_DIFF_INPUT: diff, **{k: w.astype(_WEIGHT_DTYPES[k]) for k, w in weights.items()}}
        y = _forward(args)
    with _jax.named_scope("loss_head"):
        err = _jnp.square(y.astype(_jnp.float32) - loss_target)
        return 0.5 * _jnp.sum(_jnp.mean(err, axis=-1)) if err.ndim else 0.5 * err


def _adamw(w, g, m, v):
    m = ADAM_B1 * m + (1.0 - ADAM_B1) * g
    v = ADAM_B2 * v + (1.0 - ADAM_B2) * _jnp.square(g)
    m_hat = m / (1.0 - ADAM_B1 ** ADAM_STEP)
    v_hat = v / (1.0 - ADAM_B2 ** ADAM_STEP)
    delta = -ADAM_LR * (m_hat / (_jnp.sqrt(v_hat) + ADAM_EPS) + ADAM_WD * w)
    return delta, m, v


def reference(x, mem, norm1_g, w_in_a, w_in_b, w_mem_kv, mem_norm_g, w_o, norm2_g, w_mlp1, w_mlp2, kv_norm_g, w_kv_shared, b_f, final_norm_g, loss_target, m_norm1_g, m_w_in_a, m_w_in_b, m_w_mem_kv, m_mem_norm_g, m_w_o, m_norm2_g, m_w_mlp1, m_w_mlp2, m_kv_norm_g, m_w_kv_shared, m_b_f, m_final_norm_g, v_norm1_g, v_w_in_a, v_w_in_b, v_w_mem_kv, v_mem_norm_g, v_w_o, v_norm2_g, v_w_mlp1, v_w_mlp2, v_kv_norm_g, v_w_kv_shared, v_b_f, v_final_norm_g):
    given = dict(x=x, mem=mem, norm1_g=norm1_g, w_in_a=w_in_a, w_in_b=w_in_b, w_mem_kv=w_mem_kv, mem_norm_g=mem_norm_g, w_o=w_o, norm2_g=norm2_g, w_mlp1=w_mlp1, w_mlp2=w_mlp2, kv_norm_g=kv_norm_g, w_kv_shared=w_kv_shared, b_f=b_f, final_norm_g=final_norm_g, loss_target=loss_target, m_norm1_g=m_norm1_g, m_w_in_a=m_w_in_a, m_w_in_b=m_w_in_b, m_w_mem_kv=m_w_mem_kv, m_mem_norm_g=m_mem_norm_g, m_w_o=m_w_o, m_norm2_g=m_norm2_g, m_w_mlp1=m_w_mlp1, m_w_mlp2=m_w_mlp2, m_kv_norm_g=m_kv_norm_g, m_w_kv_shared=m_w_kv_shared, m_b_f=m_b_f, m_final_norm_g=m_final_norm_g, v_norm1_g=v_norm1_g, v_w_in_a=v_w_in_a, v_w_in_b=v_w_in_b, v_w_mem_kv=v_w_mem_kv, v_mem_norm_g=v_mem_norm_g, v_w_o=v_w_o, v_norm2_g=v_norm2_g, v_w_mlp1=v_w_mlp1, v_w_mlp2=v_w_mlp2, v_kv_norm_g=v_kv_norm_g, v_w_kv_shared=v_w_kv_shared, v_b_f=v_b_f, v_final_norm_g=v_final_norm_g)
    weights = {n: given[n] for n in TWIN_WEIGHTS}
    shared = {n: given[n] for n in SHARED_INPUTS}
    per_example = {n: given[n] for n in ['x', 'mem']}
    grad_fn = _jax.value_and_grad(_loss, argnums=(0, 1))

    def one_microbatch(ex, loss_target):
        ex = dict(ex)
        diff = ex.pop(TWIN_DIFF_INPUT)
        return grad_fn(weights, diff, {**shared, **ex}, loss_target)

    if N_MICROBATCH == 1:
        loss, (grad_w, grad_x) = one_microbatch(per_example, given["loss_target"])
    else:
        def body(carry, xs):
            loss_sum, grad_sum = carry
            l_k, (gw_k, gx_k) = one_microbatch(xs[0], xs[1])
            with _jax.named_scope("update"):
                return (loss_sum + l_k, _jax.tree.map(_jnp.add, grad_sum, gw_k)), gx_k

        init = (_jnp.zeros((), _jnp.float32), _jax.tree.map(_jnp.zeros_like, weights))
        (loss, grad_w), grad_x = _jax.lax.scan(body, init, (per_example, given["loss_target"]))
    with _jax.named_scope("update"):
        delta_w, new_m, new_v = {}, {}, {}
        for n in TWIN_WEIGHTS:
            delta_w[n], new_m[n], new_v[n] = _adamw(weights[n], grad_w[n], given["m_" + n], given["v_" + n])
    return (loss, grad_x, *[grad_w[n] for n in TWIN_WEIGHTS], *[delta_w[n] for n in TWIN_WEIGHTS],
            *[new_m[n] for n in TWIN_WEIGHTS], *[new_v[n] for n in TWIN_WEIGHTS])
```

```python
import functools
import math

import jax
import jax.numpy as jnp
from jax import lax
from jax.experimental import pallas as pl
from jax.experimental.pallas import tpu as pltpu

F32 = jnp.float32
BF16 = jnp.bfloat16

N_DEV = 8
HEAD_DIM = 64
N_MIX_HEADS = 8
N_MEM_HEADS = 4
MIX_W = N_MIX_HEADS * HEAD_DIM
MEM_W = N_MEM_HEADS * HEAD_DIM
N_PAIRS = N_MIX_HEADS // 2
LANES = 128
BQ = 128
BK = 128
EPS = 1e-6
NEG_INF = -1e30
QK_SCALE = 1.0 / math.sqrt(HEAD_DIM)
N_KV_F = 8

ADAM_LR = 0.001
ADAM_B1 = 0.9
ADAM_B2 = 0.999
ADAM_EPS = 1e-08
ADAM_WD = 0.01
ADAM_STEP = 10

VMEM_LIMIT = 56 * 1024 * 1024
PACK_ROW_ALIGN = 16
PACK_BLOCK_ROWS = 1024

MESH = pl.DeviceIdType.MESH

SHARDED = (("w_in_a", 2), ("w_in_b", 1), ("w_mem_kv", 1), ("w_o", 2), ("w_mlp1", 2), ("w_mlp2", 1), ("w_kv_shared", 1))
REPLICATED = ("norm1_g", "mem_norm_g", "norm2_g", "kv_norm_g", "b_f", "final_norm_g")
WEIGHT_ORDER = ("norm1_g", "w_in_a", "w_in_b", "w_mem_kv", "mem_norm_g", "w_o", "norm2_g", "w_mlp1", "w_mlp2",
                "kv_norm_g", "w_kv_shared", "b_f", "final_norm_g")


def _cparams(sem=None):
    return pltpu.CompilerParams(dimension_semantics=sem, vmem_limit_bytes=VMEM_LIMIT)


def _pick(n, cands):
    for c in cands:
        if c <= n and n % c == 0:
            return c
    return n


def _dot(a, b, dims):
    return lax.dot_general(a, b, (dims, ((), ())), preferred_element_type=F32)


NN = ((1,), (0,))
NT = ((1,), (1,))
TN = ((0,), (0,))


def _mm(a, b, *, name, trans_b=False, out_dtype=F32, res=None, epilogue=None, u=None):
    m, k = a.shape
    n = b.shape[0] if trans_b else b.shape[1]
    tm = _pick(m, (1024, 512, 256, 128))
    tn = _pick(n, (896, 768, 512, 384, 256, 128))
    tk = _pick(k, (1024, 896, 768, 512, 256, 128))
    nk = k // tk

    def body(*refs):
        a_ref, b_ref = refs[0], refs[1]
        pos = 2
        res_ref = u_ref = None
        if res is not None:
            res_ref = refs[pos]
            pos += 1
        if u is not None:
            u_ref = refs[pos]
            pos += 1
        outs, acc_ref = refs[pos:-1], refs[-1]
        kk = pl.program_id(2)

        @pl.when(kk == 0)
        def _():
            acc_ref[...] = jnp.zeros_like(acc_ref)

        acc_ref[...] += _dot(a_ref[...].astype(BF16), b_ref[...].astype(BF16), NT if trans_b else NN)

        @pl.when(kk == nk - 1)
        def _():
            acc = acc_ref[...]
            if res_ref is not None:
                acc = res_ref[...] + acc
            if epilogue == "relu2":
                outs[0][...] = acc
                r = jnp.maximum(acc, 0.0)
                outs[1][...] = (r * r).astype(BF16)
            elif epilogue == "drelu2":
                outs[0][...] = (acc * (2.0 * jnp.maximum(u_ref[...], 0.0))).astype(out_dtype)
            else:
                outs[0][...] = acc.astype(out_dtype)

    in_specs = [pl.BlockSpec((tm, tk), lambda i, j, kk: (i, kk)),
                pl.BlockSpec((tn, tk), lambda i, j, kk: (j, kk)) if trans_b else pl.BlockSpec((tk, tn), lambda i, j, kk: (kk, j))]
    args = [a, b]
    tile = pl.BlockSpec((tm, tn), lambda i, j, kk: (i, j))
    if res is not None:
        in_specs.append(tile)
        args.append(res)
    if u is not None:
        in_specs.append(tile)
        args.append(u)
    if epilogue == "relu2":
        out_shape = (jax.ShapeDtypeStruct((m, n), F32), jax.ShapeDtypeStruct((m, n), BF16))
        out_specs = (tile, tile)
    else:
        out_shape = (jax.ShapeDtypeStruct((m, n), out_dtype),)
        out_specs = (tile,)
    outs = pl.pallas_call(
        body, name=name, grid=(m // tm, n // tn, nk), in_specs=in_specs, out_specs=out_specs, out_shape=out_shape,
        scratch_shapes=[pltpu.VMEM((tm, tn), F32)],
        compiler_params=_cparams(("parallel", "parallel", "arbitrary")),
    )(*args)
    return outs if epilogue == "relu2" else outs[0]


def _mm_tn(x, dy, *, name):
    m, k1 = x.shape
    n = dy.shape[1]
    t1 = _pick(k1, (1024, 896, 768, 512, 256, 128))
    tn = _pick(n, (1024, 896, 768, 512, 256, 128))
    tm = _pick(m, (512, 256, 128))
    nm = m // tm

    def body(x_ref, dy_ref, o_ref):
        mm = pl.program_id(2)

        @pl.when(mm == 0)
        def _():
            o_ref[...] = jnp.zeros_like(o_ref)

        o_ref[...] += _dot(x_ref[...].astype(BF16), dy_ref[...].astype(BF16), TN)

    return pl.pallas_call(
        body, name=name, grid=(k1 // t1, n // tn, nm),
        in_specs=[pl.BlockSpec((tm, t1), lambda i, j, mm: (mm, i)), pl.BlockSpec((tm, tn), lambda i, j, mm: (mm, j))],
        out_specs=pl.BlockSpec((t1, tn), lambda i, j, mm: (i, j)),
        out_shape=jax.ShapeDtypeStruct((k1, n), F32),
        compiler_params=_cparams(("parallel", "parallel", "arbitrary")),
    )(x, dy)


def _rmsnorm_fwd(x, g, *, name):
    s, d = x.shape
    tm = _pick(s, (512, 256, 128))

    def body(x_ref, g_ref, o_ref):
        xf = x_ref[...]
        r = lax.rsqrt(jnp.mean(xf * xf, axis=-1, keepdims=True) + EPS)
        o_ref[...] = (xf * r * g_ref[...]).astype(BF16)

    return pl.pallas_call(
        body, name=name, grid=(s // tm,),
        in_specs=[pl.BlockSpec((tm, d), lambda i: (i, 0)), pl.BlockSpec((1, d), lambda i: (0, 0))],
        out_specs=pl.BlockSpec((tm, d), lambda i: (i, 0)),
        out_shape=jax.ShapeDtypeStruct((s, d), BF16),
        compiler_params=_cparams(("parallel",)),
    )(x, g.reshape(1, d))


def _rmsnorm_bwd(dy, x, g, dres, *, name):
    s, d = x.shape
    tm = _pick(s, (256, 128))

    def body(*refs):
        if dres is None:
            dy_ref, x_ref, g_ref, dx_ref, dg_ref = refs
            dres_ref = None
        else:
            dy_ref, x_ref, g_ref, dres_ref, dx_ref, dg_ref = refs
        i = pl.program_id(0)
        xf = x_ref[...]
        dyv = dy_ref[...]
        r = lax.rsqrt(jnp.mean(xf * xf, axis=-1, keepdims=True) + EPS)
        xh = xf * r
        dyg = dyv * g_ref[...]
        dx = r * (dyg - xh * jnp.mean(dyg * xh, axis=-1, keepdims=True))
        dx_ref[...] = dx if dres_ref is None else dres_ref[...] + dx

        @pl.when(i == 0)
        def _():
            dg_ref[...] = jnp.zeros_like(dg_ref)

        dg_ref[...] += jnp.sum(dyv * xh, axis=0, keepdims=True)

    row = pl.BlockSpec((tm, d), lambda i: (i, 0))
    vec = pl.BlockSpec((1, d), lambda i: (0, 0))
    in_specs = [row, row, vec] + ([] if dres is None else [row])
    args = [dy, x, g.reshape(1, d)] + ([] if dres is None else [dres])
    return pl.pallas_call(
        body, name=name, grid=(s // tm,), in_specs=in_specs, out_specs=(row, vec),
        out_shape=(jax.ShapeDtypeStruct((s, d), F32), jax.ShapeDtypeStruct((1, d), F32)),
        compiler_params=_cparams(("arbitrary",)),
    )(*args)


def _final_loss(h, target, g, *, name):
    s, d = h.shape
    tm = _pick(s, (256, 128))

    def body(h_ref, t_ref, g_ref, loss_ref, dh_ref, dg_ref):
        i = pl.program_id(0)
        xf = h_ref[...]
        gv = g_ref[...]
        r = lax.rsqrt(jnp.mean(xf * xf, axis=-1, keepdims=True) + EPS)
        xh = xf * r
        err = xh * gv - t_ref[...]
        part = 0.5 * jnp.sum(jnp.mean(err * err, axis=-1, keepdims=True), axis=0, keepdims=True)
        dyv = err * (1.0 / d)
        dyg = dyv * gv
        dh_ref[...] = r * (dyg - xh * jnp.mean(dyg * xh, axis=-1, keepdims=True))

        @pl.when(i == 0)
        def _():
            dg_ref[...] = jnp.zeros_like(dg_ref)
            loss_ref[...] = jnp.zeros_like(loss_ref)

        dg_ref[...] += jnp.sum(dyv * xh, axis=0, keepdims=True)
        loss_ref[...] += jnp.broadcast_to(part, loss_ref.shape)

    row = pl.BlockSpec((tm, d), lambda i: (i, 0))
    vec = pl.BlockSpec((1, d), lambda i: (0, 0))
    return pl.pallas_call(
        body, name=name, grid=(s // tm,), in_specs=[row, row, vec],
        out_specs=(pl.BlockSpec((1, LANES), lambda i: (0, 0)), row, vec),
        out_shape=(jax.ShapeDtypeStruct((1, LANES), F32), jax.ShapeDtypeStruct((s, d), F32), jax.ShapeDtypeStruct((1, d), F32)),
        compiler_params=_cparams(("arbitrary",)),
    )(h, target, g.reshape(1, d))


def _head_lanes(rows):
    lane = lax.broadcasted_iota(jnp.int32, (rows, LANES), 1)
    return [lane < HEAD_DIM, lane >= HEAD_DIM]


def _tri(cmp):
    row = lax.broadcasted_iota(jnp.int32, (BK, BK), 0)
    col = lax.broadcasted_iota(jnp.int32, (BK, BK), 1)
    return cmp(row, col)


def _split_dot(x, tri_bf16):
    hi = x.astype(BF16)
    lo = (x - hi.astype(F32)).astype(BF16)
    return _dot(hi, tri_bf16, NN) + _dot(lo, tri_bf16, NN)


def _log_one_minus_sigmoid(z):
    e = jnp.exp(-jnp.abs(z))
    return -(jnp.maximum(z, 0.0) + jnp.log1p(e)), e


def _kv_rows(ref, j):
    return ref[pl.ds(pl.multiple_of(j * BK, BK), BK), :]


def _sb_fwd(proj, *, name):
    s = proj.shape[0]
    nq = s // BQ

    def body(q_ref, k_ref, v_ref, o_ref, tot_ref):
        i = pl.program_id(1)
        heads = _head_lanes(BQ)
        suffix = _tri(lambda r, c: r > c).astype(BF16)
        strict = _tri(lambda r, c: c < r)
        qv = q_ref[...] * QK_SCALE
        o_sel = None
        t_sel = None
        for hh in range(2):
            qm = jnp.where(heads[hh], qv, jnp.zeros_like(qv))

            def tile(j, a, acc, masked):
                z = _dot(qm, _kv_rows(k_ref, j), NT)
                lom, _ = _log_one_minus_sigmoid(z)
                if masked:
                    lom = jnp.where(strict, lom, 0.0)
                rsum = _split_dot(lom, suffix)
                w = jnp.exp((lom + z) + (a + rsum))
                if masked:
                    w = jnp.where(strict, w, 0.0)
                acc = acc + _dot(w.astype(BF16), _kv_rows(v_ref, j), NN)
                return a + rsum[:, 0:1] + lom[:, 0:1], acc

            a0 = jnp.zeros((BQ, 1), F32)
            acc0 = jnp.zeros((BQ, LANES), F32)
            a1, acc1 = tile(i, a0, acc0, True)

            def step(jj, carry):
                return tile(i - 1 - jj, carry[0], carry[1], False)

            a2, acc2 = lax.fori_loop(0, i, step, (a1, acc1))
            tb = jnp.broadcast_to(a2, (BQ, LANES))
            o_sel = acc2 if hh == 0 else jnp.where(heads[0], o_sel, acc2)
            t_sel = tb if hh == 0 else jnp.where(heads[0], t_sel, tb)
        o_ref[...] = o_sel.astype(BF16)
        tot_ref[0] = t_sel

    qblk = pl.BlockSpec((BQ, LANES), lambda p, i: (i, p))
    return pl.pallas_call(
        body, name=name, grid=(N_PAIRS, nq),
        in_specs=[qblk, pl.BlockSpec((s, LANES), lambda p, i: (0, N_PAIRS + p)), pl.BlockSpec((s, LANES), lambda p, i: (0, 2 * N_PAIRS + p))],
        out_specs=(qblk, pl.BlockSpec((1, BQ, LANES), lambda p, i: (p, i, 0))),
        out_shape=(jax.ShapeDtypeStruct((s, MIX_W), BF16), jax.ShapeDtypeStruct((N_PAIRS, s, LANES), F32)),
        compiler_params=_cparams(("parallel", "arbitrary")),
    )(proj, proj, proj)


def _sb_bwd(proj, dmerged, tot, *, name):
    s = proj.shape[0]
    nq = s // BQ

    def body(q_ref, k_ref, v_ref, do_ref, tot_ref, dq_ref, dk_ref, dv_ref):
        i = pl.program_id(1)

        @pl.when(i == 0)
        def _():
            dk_ref[...] = jnp.zeros_like(dk_ref)
            dv_ref[...] = jnp.zeros_like(dv_ref)

        heads = _head_lanes(BQ)
        incl = _tri(lambda r, c: r <= c).astype(BF16)
        excl = _tri(lambda r, c: r < c).astype(BF16)
        strict = _tri(lambda r, c: c < r)
        qv = q_ref[...] * QK_SCALE
        dov = do_ref[...]
        totv = tot_ref[0]
        dq_sel = None
        for hh in range(2):
            qm = jnp.where(heads[hh], qv, jnp.zeros_like(qv))
            dom = jnp.where(heads[hh], dov, jnp.zeros_like(dov))
            tot_h = totv[:, hh * HEAD_DIM:hh * HEAD_DIM + 1]

            def tile(j, pre, gpre, dq, masked):
                kb = _kv_rows(k_ref, j)
                vb = _kv_rows(v_ref, j)
                z = _dot(qm, kb, NT)
                lom, e = _log_one_minus_sigmoid(z)
                rinv = 1.0 / (1.0 + e)
                pos = z >= 0.0
                sig = jnp.where(pos, 1.0, e) * rinv
                oms = jnp.where(pos, e, 1.0) * rinv
                if masked:
                    lom = jnp.where(strict, lom, 0.0)
                pin = _split_dot(lom, incl)
                w = jnp.exp((lom + z) + (tot_h - (pre + pin)))
                if masked:
                    w = jnp.where(strict, w, 0.0)
                gw = _dot(dom, vb, NT) * w
                gex = _split_dot(gw, excl)
                dz = gw * oms - sig * (gpre + gex)
                if masked:
                    dz = jnp.where(strict, dz, 0.0)
                dzb = dz.astype(BF16)
                dq = dq + _dot(dzb, kb, NN)
                rows = pl.ds(pl.multiple_of(j * BK, BK), BK)
                dk_ref[rows, :] += _dot(dzb, qm, TN)
                dv_ref[rows, :] += _dot(w.astype(BF16), dom, TN)
                last = slice(BK - 1, BK)
                return pre + pin[:, last], gpre + gex[:, last] + gw[:, last], dq

            def step(j, carry):
                return tile(j, carry[0], carry[1], carry[2], False)

            zero = jnp.zeros((BQ, 1), F32)
            pre, gpre, dq = lax.fori_loop(0, i, step, (zero, zero, jnp.zeros((BQ, LANES), F32)))
            _, _, dq = tile(i, pre, gpre, dq, True)
            dq_sel = dq if hh == 0 else jnp.where(heads[0], dq_sel, dq)
        dq_ref[...] = (dq_sel * QK_SCALE).astype(BF16)

    qblk = pl.BlockSpec((BQ, LANES), lambda p, i: (i, p))
    full = pl.BlockSpec((s, LANES), lambda p, i: (0, p))
    return pl.pallas_call(
        body, name=name, grid=(N_PAIRS, nq),
        in_specs=[qblk, pl.BlockSpec((s, LANES), lambda p, i: (0, N_PAIRS + p)), pl.BlockSpec((s, LANES), lambda p, i: (0, 2 * N_PAIRS + p)),
                  qblk, pl.BlockSpec((1, BQ, LANES), lambda p, i: (p, i, 0))],
        out_specs=(qblk, full, full),
        out_shape=(jax.ShapeDtypeStruct((s, MIX_W), BF16), jax.ShapeDtypeStruct((s, MIX_W), F32), jax.ShapeDtypeStruct((s, MIX_W), F32)),
        compiler_params=_cparams(("parallel", "arbitrary")),
    )(proj, proj, proj, dmerged, tot)


def _fox_fwd(proj, kv, c_col, c_row, *, name):
    s = proj.shape[0]
    nq = s // BQ

    def body(q_ref, k_ref, v_ref, cc_ref, cr_ref, o_ref, lse_ref):
        i = pl.program_id(1)
        heads = _head_lanes(BQ)
        causal = _tri(lambda r, c: c <= r)
        qv = q_ref[...] * QK_SCALE
        ccv = cc_ref[0]
        o_sel = None
        l_sel = None
        for hh in range(2):
            qm = jnp.where(heads[hh], qv, jnp.zeros_like(qv))
            c_t = ccv[:, hh * HEAD_DIM:hh * HEAD_DIM + 1]

            def tile(j, m, l, acc, masked):
                c_s = cr_ref[0, hh:hh + 1, pl.ds(pl.multiple_of(j * BK, BK), BK)]
                sc = _dot(qm, _kv_rows(k_ref, j), NT) + (c_t - c_s)
                if masked:
                    sc = jnp.where(causal, sc, NEG_INF)
                m_new = jnp.maximum(m, jnp.max(sc, axis=1, keepdims=True))
                p = jnp.exp(sc - m_new)
                alpha = jnp.exp(m - m_new)
                l = alpha * l + jnp.sum(p, axis=1, keepdims=True)
                acc = alpha * acc + _dot(p.astype(BF16), _kv_rows(v_ref, j), NN)
                return m_new, l, acc

            def step(j, carry):
                return tile(j, carry[0], carry[1], carry[2], False)

            init = (jnp.full((BQ, 1), NEG_INF, F32), jnp.zeros((BQ, 1), F32), jnp.zeros((BQ, LANES), F32))
            m, l, acc = lax.fori_loop(0, i, step, init)
            m, l, acc = tile(i, m, l, acc, True)
            out = acc * (1.0 / l)
            lb = jnp.broadcast_to(m + jnp.log(l), (BQ, LANES))
            o_sel = out if hh == 0 else jnp.where(heads[0], o_sel, out)
            l_sel = lb if hh == 0 else jnp.where(heads[0], l_sel, lb)
        o_ref[...] = o_sel.astype(BF16)
        lse_ref[0] = l_sel

    qblk = pl.BlockSpec((BQ, LANES), lambda p, i: (i, p))
    stat = pl.BlockSpec((1, BQ, LANES), lambda p, i: (p, i, 0))
    return pl.pallas_call(
        body, name=name, grid=(N_PAIRS, nq),
        in_specs=[qblk, pl.BlockSpec((s, LANES), lambda p, i: (0, p)), pl.BlockSpec((s, LANES), lambda p, i: (0, N_PAIRS + p)),
                  stat, pl.BlockSpec((1, 8, s), lambda p, i: (p, 0, 0))],
        out_specs=(qblk, stat),
        out_shape=(jax.ShapeDtypeStruct((s, MIX_W), BF16), jax.ShapeDtypeStruct((N_PAIRS, s, LANES), F32)),
        compiler_params=_cparams(("parallel", "arbitrary")),
    )(proj, kv, kv, c_col, c_row)


def _fox_bwd(proj, kv, c_col, c_row, merged, dmerged, lse, *, name):
    s = proj.shape[0]
    nq = s // BQ

    def body(q_ref, k_ref, v_ref, cc_ref, cr_ref, o_ref, do_ref, lse_ref, dq_ref, dk_ref, dv_ref, dc_ref, dcq_ref):
        i = pl.program_id(1)

        @pl.when(i == 0)
        def _():
            dk_ref[...] = jnp.zeros_like(dk_ref)
            dv_ref[...] = jnp.zeros_like(dv_ref)
            dc_ref[...] = jnp.zeros_like(dc_ref)

        heads = _head_lanes(BQ)
        causal = _tri(lambda r, c: c <= r)
        qv = q_ref[...] * QK_SCALE
        dov = do_ref[...]
        prod = dov.astype(F32) * o_ref[...].astype(F32)
        ccv = cc_ref[0]
        lsev = lse_ref[0]
        dq_sel = None
        r_sel = None
        for hh in range(2):
            qm = jnp.where(heads[hh], qv, jnp.zeros_like(qv))
            dom = jnp.where(heads[hh], dov, jnp.zeros_like(dov))
            c_t = ccv[:, hh * HEAD_DIM:hh * HEAD_DIM + 1]
            lse_t = lsev[:, hh * HEAD_DIM:hh * HEAD_DIM + 1]
            delta = jnp.sum(jnp.where(heads[hh], prod, 0.0), axis=1, keepdims=True)

            def tile(j, dq, rsum, masked):
                cols = pl.ds(pl.multiple_of(j * BK, BK), BK)
                kb = _kv_rows(k_ref, j)
                vb = _kv_rows(v_ref, j)
                c_s = cr_ref[0, hh:hh + 1, cols]
                sc = _dot(qm, kb, NT) + (c_t - c_s)
                p = jnp.exp(sc - lse_t)
                if masked:
                    p = jnp.where(causal, p, 0.0)
                ds = p * (_dot(dom, vb, NT) - delta)
                dsb = ds.astype(BF16)
                dq = dq + _dot(dsb, kb, NN)
                dk_ref[cols, :] += _dot(dsb, qm, TN)
                dv_ref[cols, :] += _dot(p.astype(BF16), dom, TN)
                dc_ref[0, hh:hh + 1, cols] -= jnp.sum(ds, axis=0, keepdims=True)
                return dq, rsum + jnp.sum(ds, axis=1, keepdims=True)

            init = (jnp.zeros((BQ, LANES), F32), jnp.zeros((BQ, 1), F32))
            dq, rsum = lax.fori_loop(0, i, lambda j, carry: tile(j, carry[0], carry[1], False), init)
            dq, rsum = tile(i, dq, rsum, True)
            rb = jnp.broadcast_to(rsum, (BQ, LANES))
            dq_sel = dq if hh == 0 else jnp.where(heads[0], dq_sel, dq)
            r_sel = rb if hh == 0 else jnp.where(heads[0], r_sel, rb)
        dq_ref[...] = (dq_sel * QK_SCALE).astype(BF16)
        dcq_ref[0] = r_sel

    qblk = pl.BlockSpec((BQ, LANES), lambda p, i: (i, p))
    stat = pl.BlockSpec((1, BQ, LANES), lambda p, i: (p, i, 0))
    crow = pl.BlockSpec((1, 8, s), lambda p, i: (p, 0, 0))
    full = pl.BlockSpec((s, LANES), lambda p, i: (0, p))
    return pl.pallas_call(
        body, name=name, grid=(N_PAIRS, nq),
        in_specs=[qblk, full, pl.BlockSpec((s, LANES), lambda p, i: (0, N_PAIRS + p)), stat, crow, qblk, qblk, stat],
        out_specs=(qblk, full, full, crow, stat),
        out_shape=(jax.ShapeDtypeStruct((s, MIX_W), BF16), jax.ShapeDtypeStruct((s, MIX_W), F32), jax.ShapeDtypeStruct((s, MIX_W), F32),
                   jax.ShapeDtypeStruct((N_PAIRS, 8, s), F32), jax.ShapeDtypeStruct((N_PAIRS, s, LANES), F32)),
        compiler_params=_cparams(("parallel", "arbitrary")),
    )(proj, kv, kv, c_col, c_row, merged, dmerged, lse)


MEM_TQ = 256


def _mem_fwd(proj, q_col_block, mkv, *, name):
    s = proj.shape[0]
    tq = _pick(s, (MEM_TQ, 128))
    n_mem = mkv.shape[0]

    def body(q_ref, mkv_ref, o_ref):
        heads = _head_lanes(tq)
        for pp in range(MEM_W // LANES):
            cols = slice(pp * LANES, (pp + 1) * LANES)
            qv = q_ref[:, cols] * QK_SCALE
            mk = mkv_ref[:, pp * LANES:(pp + 1) * LANES]
            mv = mkv_ref[:, MEM_W + pp * LANES:MEM_W + (pp + 1) * LANES]
            o_sel = None
            for hh in range(2):
                qm = jnp.where(heads[hh], qv, jnp.zeros_like(qv))
                sc = _dot(qm, mk, NT)
                p = jnp.exp(sc - jnp.max(sc, axis=1, keepdims=True))
                p = p / jnp.sum(p, axis=1, keepdims=True)
                out = _dot(p.astype(BF16), mv, NN)
                o_sel = out if hh == 0 else jnp.where(heads[0], o_sel, out)
            o_ref[:, cols] = o_sel.astype(BF16)

    return pl.pallas_call(
        body, name=name, grid=(s // tq,),
        in_specs=[pl.BlockSpec((tq, MEM_W), lambda i: (i, q_col_block)), pl.BlockSpec((n_mem, 2 * MEM_W), lambda i: (0, 0))],
        out_specs=pl.BlockSpec((tq, MEM_W), lambda i: (i, 0)),
        out_shape=jax.ShapeDtypeStruct((s, MEM_W), BF16),
        compiler_params=_cparams(("parallel",)),
    )(proj, mkv)


def _mem_bwd(proj, q_col_block, mkv, dmerged, *, name):
    s = proj.shape[0]
    tq = _pick(s, (MEM_TQ, 128))
    n_mem = mkv.shape[0]

    def body(q_ref, mkv_ref, do_ref, dq_ref, dmkv_ref):
        i = pl.program_id(0)

        @pl.when(i == 0)
        def _():
            dmkv_ref[...] = jnp.zeros_like(dmkv_ref)

        heads = _head_lanes(tq)
        for pp in range(MEM_W // LANES):
            cols = slice(pp * LANES, (pp + 1) * LANES)
            vcols = slice(MEM_W + pp * LANES, MEM_W + (pp + 1) * LANES)
            qv = q_ref[:, cols] * QK_SCALE
            dov = do_ref[:, cols]
            mk = mkv_ref[:, cols]
            mv = mkv_ref[:, vcols]
            dq_sel = None
            for hh in range(2):
                qm = jnp.where(heads[hh], qv, jnp.zeros_like(qv))
                dom = jnp.where(heads[hh], dov, jnp.zeros_like(dov))
                sc = _dot(qm, mk, NT)
                p = jnp.exp(sc - jnp.max(sc, axis=1, keepdims=True))
                p = p / jnp.sum(p, axis=1, keepdims=True)
                dp = _dot(dom, mv, NT)
                ds = p * (dp - jnp.sum(p * dp, axis=1, keepdims=True))
                dsb = ds.astype(BF16)
                dq = _dot(dsb, mk, NN)
                dmkv_ref[:, cols] += _dot(dsb, qm, TN)
                dmkv_ref[:, vcols] += _dot(p.astype(BF16), dom, TN)
                dq_sel = dq if hh == 0 else jnp.where(heads[0], dq_sel, dq)
            dq_ref[:, cols] = (dq_sel * QK_SCALE).astype(BF16)

    return pl.pallas_call(
        body, name=name, grid=(s // tq,),
        in_specs=[pl.BlockSpec((tq, MEM_W), lambda i: (i, q_col_block)), pl.BlockSpec((n_mem, 2 * MEM_W), lambda i: (0, 0)),
                  pl.BlockSpec((tq, MEM_W), lambda i: (i, MIX_W // MEM_W))],
        out_specs=(pl.BlockSpec((tq, MEM_W), lambda i: (i, 0)), pl.BlockSpec((n_mem, 2 * MEM_W), lambda i: (0, 0))),
        out_shape=(jax.ShapeDtypeStruct((s, MEM_W), BF16), jax.ShapeDtypeStruct((n_mem, 2 * MEM_W), F32)),
        compiler_params=_cparams(("arbitrary",)),
    )(proj, mkv, dmerged)


GATE_TB = 256


def _split3_dot(tri_bf16, x):
    x1 = x.astype(BF16)
    r1 = x - x1.astype(F32)
    x2 = r1.astype(BF16)
    x3 = (r1 - x2.astype(F32)).astype(BF16)
    return _dot(tri_bf16, x1, NN) + _dot(tri_bf16, x2, NN) + _dot(tri_bf16, x3, NN)


def _gate_fwd(f, b, *, name):
    s = f.shape[0]
    tb = _pick(s, (GATE_TB, 128))

    def body(f_ref, b_ref, c_ref, carry_ref):
        i = pl.program_id(0)

        @pl.when(i == 0)
        def _():
            carry_ref[...] = jnp.zeros_like(carry_ref)

        x = f_ref[...] + b_ref[...]
        lf = jnp.minimum(x, 0.0) - jnp.log1p(jnp.exp(-jnp.abs(x)))
        row = lax.broadcasted_iota(jnp.int32, (tb, tb), 0)
        col = lax.broadcasted_iota(jnp.int32, (tb, tb), 1)
        lower = (col <= row).astype(BF16)
        c = carry_ref[...] + _split3_dot(lower, lf)
        c_ref[...] = c
        carry_ref[...] = c[tb - 1:tb, :]

    return pl.pallas_call(
        body, name=name, grid=(s // tb,),
        in_specs=[pl.BlockSpec((tb, LANES), lambda i: (i, 0)), pl.BlockSpec((1, LANES), lambda i: (0, 0))],
        out_specs=pl.BlockSpec((tb, LANES), lambda i: (i, 0)),
        out_shape=jax.ShapeDtypeStruct((s, LANES), F32),
        scratch_shapes=[pltpu.VMEM((1, LANES), F32)],
        compiler_params=_cparams(("arbitrary",)),
    )(f, b)


def _gate_bwd(f, b, dc, *, name):
    s = f.shape[0]
    tb = _pick(s, (GATE_TB, 128))
    nb = s // tb

    def body(f_ref, b_ref, dc_ref, df_ref, db_ref, carry_ref):
        i = pl.program_id(0)

        @pl.when(i == 0)
        def _():
            carry_ref[...] = jnp.zeros_like(carry_ref)
            db_ref[...] = jnp.zeros_like(db_ref)

        row = lax.broadcasted_iota(jnp.int32, (tb, tb), 0)
        col = lax.broadcasted_iota(jnp.int32, (tb, tb), 1)
        upper = (col >= row).astype(BF16)
        dlf = carry_ref[...] + _split3_dot(upper, dc_ref[...])
        carry_ref[...] = dlf[0:1, :]
        x = f_ref[...] + b_ref[...]
        e = jnp.exp(-jnp.abs(x))
        one_minus_sig = jnp.where(x >= 0.0, e, 1.0) / (1.0 + e)
        df = dlf * one_minus_sig
        df_ref[...] = df
        db_ref[...] += jnp.sum(df, axis=0, keepdims=True)

    rev = pl.BlockSpec((tb, LANES), lambda i: (nb - 1 - i, 0))
    vec = pl.BlockSpec((1, LANES), lambda i: (0, 0))
    return pl.pallas_call(
        body, name=name, grid=(nb,), in_specs=[rev, vec, rev], out_specs=(rev, vec),
        out_shape=(jax.ShapeDtypeStruct((s, LANES), F32), jax.ShapeDtypeStruct((1, LANES), F32)),
        scratch_shapes=[pltpu.VMEM((1, LANES), F32)],
        compiler_params=_cparams(("arbitrary",)),
    )(f, b, dc)


def _all_gather(x, *, name):
    r, cdim = x.shape

    def body(x_ref, out_ref, send_sems, recv_sems, local_sem):
        mx, my, mc = lax.axis_index("x"), lax.axis_index("y"), lax.axis_index("c")
        me, sibling = (mx, my, mc), (mx, my, 1 - mc)
        chips = [(1 - mx, my), (mx, 1 - my), (1 - mx, 1 - my)]

        def rows(px, py, pc):
            return out_ref.at[pl.ds((4 * px + 2 * py + pc) * r, r), :]

        def copy(k, block, to, src=None):
            return pltpu.make_async_remote_copy(
                src_ref=rows(*block) if src is None else src, dst_ref=rows(*block),
                send_sem=send_sems.at[k], recv_sem=recv_sems.at[k], device_id=to, device_id_type=MESH)

        mine = pltpu.make_async_copy(x_ref, rows(*me), local_sem)
        mine.start()
        first = [copy(0, me, sibling, src=x_ref)]
        first += [copy(1 + j, me, (*chip, mc), src=x_ref) for j, chip in enumerate(chips)]
        for cp in first:
            cp.start()
        passed = [copy(4 + j, (*chip, mc), sibling) for j, chip in enumerate(chips)]
        for j, chip in enumerate(chips):
            copy(1 + j, (*chip, mc), me).wait_recv()
            passed[j].start()
        copy(0, sibling, me).wait_recv()
        for j, chip in enumerate(chips):
            copy(4 + j, (*chip, 1 - mc), me).wait_recv()
        for cp in first + passed:
            cp.wait_send()
        mine.wait()

    return pl.pallas_call(
        body, name=name,
        in_specs=[pl.BlockSpec(memory_space=pl.ANY)], out_specs=pl.BlockSpec(memory_space=pl.ANY),
        out_shape=jax.ShapeDtypeStruct((N_DEV * r, cdim), x.dtype),
        scratch_shapes=[pltpu.SemaphoreType.DMA((7,)), pltpu.SemaphoreType.DMA((7,)), pltpu.SemaphoreType.DMA],
    )(x)


def _exchange_partials(g, *, name):
    def body(g_ref, out_ref, send_sems, recv_sems, local_sem):
        mx, my, mc = lax.axis_index("x"), lax.axis_index("y"), lax.axis_index("c")
        me = 4 * mx + 2 * my + mc
        mine = pltpu.make_async_copy(g_ref.at[me], out_ref.at[me], local_sem)
        mine.start()
        copies = []
        for mask in range(1, N_DEV):
            px = 1 - mx if mask & 4 else mx
            py = 1 - my if mask & 2 else my
            pc = 1 - mc if mask & 1 else mc
            cp = pltpu.make_async_remote_copy(
                src_ref=g_ref.at[4 * px + 2 * py + pc], dst_ref=out_ref.at[me],
                send_sem=send_sems.at[mask - 1], recv_sem=recv_sems.at[mask - 1], device_id=(px, py, pc), device_id_type=MESH)
            cp.start()
            copies.append(cp)
        for cp in copies:
            cp.wait()
        mine.wait()

    return pl.pallas_call(
        body, name=name,
        in_specs=[pl.BlockSpec(memory_space=pl.ANY)], out_specs=pl.BlockSpec(memory_space=pl.ANY),
        out_shape=jax.ShapeDtypeStruct(g.shape, g.dtype),
        scratch_shapes=[pltpu.SemaphoreType.DMA((7,)), pltpu.SemaphoreType.DMA((7,)), pltpu.SemaphoreType.DMA],
    )(g)


def _adamw(parts, w, m, v, *, name):
    r, cdim = w.shape
    tr = _pick(r, (PACK_BLOCK_ROWS, 512, 256, 128, 64, 32, 16, 8))
    c1 = 1.0 / (1.0 - ADAM_B1 ** ADAM_STEP)
    c2 = 1.0 / (1.0 - ADAM_B2 ** ADAM_STEP)

    def body(p_ref, w_ref, m_ref, v_ref, g_ref, d_ref, nm_ref, nv_ref):
        g = p_ref[0]
        for dev in range(1, N_DEV):
            g = g + p_ref[dev]
        mn = ADAM_B1 * m_ref[...] + (1.0 - ADAM_B1) * g
        vn = ADAM_B2 * v_ref[...] + (1.0 - ADAM_B2) * (g * g)
        g_ref[...] = g
        nm_ref[...] = mn
        nv_ref[...] = vn
        d_ref[...] = -ADAM_LR * ((mn * c1) / (jnp.sqrt(vn * c2) + ADAM_EPS) + ADAM_WD * w_ref[...])

    blk = pl.BlockSpec((tr, cdim), lambda i: (i, 0))
    shape = jax.ShapeDtypeStruct((r, cdim), F32)
    return pl.pallas_call(
        body, name=name, grid=(r // tr,),
        in_specs=[pl.BlockSpec((N_DEV, tr, cdim), lambda i: (0, i, 0)), blk, blk, blk],
        out_specs=(blk, blk, blk, blk), out_shape=(shape, shape, shape, shape),
        compiler_params=_cparams(("parallel",)),
    )(parts, w, m, v)


def _rows_of(shape):
    n = math.prod(shape)
    assert n % LANES == 0, shape
    rows = n // LANES
    return -(-rows // PACK_ROW_ALIGN) * PACK_ROW_ALIGN


def _layout(shard_shapes):
    out, off = {}, 0
    for name, _ in SHARDED:
        rows = _rows_of(shard_shapes[name])
        out[name] = (off, rows, tuple(shard_shapes[name]))
        off += rows
    return out, -(-off // PACK_BLOCK_ROWS) * PACK_BLOCK_ROWS


def _pack_shards(layout, total, arrays, dtype):
    parts = []
    for name, _ in SHARDED:
        _, rows, _ = layout[name]
        flat = arrays[name].astype(dtype).reshape(-1, LANES)
        parts.append(jnp.pad(flat, ((0, rows - flat.shape[0]), (0, 0))))
    used = sum(p.shape[0] for p in parts)
    if total > used:
        parts.append(jnp.zeros((total - used, LANES), dtype))
    return jnp.concatenate(parts, axis=0)


def _unpack_shard(layout, flat, name):
    off, _, shape = layout[name]
    n = math.prod(shape) // LANES
    return flat[off:off + n].reshape(shape)


def _unpack_full(layout, gathered, name, axis):
    off, _, shape = layout[name]
    n = math.prod(shape) // LANES
    blocks = gathered[:, off:off + n].reshape((N_DEV,) + shape)
    blocks = jnp.moveaxis(blocks, 0, axis)
    return blocks.reshape(shape[:axis] + (N_DEV * shape[axis],) + shape[axis + 1:])


def _pack_full(layout, total, grads):
    parts = []
    for name, axis in SHARDED:
        _, rows, shape = layout[name]
        g = grads[name]
        blocks = g.reshape(shape[:axis] + (N_DEV, shape[axis]) + shape[axis + 1:])
        blocks = jnp.moveaxis(blocks, axis, 0).reshape(N_DEV, -1, LANES)
        parts.append(jnp.pad(blocks, ((0, 0), (0, rows - blocks.shape[1]), (0, 0))))
    used = sum(p.shape[1] for p in parts)
    if total > used:
        parts.append(jnp.zeros((N_DEV, total - used, LANES), F32))
    return jnp.concatenate(parts, axis=1)


def _pad_lanes(a):
    return jnp.pad(a, ((0, 0), (0, LANES - a.shape[1])))


def _pair_layouts(c):
    s = c.shape[0]
    by_pair = c.T.reshape(N_PAIRS, 2, s)
    c_col = jnp.repeat(by_pair.transpose(0, 2, 1), HEAD_DIM, axis=2)
    c_row = jnp.pad(by_pair, ((0, 0), (0, 6), (0, 0)))
    return c_col, c_row


def _forward_backward(x, mem, target, wts, small):
    n_a = wts["w_in_a"].shape[0]
    n_b = wts["w_in_b"].shape[0]
    depth = n_a + n_b
    w_kv = wts["w_kv_shared"]
    w_kv_kv = w_kv[:, :2 * MIX_W]
    w_kv_f = _pad_lanes(w_kv[:, 2 * MIX_W:])
    b_f = _pad_lanes(small["b_f"].reshape(1, -1))

    saved = []
    shared = None
    h = x
    for l in range(depth):
        is_a = l < n_a
        if l == n_a:
            hs = _rmsnorm_fwd(h, small["kv_norm_g"], name="kv_norm")
            kv = _mm(hs, w_kv_kv, name="kv_proj", out_dtype=BF16)
            f = _mm(hs, w_kv_f, name="gate_proj")
            c = _gate_fwd(f, b_f, name="gate_cumsum")
            c_col, c_row = _pair_layouts(c[:, :N_MIX_HEADS])
            shared = dict(h=h, hs=hs, kv=kv, f=f, c_col=c_col, c_row=c_row)
        hn = _rmsnorm_fwd(h, small["norm1_g"][l], name=f"norm1_{l}")
        memn = _rmsnorm_fwd(mem, small["mem_norm_g"][l], name=f"mem_norm_{l}")
        mkv = _mm(memn, wts["w_mem_kv"][l], name=f"mem_kv_{l}", out_dtype=BF16)
        if is_a:
            proj = _mm(hn, wts["w_in_a"][l], name=f"in_proj_{l}", out_dtype=BF16)
            mix, stat = _sb_fwd(proj, name=f"sb_fwd_{l}")
            q_block = 3 * MIX_W // MEM_W
        else:
            proj = _mm(hn, wts["w_in_b"][l - n_a], name=f"in_proj_{l}", out_dtype=BF16)
            mix, stat = _fox_fwd(proj, shared["kv"], shared["c_col"], shared["c_row"], name=f"fox_fwd_{l}")
            q_block = MIX_W // MEM_W
        mem_out = _mem_fwd(proj, q_block, mkv, name=f"mem_fwd_{l}")
        merged = jnp.concatenate([mix, mem_out], axis=1)
        h_mid = _mm(merged, wts["w_o"][l], name=f"o_proj_{l}", res=h)
        h2n = _rmsnorm_fwd(h_mid, small["norm2_g"][l], name=f"norm2_{l}")
        u, act = _mm(h2n, wts["w_mlp1"][l], name=f"mlp1_{l}", epilogue="relu2")
        h_out = _mm(act, wts["w_mlp2"][l], name=f"mlp2_{l}", res=h_mid)
        saved.append(dict(h=h, hn=hn, memn=memn, mkv=mkv, proj=proj, stat=stat, merged=merged, h_mid=h_mid, h2n=h2n, u=u, act=act,
                          q_block=q_block))
        h = h_out

    loss, dh, dg_final = _final_loss(h, target, small["final_norm_g"], name="final_loss")

    g_w = {k: [None] * wts[k].shape[0] for k in ("w_in_a", "w_in_b", "w_mem_kv", "w_o", "w_mlp1", "w_mlp2")}
    g_n = {k: [None] * depth for k in ("norm1_g", "mem_norm_g", "norm2_g")}
    dk_sh = dv_sh = dc_sh = dcq_sh = None
    for l in reversed(range(depth)):
        sv = saved[l]
        is_a = l < n_a
        du = _mm(dh, wts["w_mlp2"][l], name=f"d_act_{l}", trans_b=True, epilogue="drelu2", u=sv["u"], out_dtype=BF16)
        g_w["w_mlp2"][l] = _mm_tn(sv["act"], dh, name=f"dw_mlp2_{l}")
        g_w["w_mlp1"][l] = _mm_tn(sv["h2n"], du, name=f"dw_mlp1_{l}")
        dh2n = _mm(du, wts["w_mlp1"][l], name=f"d_h2n_{l}", trans_b=True)
        dh_mid, g_n["norm2_g"][l] = _rmsnorm_bwd(dh2n, sv["h_mid"], small["norm2_g"][l], dh, name=f"norm2_bwd_{l}")
        dmerged = _mm(dh_mid, wts["w_o"][l], name=f"d_merged_{l}", trans_b=True, out_dtype=BF16)
        g_w["w_o"][l] = _mm_tn(sv["merged"], dh_mid, name=f"dw_o_{l}")
        dqm, dmkv = _mem_bwd(sv["proj"], sv["q_block"], sv["mkv"], dmerged, name=f"mem_bwd_{l}")
        if is_a:
            dq, dk, dv = _sb_bwd(sv["proj"], dmerged, sv["stat"], name=f"sb_bwd_{l}")
            dproj = jnp.concatenate([dq, dk.astype(BF16), dv.astype(BF16), dqm], axis=1)
            w_in, key, idx = wts["w_in_a"][l], "w_in_a", l
        else:
            dq, dk, dv, dc, dcq = _fox_bwd(sv["proj"], shared["kv"], shared["c_col"], shared["c_row"], sv["merged"], dmerged, sv["stat"],
                                           name=f"fox_bwd_{l}")
            dk_sh = dk if dk_sh is None else dk_sh + dk
            dv_sh = dv if dv_sh is None else dv_sh + dv
            dc_sh = dc if dc_sh is None else dc_sh + dc
            dcq_sh = dcq if dcq_sh is None else dcq_sh + dcq
            dproj = jnp.concatenate([dq, dqm], axis=1)
            w_in, key, idx = wts["w_in_b"][l - n_a], "w_in_b", l - n_a
        g_w[key][idx] = _mm_tn(sv["hn"], dproj, name=f"dw_in_{l}")
        dhn = _mm(dproj, w_in, name=f"d_hn_{l}", trans_b=True)
        dh, g_n["norm1_g"][l] = _rmsnorm_bwd(dhn, sv["h"], small["norm1_g"][l], dh_mid, name=f"norm1_bwd_{l}")
        g_w["w_mem_kv"][l] = _mm_tn(sv["memn"], dmkv, name=f"dw_mem_kv_{l}")
        dmemn = _mm(dmkv, wts["w_mem_kv"][l], name=f"d_memn_{l}", trans_b=True)
        _, g_n["mem_norm_g"][l] = _rmsnorm_bwd(dmemn, mem, small["mem_norm_g"][l], None, name=f"mem_norm_bwd_{l}")
        if l == n_a:
            s_len = x.shape[0]
            dc_query = dcq_sh[:, :, ::HEAD_DIM].transpose(1, 0, 2).reshape(s_len, N_MIX_HEADS)
            dc_tok = _pad_lanes(dc_sh[:, :2, :].reshape(N_MIX_HEADS, s_len).T + dc_query)
            df, db = _gate_bwd(shared["f"], b_f, dc_tok, name="gate_bwd")
            dkv = jnp.concatenate([dk_sh, dv_sh], axis=1)
            dw_kv_kv = _mm_tn(shared["hs"], dkv, name="dw_kv")
            dw_kv_f = _mm_tn(shared["hs"], df, name="dw_gate")
            dhs = _mm(dkv, w_kv_kv, name="d_hs_kv", trans_b=True)
            dhs = _mm(df, w_kv_f, name="d_hs_gate", trans_b=True, res=dhs)
            dh, dg_kv = _rmsnorm_bwd(dhs, shared["h"], small["kv_norm_g"], dh, name="kv_norm_bwd")
            g_kv = jnp.concatenate([dw_kv_kv, dw_kv_f[:, :N_KV_F]], axis=1)

    grads = {k: jnp.stack(v) for k, v in g_w.items()}
    grads["w_kv_shared"] = g_kv
    d = x.shape[1]
    small_g = dict(
        norm1_g=jnp.concatenate(g_n["norm1_g"], axis=0), mem_norm_g=jnp.concatenate(g_n["mem_norm_g"], axis=0),
        norm2_g=jnp.concatenate(g_n["norm2_g"], axis=0), kv_norm_g=dg_kv.reshape(d), b_f=db[0, :N_KV_F], final_norm_g=dg_final.reshape(d))
    return loss, dh, grads, small_g


def _pack_small(vals, d):
    rows = []
    for name in REPLICATED:
        a = vals[name].astype(F32)
        if name == "b_f":
            a = jnp.pad(a, (0, d - a.shape[0]))
        rows.append(a.reshape(-1, d))
    packed = jnp.concatenate(rows, axis=0)
    pad = -packed.shape[0] % 8
    return jnp.pad(packed, ((0, pad), (0, 0)))


def _unpack_small(packed, shapes):
    out, off = {}, 0
    for name in REPLICATED:
        shape = shapes[name]
        if name == "b_f":
            out[name] = packed[off, :shape[0]]
            off += 1
        else:
            n = math.prod(shape) // packed.shape[1]
            out[name] = packed[off:off + n].reshape(shape)
            off += n
    return out


def kernel(x, mem, norm1_g, w_in_a, w_in_b, w_mem_kv, mem_norm_g, w_o, norm2_g, w_mlp1, w_mlp2, kv_norm_g, w_kv_shared, b_f, final_norm_g, loss_target, m_norm1_g, m_w_in_a, m_w_in_b, m_w_mem_kv, m_mem_norm_g, m_w_o, m_norm2_g, m_w_mlp1, m_w_mlp2, m_kv_norm_g, m_w_kv_shared, m_b_f, m_final_norm_g, v_norm1_g, v_w_in_a, v_w_in_b, v_w_mem_kv, v_mem_norm_g, v_w_o, v_norm2_g, v_w_mlp1, v_w_mlp2, v_kv_norm_g, v_w_kv_shared, v_b_f, v_final_norm_g):
    w = dict(norm1_g=norm1_g, w_in_a=w_in_a, w_in_b=w_in_b, w_mem_kv=w_mem_kv, mem_norm_g=mem_norm_g, w_o=w_o, norm2_g=norm2_g,
             w_mlp1=w_mlp1, w_mlp2=w_mlp2, kv_norm_g=kv_norm_g, w_kv_shared=w_kv_shared, b_f=b_f, final_norm_g=final_norm_g)
    m = dict(norm1_g=m_norm1_g, w_in_a=m_w_in_a, w_in_b=m_w_in_b, w_mem_kv=m_w_mem_kv, mem_norm_g=m_mem_norm_g, w_o=m_w_o,
             norm2_g=m_norm2_g, w_mlp1=m_w_mlp1, w_mlp2=m_w_mlp2, kv_norm_g=m_kv_norm_g, w_kv_shared=m_w_kv_shared, b_f=m_b_f,
             final_norm_g=m_final_norm_g)
    v = dict(norm1_g=v_norm1_g, w_in_a=v_w_in_a, w_in_b=v_w_in_b, w_mem_kv=v_w_mem_kv, mem_norm_g=v_mem_norm_g, w_o=v_w_o,
             norm2_g=v_norm2_g, w_mlp1=v_w_mlp1, w_mlp2=v_w_mlp2, kv_norm_g=v_kv_norm_g, w_kv_shared=v_w_kv_shared, b_f=v_b_f,
             final_norm_g=v_final_norm_g)
    d = x.shape[-1]
    layout, total = _layout({name: w[name].shape for name, _ in SHARDED})

    gathered = _all_gather(_pack_shards(layout, total, w, BF16), name="gather_weights").reshape(N_DEV, total, LANES)
    wts = {name: _unpack_full(layout, gathered, name, axis) for name, axis in SHARDED}
    small = {name: w[name] for name in REPLICATED}

    loss, grad_x, grads, small_g = _forward_backward(x[0], mem[0], loss_target[0], wts, small)

    parts = _exchange_partials(_pack_full(layout, total, grads), name="exchange_grads")
    g_flat, d_flat, m_flat, v_flat = _adamw(parts, _pack_shards(layout, total, w, F32), _pack_shards(layout, total, m, F32),
                                            _pack_shards(layout, total, v, F32), name="adamw_sharded")

    small_packed = _pack_small(small_g, d)
    n_small = small_packed.shape[0]
    small_parts = _all_gather(small_packed, name="gather_small_grads").reshape(N_DEV, n_small, d)
    gs, ds_, ms, vs = _adamw(small_parts, _pack_small(w, d), _pack_small(m, d), _pack_small(v, d), name="adamw_replicated")

    shapes = {name: w[name].shape for name in REPLICATED}
    out_g, out_d, out_m, out_v = {}, {}, {}, {}
    for flat, small_flat, out in ((g_flat, gs, out_g), (d_flat, ds_, out_d), (m_flat, ms, out_m), (v_flat, vs, out_v)):
        for name, _ in SHARDED:
            out[name] = _unpack_shard(layout, flat, name)
        out.update(_unpack_small(small_flat, shapes))

    loss_total = lax.psum(loss[0, 0], ("x", "y", "c"))
    return (loss_total, grad_x[None], *[out_g[n] for n in WEIGHT_ORDER], *[out_d[n] for n in WEIGHT_ORDER],
            *[out_m[n] for n in WEIGHT_ORDER], *[out_v[n] for n in WEIGHT_ORDER])
```

```python
import functools
import math

import jax
import jax.numpy as jnp
from jax import lax
from jax.experimental import pallas as pl
from jax.experimental.pallas import tpu as pltpu

F32 = jnp.float32
BF16 = jnp.bfloat16

N_DEV = 8
HEAD_DIM = 64
N_MIX_HEADS = 8
N_MEM_HEADS = 4
MIX_W = N_MIX_HEADS * HEAD_DIM
MEM_W = N_MEM_HEADS * HEAD_DIM
N_PAIRS = N_MIX_HEADS // 2
LANES = 128
ATT_B = 512
CUM_SUB = 256
EPS = 1e-6
NEG_INF = -1e30
QK_SCALE = 1.0 / math.sqrt(HEAD_DIM)
LOG2E = 1.4426950408889634
N_KV_F = 8

ADAM_LR = 0.001
ADAM_B1 = 0.9
ADAM_B2 = 0.999
ADAM_EPS = 1e-08
ADAM_WD = 0.01
ADAM_STEP = 10

VMEM_LIMIT = 56 * 1024 * 1024
PACK_ROW_ALIGN = 16
PACK_BLOCK_ROWS = 1024

MESH = pl.DeviceIdType.MESH

SHARDED = (("w_in_a", 2), ("w_in_b", 1), ("w_mem_kv", 1), ("w_o", 2), ("w_mlp1", 2), ("w_mlp2", 1), ("w_kv_shared", 1))
REPLICATED = ("norm1_g", "mem_norm_g", "norm2_g", "kv_norm_g", "b_f", "final_norm_g")
WEIGHT_ORDER = ("norm1_g", "w_in_a", "w_in_b", "w_mem_kv", "mem_norm_g", "w_o", "norm2_g", "w_mlp1", "w_mlp2",
                "kv_norm_g", "w_kv_shared", "b_f", "final_norm_g")


def _cparams(sem=None):
    return pltpu.CompilerParams(dimension_semantics=sem, vmem_limit_bytes=VMEM_LIMIT)


def _pick(n, cands):
    for c in cands:
        if c <= n and n % c == 0:
            return c
    return n


def _dot(a, b, dims):
    return lax.dot_general(a, b, (dims, ((), ())), preferred_element_type=F32)


NN = ((1,), (0,))
NT = ((1,), (1,))
TN = ((0,), (0,))


def _mm(a, b, *, name, trans_b=False, out_dtype=F32, res=None, epilogue=None, u=None):
    m, k = a.shape
    n = b.shape[0] if trans_b else b.shape[1]
    tm = _pick(m, (1024, 512, 256, 128))
    tn = _pick(n, (896, 768, 512, 384, 256, 128))
    tk = _pick(k, (1024, 896, 768, 512, 256, 128))
    nk = k // tk

    def body(*refs):
        a_ref, b_ref = refs[0], refs[1]
        pos = 2
        res_ref = u_ref = None
        if res is not None:
            res_ref = refs[pos]
            pos += 1
        if u is not None:
            u_ref = refs[pos]
            pos += 1
        outs, acc_ref = refs[pos:-1], refs[-1]
        kk = pl.program_id(2)

        @pl.when(kk == 0)
        def _():
            acc_ref[...] = jnp.zeros_like(acc_ref)

        acc_ref[...] += _dot(a_ref[...].astype(BF16), b_ref[...].astype(BF16), NT if trans_b else NN)

        @pl.when(kk == nk - 1)
        def _():
            acc = acc_ref[...]
            if res_ref is not None:
                acc = res_ref[...] + acc
            if epilogue == "relu2":
                outs[0][...] = acc
                r = jnp.maximum(acc, 0.0)
                outs[1][...] = (r * r).astype(BF16)
            elif epilogue == "drelu2":
                outs[0][...] = (acc * (2.0 * jnp.maximum(u_ref[...], 0.0))).astype(out_dtype)
            else:
                outs[0][...] = acc.astype(out_dtype)

    in_specs = [pl.BlockSpec((tm, tk), lambda i, j, kk: (i, kk)),
                pl.BlockSpec((tn, tk), lambda i, j, kk: (j, kk)) if trans_b else pl.BlockSpec((tk, tn), lambda i, j, kk: (kk, j))]
    args = [a, b]
    tile = pl.BlockSpec((tm, tn), lambda i, j, kk: (i, j))
    if res is not None:
        in_specs.append(tile)
        args.append(res)
    if u is not None:
        in_specs.append(tile)
        args.append(u)
    if epilogue == "relu2":
        out_shape = (jax.ShapeDtypeStruct((m, n), F32), jax.ShapeDtypeStruct((m, n), BF16))
        out_specs = (tile, tile)
    else:
        out_shape = (jax.ShapeDtypeStruct((m, n), out_dtype),)
        out_specs = (tile,)
    outs = pl.pallas_call(
        body, name=name, grid=(m // tm, n // tn, nk), in_specs=in_specs, out_specs=out_specs, out_shape=out_shape,
        scratch_shapes=[pltpu.VMEM((tm, tn), F32)],
        compiler_params=_cparams(("parallel", "parallel", "arbitrary")),
    )(*args)
    return outs if epilogue == "relu2" else outs[0]


def _mm_tn(x, dy, *, name):
    m, k1 = x.shape
    n = dy.shape[1]
    t1 = _pick(k1, (1024, 896, 768, 512, 256, 128))
    tn = _pick(n, (1024, 896, 768, 512, 256, 128))
    tm = _pick(m, (512, 256, 128))
    nm = m // tm

    def body(x_ref, dy_ref, o_ref):
        mm = pl.program_id(2)

        @pl.when(mm == 0)
        def _():
            o_ref[...] = jnp.zeros_like(o_ref)

        o_ref[...] += _dot(x_ref[...].astype(BF16), dy_ref[...].astype(BF16), TN)

    return pl.pallas_call(
        body, name=name, grid=(k1 // t1, n // tn, nm),
        in_specs=[pl.BlockSpec((tm, t1), lambda i, j, mm: (mm, i)), pl.BlockSpec((tm, tn), lambda i, j, mm: (mm, j))],
        out_specs=pl.BlockSpec((t1, tn), lambda i, j, mm: (i, j)),
        out_shape=jax.ShapeDtypeStruct((k1, n), F32),
        compiler_params=_cparams(("parallel", "parallel", "arbitrary")),
    )(x, dy)


def _rmsnorm_fwd(x, g, *, name):
    s, d = x.shape
    tm = _pick(s, (512, 256, 128))

    def body(x_ref, g_ref, o_ref):
        xf = x_ref[...]
        r = lax.rsqrt(jnp.mean(xf * xf, axis=-1, keepdims=True) + EPS)
        o_ref[...] = (xf * r * g_ref[...]).astype(BF16)

    return pl.pallas_call(
        body, name=name, grid=(s // tm,),
        in_specs=[pl.BlockSpec((tm, d), lambda i: (i, 0)), pl.BlockSpec((1, d), lambda i: (0, 0))],
        out_specs=pl.BlockSpec((tm, d), lambda i: (i, 0)),
        out_shape=jax.ShapeDtypeStruct((s, d), BF16),
        compiler_params=_cparams(("parallel",)),
    )(x, g.reshape(1, d))


def _rmsnorm_bwd(dy, x, g, dres, *, name):
    s, d = x.shape
    tm = _pick(s, (256, 128))

    def body(*refs):
        if dres is None:
            dy_ref, x_ref, g_ref, dx_ref, dg_ref = refs
            dres_ref = None
        else:
            dy_ref, x_ref, g_ref, dres_ref, dx_ref, dg_ref = refs
        i = pl.program_id(0)
        xf = x_ref[...]
        dyv = dy_ref[...]
        r = lax.rsqrt(jnp.mean(xf * xf, axis=-1, keepdims=True) + EPS)
        xh = xf * r
        dyg = dyv * g_ref[...]
        dx = r * (dyg - xh * jnp.mean(dyg * xh, axis=-1, keepdims=True))
        dx_ref[...] = dx if dres_ref is None else dres_ref[...] + dx

        @pl.when(i == 0)
        def _():
            dg_ref[...] = jnp.zeros_like(dg_ref)

        dg_ref[...] += jnp.sum(dyv * xh, axis=0, keepdims=True)

    row = pl.BlockSpec((tm, d), lambda i: (i, 0))
    vec = pl.BlockSpec((1, d), lambda i: (0, 0))
    in_specs = [row, row, vec] + ([] if dres is None else [row])
    args = [dy, x, g.reshape(1, d)] + ([] if dres is None else [dres])
    return pl.pallas_call(
        body, name=name, grid=(s // tm,), in_specs=in_specs, out_specs=(row, vec),
        out_shape=(jax.ShapeDtypeStruct((s, d), F32), jax.ShapeDtypeStruct((1, d), F32)),
        compiler_params=_cparams(("arbitrary",)),
    )(*args)


def _final_loss(h, target, g, *, name):
    s, d = h.shape
    tm = _pick(s, (256, 128))

    def body(h_ref, t_ref, g_ref, loss_ref, dh_ref, dg_ref):
        i = pl.program_id(0)
        xf = h_ref[...]
        gv = g_ref[...]
        r = lax.rsqrt(jnp.mean(xf * xf, axis=-1, keepdims=True) + EPS)
        xh = xf * r
        err = xh * gv - t_ref[...]
        part = 0.5 * jnp.sum(jnp.mean(err * err, axis=-1, keepdims=True), axis=0, keepdims=True)
        dyv = err * (1.0 / d)
        dyg = dyv * gv
        dh_ref[...] = r * (dyg - xh * jnp.mean(dyg * xh, axis=-1, keepdims=True))

        @pl.when(i == 0)
        def _():
            dg_ref[...] = jnp.zeros_like(dg_ref)
            loss_ref[...] = jnp.zeros_like(loss_ref)

        dg_ref[...] += jnp.sum(dyv * xh, axis=0, keepdims=True)
        loss_ref[...] += jnp.broadcast_to(part, loss_ref.shape)

    row = pl.BlockSpec((tm, d), lambda i: (i, 0))
    vec = pl.BlockSpec((1, d), lambda i: (0, 0))
    return pl.pallas_call(
        body, name=name, grid=(s // tm,), in_specs=[row, row, vec],
        out_specs=(pl.BlockSpec((1, LANES), lambda i: (0, 0)), row, vec),
        out_shape=(jax.ShapeDtypeStruct((1, LANES), F32), jax.ShapeDtypeStruct((s, d), F32), jax.ShapeDtypeStruct((1, d), F32)),
        compiler_params=_cparams(("arbitrary",)),
    )(h, target, g.reshape(1, d))


def _resident(shape, index_map):
    return pl.BlockSpec(shape, index_map, pipeline_mode=pl.Buffered(1))


def _head_lanes(rows):
    lane = lax.broadcasted_iota(jnp.int32, (rows, LANES), 1)
    return [lane < HEAD_DIM, lane >= HEAD_DIM]


def _tri(b, cmp):
    row = lax.broadcasted_iota(jnp.int32, (b, b), 0)
    col = lax.broadcasted_iota(jnp.int32, (b, b), 1)
    return cmp(row, col)


def _twice(mask):
    return jnp.concatenate([mask, mask], axis=0)


def _stack_heads(x, heads):
    zero = jnp.zeros_like(x)
    return jnp.concatenate([jnp.where(heads[0], x, zero), jnp.where(heads[1], x, zero)], axis=0)


def _unstack_heads(x2, heads):
    b = x2.shape[0] // 2
    return jnp.where(heads[0], x2[:b], x2[b:])


def _stack_stat(stat):
    return jnp.concatenate([stat[:, 0:1], stat[:, HEAD_DIM:HEAD_DIM + 1]], axis=0)


def _unstack_stat(col, heads):
    b = col.shape[0] // 2
    return jnp.where(heads[0], jnp.broadcast_to(col[:b], (b, LANES)), jnp.broadcast_to(col[b:], (b, LANES)))


def _split_dot(x, tri_bf16):
    hi = x.astype(BF16)
    lo = (x - hi.astype(F32)).astype(BF16)
    return _dot(hi, tri_bf16, NN) + _dot(lo, tri_bf16, NN)


def _prefix_sums(x, tri_bf16, inclusive):
    sub = tri_bf16.shape[0]
    outs, carry = [], None
    for c in range(x.shape[1] // sub):
        xs = x[:, c * sub:(c + 1) * sub]
        loc = _split_dot(xs, tri_bf16)
        outs.append(loc if carry is None else loc + carry)
        tot = loc[:, sub - 1:sub] if inclusive else loc[:, sub - 1:sub] + xs[:, sub - 1:sub]
        carry = tot if carry is None else carry + tot
    return (outs[0] if len(outs) == 1 else jnp.concatenate(outs, axis=1)), carry


def _suffix_sums(x, tri_bf16):
    sub = tri_bf16.shape[0]
    n = x.shape[1] // sub
    outs, carry = [None] * n, None
    for c in reversed(range(n)):
        xs = x[:, c * sub:(c + 1) * sub]
        loc = _split_dot(xs, tri_bf16)
        outs[c] = loc if carry is None else loc + carry
        tot = loc[:, 0:1] + xs[:, 0:1]
        carry = tot if carry is None else carry + tot
    return (outs[0] if n == 1 else jnp.concatenate(outs, axis=1)), carry


def _softplus2(z):
    z2 = z * LOG2E
    e = jnp.exp2(-jnp.abs(z2))
    return z2, jnp.maximum(z2, 0.0) + jnp.log2(1.0 + e), e


def _block_rows(j, b):
    return pl.ds(pl.multiple_of(j * b, b), b)


def _sb_fwd(proj, *, name):
    s = proj.shape[0]
    b = _pick(s, (ATT_B, 128))

    def body(q_ref, k_ref, v_ref, o_ref, tot_ref, acc_ref):
        i = pl.program_id(1)
        heads = _head_lanes(b)
        suffix = _tri(min(b, CUM_SUB), lambda r, c: r > c).astype(BF16)
        strict = _twice(_tri(b, lambda r, c: c < r))
        q2 = _stack_heads(q_ref[...] * QK_SCALE, heads)

        def tile(j, a, masked):
            rows = _block_rows(j, b)
            z2, sp, _ = _softplus2(_dot(q2, k_ref[rows, :], NT))
            if masked:
                sp = jnp.where(strict, sp, 0.0)
            rsum, total = _suffix_sums(sp, suffix)
            w = jnp.exp2((z2 - sp) - (a + rsum))
            if masked:
                w = jnp.where(strict, w, 0.0)
            acc_ref[...] += _dot(w.astype(BF16), v_ref[rows, :], NN)
            return a + total

        acc_ref[...] = jnp.zeros_like(acc_ref)
        a = tile(i, jnp.zeros((2 * b, 1), F32), True)
        a = lax.fori_loop(0, i, lambda jj, a: tile(i - 1 - jj, a, False), a)
        o_ref[...] = _unstack_heads(acc_ref[...], heads).astype(BF16)
        tot_ref[0] = _unstack_stat(a, heads)

    qblk = pl.BlockSpec((b, LANES), lambda p, i: (i, p))
    return pl.pallas_call(
        body, name=name, grid=(N_PAIRS, s // b),
        in_specs=[qblk, _resident((s, LANES), lambda p, i: (0, N_PAIRS + p)), _resident((s, LANES), lambda p, i: (0, 2 * N_PAIRS + p))],
        out_specs=(qblk, pl.BlockSpec((1, b, LANES), lambda p, i: (p, i, 0))),
        out_shape=(jax.ShapeDtypeStruct((s, MIX_W), BF16), jax.ShapeDtypeStruct((N_PAIRS, s, LANES), F32)),
        scratch_shapes=[pltpu.VMEM((2 * b, LANES), F32)],
        compiler_params=_cparams(("parallel", "arbitrary")),
    )(proj, proj, proj)


def _sb_bwd(proj, dmerged, tot, *, name):
    s = proj.shape[0]
    b = _pick(s, (ATT_B, 128))

    def body(q_ref, k_ref, v_ref, do_ref, tot_ref, dq_ref, dk_ref, dv_ref, dq_acc):
        i = pl.program_id(1)

        @pl.when(i == 0)
        def _():
            dk_ref[...] = jnp.zeros_like(dk_ref)
            dv_ref[...] = jnp.zeros_like(dv_ref)

        heads = _head_lanes(b)
        incl = _tri(min(b, CUM_SUB), lambda r, c: r <= c).astype(BF16)
        excl = _tri(min(b, CUM_SUB), lambda r, c: r < c).astype(BF16)
        strict = _twice(_tri(b, lambda r, c: c < r))
        q2 = _stack_heads(q_ref[...] * QK_SCALE, heads)
        do2 = _stack_heads(do_ref[...], heads)
        tot2 = _stack_stat(tot_ref[0])

        def tile(j, pre, gpre, masked):
            rows = _block_rows(j, b)
            kb = k_ref[rows, :]
            z2, sp, e = _softplus2(_dot(q2, kb, NT))
            sig = jnp.where(z2 >= 0.0, 1.0, e) * pl.reciprocal(1.0 + e, approx=True)
            if masked:
                sp = jnp.where(strict, sp, 0.0)
            pin, ptot = _prefix_sums(sp, incl, True)
            w = jnp.exp2((z2 - sp) + (pin + (pre - tot2)))
            if masked:
                w = jnp.where(strict, w, 0.0)
            gw = _dot(do2, v_ref[rows, :], NT) * w
            gex, gtot = _prefix_sums(gw, excl, False)
            dz = gw * (1.0 - sig) - sig * (gpre + gex)
            if masked:
                dz = jnp.where(strict, dz, 0.0)
            dzb = dz.astype(BF16)
            dq_acc[...] += _dot(dzb, kb, NN)
            dk_ref[rows, :] += _dot(dzb, q2, TN)
            dv_ref[rows, :] += _dot(w.astype(BF16), do2, TN)
            return pre + ptot, gpre + gtot

        dq_acc[...] = jnp.zeros_like(dq_acc)
        zero = jnp.zeros((2 * b, 1), F32)
        pre, gpre = lax.fori_loop(0, i, lambda j, c: tile(j, c[0], c[1], False), (zero, zero))
        tile(i, pre, gpre, True)
        dq_ref[...] = (_unstack_heads(dq_acc[...], heads) * QK_SCALE).astype(BF16)

    qblk = pl.BlockSpec((b, LANES), lambda p, i: (i, p))
    full = _resident((s, LANES), lambda p, i: (0, p))
    return pl.pallas_call(
        body, name=name, grid=(N_PAIRS, s // b),
        in_specs=[qblk, _resident((s, LANES), lambda p, i: (0, N_PAIRS + p)), _resident((s, LANES), lambda p, i: (0, 2 * N_PAIRS + p)),
                  qblk, pl.BlockSpec((1, b, LANES), lambda p, i: (p, i, 0))],
        out_specs=(qblk, full, full),
        out_shape=(jax.ShapeDtypeStruct((s, MIX_W), BF16), jax.ShapeDtypeStruct((s, MIX_W), F32), jax.ShapeDtypeStruct((s, MIX_W), F32)),
        scratch_shapes=[pltpu.VMEM((2 * b, LANES), F32)],
        compiler_params=_cparams(("parallel", "arbitrary")),
    )(proj, proj, proj, dmerged, tot)


def _fox_fwd(proj, kv, c_col, c_row, *, name):
    s = proj.shape[0]
    b = _pick(s, (ATT_B, 128))

    def body(q_ref, k_ref, v_ref, cc_ref, cr_ref, o_ref, lse_ref, acc_ref):
        i = pl.program_id(1)
        heads = _head_lanes(b)
        causal = _twice(_tri(b, lambda r, c: c <= r))
        top = lax.broadcasted_iota(jnp.int32, (2 * b, b), 0) < b
        q2 = _stack_heads(q_ref[...] * QK_SCALE, heads)
        c_t = _stack_stat(cc_ref[0])

        def tile(j, m, l, masked):
            rows = _block_rows(j, b)
            c_s = jnp.where(top, cr_ref[0, 0:1, rows], cr_ref[0, 1:2, rows])
            sc = _dot(q2, k_ref[rows, :], NT) + (c_t - c_s)
            if masked:
                sc = jnp.where(causal, sc, NEG_INF)
            m_new = jnp.maximum(m, jnp.max(sc, axis=1, keepdims=True))
            p = jnp.exp(sc - m_new)
            alpha = jnp.exp(m - m_new)
            acc_ref[...] = alpha * acc_ref[...] + _dot(p.astype(BF16), v_ref[rows, :], NN)
            return m_new, alpha * l + jnp.sum(p, axis=1, keepdims=True)

        acc_ref[...] = jnp.zeros_like(acc_ref)
        init = (jnp.full((2 * b, 1), NEG_INF, F32), jnp.zeros((2 * b, 1), F32))
        m, l = lax.fori_loop(0, i, lambda j, c: tile(j, c[0], c[1], False), init)
        m, l = tile(i, m, l, True)
        o_ref[...] = _unstack_heads(acc_ref[...] * (1.0 / l), heads).astype(BF16)
        lse_ref[0] = _unstack_stat(m + jnp.log(l), heads)

    qblk = pl.BlockSpec((b, LANES), lambda p, i: (i, p))
    stat = pl.BlockSpec((1, b, LANES), lambda p, i: (p, i, 0))
    return pl.pallas_call(
        body, name=name, grid=(N_PAIRS, s // b),
        in_specs=[qblk, _resident((s, LANES), lambda p, i: (0, p)), _resident((s, LANES), lambda p, i: (0, N_PAIRS + p)),
                  stat, _resident((1, 8, s), lambda p, i: (p, 0, 0))],
        out_specs=(qblk, stat),
        out_shape=(jax.ShapeDtypeStruct((s, MIX_W), BF16), jax.ShapeDtypeStruct((N_PAIRS, s, LANES), F32)),
        scratch_shapes=[pltpu.VMEM((2 * b, LANES), F32)],
        compiler_params=_cparams(("parallel", "arbitrary")),
    )(proj, kv, kv, c_col, c_row)


def _fox_bwd(proj, kv, c_col, c_row, merged, dmerged, lse, *, name):
    s = proj.shape[0]
    b = _pick(s, (ATT_B, 128))

    def body(q_ref, k_ref, v_ref, cc_ref, cr_ref, o_ref, do_ref, lse_ref, dq_ref, dk_ref, dv_ref, dc_ref, dcq_ref, dq_acc):
        i = pl.program_id(1)

        @pl.when(i == 0)
        def _():
            dk_ref[...] = jnp.zeros_like(dk_ref)
            dv_ref[...] = jnp.zeros_like(dv_ref)
            dc_ref[...] = jnp.zeros_like(dc_ref)

        heads = _head_lanes(b)
        causal = _twice(_tri(b, lambda r, c: c <= r))
        top = lax.broadcasted_iota(jnp.int32, (2 * b, b), 0) < b
        q2 = _stack_heads(q_ref[...] * QK_SCALE, heads)
        dov = do_ref[...]
        do2 = _stack_heads(dov, heads)
        prod = dov.astype(F32) * o_ref[...].astype(F32)
        delta = jnp.concatenate([jnp.sum(jnp.where(heads[hh], prod, 0.0), axis=1, keepdims=True) for hh in range(2)], axis=0)
        c_t = _stack_stat(cc_ref[0])
        lse_t = _stack_stat(lse_ref[0])

        def tile(j, rsum, masked):
            rows = _block_rows(j, b)
            kb = k_ref[rows, :]
            c_s = jnp.where(top, cr_ref[0, 0:1, rows], cr_ref[0, 1:2, rows])
            sc = _dot(q2, kb, NT) + (c_t - c_s)
            p = jnp.exp(sc - lse_t)
            if masked:
                p = jnp.where(causal, p, 0.0)
            ds = p * (_dot(do2, v_ref[rows, :], NT) - delta)
            dsb = ds.astype(BF16)
            dq_acc[...] += _dot(dsb, kb, NN)
            dk_ref[rows, :] += _dot(dsb, q2, TN)
            dv_ref[rows, :] += _dot(p.astype(BF16), do2, TN)
            dc_ref[0, 0:1, rows] -= jnp.sum(ds[:b], axis=0, keepdims=True)
            dc_ref[0, 1:2, rows] -= jnp.sum(ds[b:], axis=0, keepdims=True)
            return rsum + jnp.sum(ds, axis=1, keepdims=True)

        dq_acc[...] = jnp.zeros_like(dq_acc)
        rsum = lax.fori_loop(0, i, lambda j, r: tile(j, r, False), jnp.zeros((2 * b, 1), F32))
        rsum = tile(i, rsum, True)
        dq_ref[...] = (_unstack_heads(dq_acc[...], heads) * QK_SCALE).astype(BF16)
        dcq_ref[0] = _unstack_stat(rsum, heads)

    qblk = pl.BlockSpec((b, LANES), lambda p, i: (i, p))
    stat = pl.BlockSpec((1, b, LANES), lambda p, i: (p, i, 0))
    crow = _resident((1, 8, s), lambda p, i: (p, 0, 0))
    full = _resident((s, LANES), lambda p, i: (0, p))
    return pl.pallas_call(
        body, name=name, grid=(N_PAIRS, s // b),
        in_specs=[qblk, full, _resident((s, LANES), lambda p, i: (0, N_PAIRS + p)), stat, crow, qblk, qblk, stat],
        out_specs=(qblk, full, full, crow, stat),
        out_shape=(jax.ShapeDtypeStruct((s, MIX_W), BF16), jax.ShapeDtypeStruct((s, MIX_W), F32), jax.ShapeDtypeStruct((s, MIX_W), F32),
                   jax.ShapeDtypeStruct((N_PAIRS, 8, s), F32), jax.ShapeDtypeStruct((N_PAIRS, s, LANES), F32)),
        scratch_shapes=[pltpu.VMEM((2 * b, LANES), F32)],
        compiler_params=_cparams(("parallel", "arbitrary")),
    )(proj, kv, kv, c_col, c_row, merged, dmerged, lse)


MEM_TQ = 256


def _mem_fwd(proj, q_col_block, mkv, *, name):
    s = proj.shape[0]
    tq = _pick(s, (MEM_TQ, 128))
    n_mem = mkv.shape[0]

    def body(q_ref, mkv_ref, o_ref):
        heads = _head_lanes(tq)
        for pp in range(MEM_W // LANES):
            cols = slice(pp * LANES, (pp + 1) * LANES)
            qv = q_ref[:, cols] * QK_SCALE
            mk = mkv_ref[:, pp * LANES:(pp + 1) * LANES]
            mv = mkv_ref[:, MEM_W + pp * LANES:MEM_W + (pp + 1) * LANES]
            o_sel = None
            for hh in range(2):
                qm = jnp.where(heads[hh], qv, jnp.zeros_like(qv))
                sc = _dot(qm, mk, NT)
                p = jnp.exp(sc - jnp.max(sc, axis=1, keepdims=True))
                p = p / jnp.sum(p, axis=1, keepdims=True)
                out = _dot(p.astype(BF16), mv, NN)
                o_sel = out if hh == 0 else jnp.where(heads[0], o_sel, out)
            o_ref[:, cols] = o_sel.astype(BF16)

    return pl.pallas_call(
        body, name=name, grid=(s // tq,),
        in_specs=[pl.BlockSpec((tq, MEM_W), lambda i: (i, q_col_block)), pl.BlockSpec((n_mem, 2 * MEM_W), lambda i: (0, 0))],
        out_specs=pl.BlockSpec((tq, MEM_W), lambda i: (i, 0)),
        out_shape=jax.ShapeDtypeStruct((s, MEM_W), BF16),
        compiler_params=_cparams(("parallel",)),
    )(proj, mkv)


def _mem_bwd(proj, q_col_block, mkv, dmerged, *, name):
    s = proj.shape[0]
    tq = _pick(s, (MEM_TQ, 128))
    n_mem = mkv.shape[0]

    def body(q_ref, mkv_ref, do_ref, dq_ref, dmkv_ref):
        i = pl.program_id(0)

        @pl.when(i == 0)
        def _():
            dmkv_ref[...] = jnp.zeros_like(dmkv_ref)

        heads = _head_lanes(tq)
        for pp in range(MEM_W // LANES):
            cols = slice(pp * LANES, (pp + 1) * LANES)
            vcols = slice(MEM_W + pp * LANES, MEM_W + (pp + 1) * LANES)
            qv = q_ref[:, cols] * QK_SCALE
            dov = do_ref[:, cols]
            mk = mkv_ref[:, cols]
            mv = mkv_ref[:, vcols]
            dq_sel = None
            for hh in range(2):
                qm = jnp.where(heads[hh], qv, jnp.zeros_like(qv))
                dom = jnp.where(heads[hh], dov, jnp.zeros_like(dov))
                sc = _dot(qm, mk, NT)
                p = jnp.exp(sc - jnp.max(sc, axis=1, keepdims=True))
                p = p / jnp.sum(p, axis=1, keepdims=True)
                dp = _dot(dom, mv, NT)
                ds = p * (dp - jnp.sum(p * dp, axis=1, keepdims=True))
                dsb = ds.astype(BF16)
                dq = _dot(dsb, mk, NN)
                dmkv_ref[:, cols] += _dot(dsb, qm, TN)
                dmkv_ref[:, vcols] += _dot(p.astype(BF16), dom, TN)
                dq_sel = dq if hh == 0 else jnp.where(heads[0], dq_sel, dq)
            dq_ref[:, cols] = (dq_sel * QK_SCALE).astype(BF16)

    return pl.pallas_call(
        body, name=name, grid=(s // tq,),
        in_specs=[pl.BlockSpec((tq, MEM_W), lambda i: (i, q_col_block)), pl.BlockSpec((n_mem, 2 * MEM_W), lambda i: (0, 0)),
                  pl.BlockSpec((tq, MEM_W), lambda i: (i, MIX_W // MEM_W))],
        out_specs=(pl.BlockSpec((tq, MEM_W), lambda i: (i, 0)), pl.BlockSpec((n_mem, 2 * MEM_W), lambda i: (0, 0))),
        out_shape=(jax.ShapeDtypeStruct((s, MEM_W), BF16), jax.ShapeDtypeStruct((n_mem, 2 * MEM_W), F32)),
        compiler_params=_cparams(("arbitrary",)),
    )(proj, mkv, dmerged)


GATE_TB = 256


def _split3_dot(tri_bf16, x):
    x1 = x.astype(BF16)
    r1 = x - x1.astype(F32)
    x2 = r1.astype(BF16)
    x3 = (r1 - x2.astype(F32)).astype(BF16)
    return _dot(tri_bf16, x1, NN) + _dot(tri_bf16, x2, NN) + _dot(tri_bf16, x3, NN)


def _gate_fwd(f, b, *, name):
    s = f.shape[0]
    tb = _pick(s, (GATE_TB, 128))

    def body(f_ref, b_ref, c_ref, carry_ref):
        i = pl.program_id(0)

        @pl.when(i == 0)
        def _():
            carry_ref[...] = jnp.zeros_like(carry_ref)

        x = f_ref[...] + b_ref[...]
        lf = jnp.minimum(x, 0.0) - jnp.log1p(jnp.exp(-jnp.abs(x)))
        row = lax.broadcasted_iota(jnp.int32, (tb, tb), 0)
        col = lax.broadcasted_iota(jnp.int32, (tb, tb), 1)
        lower = (col <= row).astype(BF16)
        c = carry_ref[...] + _split3_dot(lower, lf)
        c_ref[...] = c
        carry_ref[...] = c[tb - 1:tb, :]

    return pl.pallas_call(
        body, name=name, grid=(s // tb,),
        in_specs=[pl.BlockSpec((tb, LANES), lambda i: (i, 0)), pl.BlockSpec((1, LANES), lambda i: (0, 0))],
        out_specs=pl.BlockSpec((tb, LANES), lambda i: (i, 0)),
        out_shape=jax.ShapeDtypeStruct((s, LANES), F32),
        scratch_shapes=[pltpu.VMEM((1, LANES), F32)],
        compiler_params=_cparams(("arbitrary",)),
    )(f, b)


def _gate_bwd(f, b, dc, *, name):
    s = f.shape[0]
    tb = _pick(s, (GATE_TB, 128))
    nb = s // tb

    def body(f_ref, b_ref, dc_ref, df_ref, db_ref, carry_ref):
        i = pl.program_id(0)

        @pl.when(i == 0)
        def _():
            carry_ref[...] = jnp.zeros_like(carry_ref)
            db_ref[...] = jnp.zeros_like(db_ref)

        row = lax.broadcasted_iota(jnp.int32, (tb, tb), 0)
        col = lax.broadcasted_iota(jnp.int32, (tb, tb), 1)
        upper = (col >= row).astype(BF16)
        dlf = carry_ref[...] + _split3_dot(upper, dc_ref[...])
        carry_ref[...] = dlf[0:1, :]
        x = f_ref[...] + b_ref[...]
        e = jnp.exp(-jnp.abs(x))
        one_minus_sig = jnp.where(x >= 0.0, e, 1.0) / (1.0 + e)
        df = dlf * one_minus_sig
        df_ref[...] = df
        db_ref[...] += jnp.sum(df, axis=0, keepdims=True)

    rev = pl.BlockSpec((tb, LANES), lambda i: (nb - 1 - i, 0))
    vec = pl.BlockSpec((1, LANES), lambda i: (0, 0))
    return pl.pallas_call(
        body, name=name, grid=(nb,), in_specs=[rev, vec, rev], out_specs=(rev, vec),
        out_shape=(jax.ShapeDtypeStruct((s, LANES), F32), jax.ShapeDtypeStruct((1, LANES), F32)),
        scratch_shapes=[pltpu.VMEM((1, LANES), F32)],
        compiler_params=_cparams(("arbitrary",)),
    )(f, b, dc)


def _all_gather(x, *, name):
    r, cdim = x.shape

    def body(x_ref, out_ref, send_sems, recv_sems, local_sem):
        mx, my, mc = lax.axis_index("x"), lax.axis_index("y"), lax.axis_index("c")
        me, sibling = (mx, my, mc), (mx, my, 1 - mc)
        chips = [(1 - mx, my), (mx, 1 - my), (1 - mx, 1 - my)]

        def rows(px, py, pc):
            return out_ref.at[pl.ds((4 * px + 2 * py + pc) * r, r), :]

        def copy(k, block, to, src=None):
            return pltpu.make_async_remote_copy(
                src_ref=rows(*block) if src is None else src, dst_ref=rows(*block),
                send_sem=send_sems.at[k], recv_sem=recv_sems.at[k], device_id=to, device_id_type=MESH)

        mine = pltpu.make_async_copy(x_ref, rows(*me), local_sem)
        mine.start()
        first = [copy(0, me, sibling, src=x_ref)]
        first += [copy(1 + j, me, (*chip, mc), src=x_ref) for j, chip in enumerate(chips)]
        for cp in first:
            cp.start()
        passed = [copy(4 + j, (*chip, mc), sibling) for j, chip in enumerate(chips)]
        for j, chip in enumerate(chips):
            copy(1 + j, (*chip, mc), me).wait_recv()
            passed[j].start()
        copy(0, sibling, me).wait_recv()
        for j, chip in enumerate(chips):
            copy(4 + j, (*chip, 1 - mc), me).wait_recv()
        for cp in first + passed:
            cp.wait_send()
        mine.wait()

    return pl.pallas_call(
        body, name=name,
        in_specs=[pl.BlockSpec(memory_space=pl.ANY)], out_specs=pl.BlockSpec(memory_space=pl.ANY),
        out_shape=jax.ShapeDtypeStruct((N_DEV * r, cdim), x.dtype),
        scratch_shapes=[pltpu.SemaphoreType.DMA((7,)), pltpu.SemaphoreType.DMA((7,)), pltpu.SemaphoreType.DMA],
    )(x)


def _exchange_partials(g, *, name):
    def body(g_ref, out_ref, send_sems, recv_sems, local_sem):
        mx, my, mc = lax.axis_index("x"), lax.axis_index("y"), lax.axis_index("c")
        me = 4 * mx + 2 * my + mc
        mine = pltpu.make_async_copy(g_ref.at[me], out_ref.at[me], local_sem)
        mine.start()
        copies = []
        for mask in range(1, N_DEV):
            px = 1 - mx if mask & 4 else mx
            py = 1 - my if mask & 2 else my
            pc = 1 - mc if mask & 1 else mc
            cp = pltpu.make_async_remote_copy(
                src_ref=g_ref.at[4 * px + 2 * py + pc], dst_ref=out_ref.at[me],
                send_sem=send_sems.at[mask - 1], recv_sem=recv_sems.at[mask - 1], device_id=(px, py, pc), device_id_type=MESH)
            cp.start()
            copies.append(cp)
        for cp in copies:
            cp.wait()
        mine.wait()

    return pl.pallas_call(
        body, name=name,
        in_specs=[pl.BlockSpec(memory_space=pl.ANY)], out_specs=pl.BlockSpec(memory_space=pl.ANY),
        out_shape=jax.ShapeDtypeStruct(g.shape, g.dtype),
        scratch_shapes=[pltpu.SemaphoreType.DMA((7,)), pltpu.SemaphoreType.DMA((7,)), pltpu.SemaphoreType.DMA],
    )(g)


def _adamw(parts, w, m, v, *, name):
    r, cdim = w.shape
    tr = _pick(r, (PACK_BLOCK_ROWS, 512, 256, 128, 64, 32, 16, 8))
    c1 = 1.0 / (1.0 - ADAM_B1 ** ADAM_STEP)
    c2 = 1.0 / (1.0 - ADAM_B2 ** ADAM_STEP)

    def body(p_ref, w_ref, m_ref, v_ref, g_ref, d_ref, nm_ref, nv_ref):
        g = p_ref[0]
        for dev in range(1, N_DEV):
            g = g + p_ref[dev]
        mn = ADAM_B1 * m_ref[...] + (1.0 - ADAM_B1) * g
        vn = ADAM_B2 * v_ref[...] + (1.0 - ADAM_B2) * (g * g)
        g_ref[...] = g
        nm_ref[...] = mn
        nv_ref[...] = vn
        d_ref[...] = -ADAM_LR * ((mn * c1) / (jnp.sqrt(vn * c2) + ADAM_EPS) + ADAM_WD * w_ref[...])

    blk = pl.BlockSpec((tr, cdim), lambda i: (i, 0))
    shape = jax.ShapeDtypeStruct((r, cdim), F32)
    return pl.pallas_call(
        body, name=name, grid=(r // tr,),
        in_specs=[pl.BlockSpec((N_DEV, tr, cdim), lambda i: (0, i, 0)), blk, blk, blk],
        out_specs=(blk, blk, blk, blk), out_shape=(shape, shape, shape, shape),
        compiler_params=_cparams(("parallel",)),
    )(parts, w, m, v)


def _rows_of(shape):
    n = math.prod(shape)
    assert n % LANES == 0, shape
    rows = n // LANES
    return -(-rows // PACK_ROW_ALIGN) * PACK_ROW_ALIGN


def _layout(shard_shapes):
    out, off = {}, 0
    for name, _ in SHARDED:
        rows = _rows_of(shard_shapes[name])
        out[name] = (off, rows, tuple(shard_shapes[name]))
        off += rows
    return out, -(-off // PACK_BLOCK_ROWS) * PACK_BLOCK_ROWS


def _pack_shards(layout, total, arrays, dtype):
    parts = []
    for name, _ in SHARDED:
        _, rows, _ = layout[name]
        flat = arrays[name].astype(dtype).reshape(-1, LANES)
        parts.append(jnp.pad(flat, ((0, rows - flat.shape[0]), (0, 0))))
    used = sum(p.shape[0] for p in parts)
    if total > used:
        parts.append(jnp.zeros((total - used, LANES), dtype))
    return jnp.concatenate(parts, axis=0)


def _unpack_shard(layout, flat, name):
    off, _, shape = layout[name]
    n = math.prod(shape) // LANES
    return flat[off:off + n].reshape(shape)


def _unpack_full(layout, gathered, name, axis):
    off, _, shape = layout[name]
    n = math.prod(shape) // LANES
    blocks = gathered[:, off:off + n].reshape((N_DEV,) + shape)
    blocks = jnp.moveaxis(blocks, 0, axis)
    return blocks.reshape(shape[:axis] + (N_DEV * shape[axis],) + shape[axis + 1:])


def _pack_full(layout, total, grads):
    parts = []
    for name, axis in SHARDED:
        _, rows, shape = layout[name]
        g = grads[name]
        blocks = g.reshape(shape[:axis] + (N_DEV, shape[axis]) + shape[axis + 1:])
        blocks = jnp.moveaxis(blocks, axis, 0).reshape(N_DEV, -1, LANES)
        parts.append(jnp.pad(blocks, ((0, 0), (0, rows - blocks.shape[1]), (0, 0))))
    used = sum(p.shape[1] for p in parts)
    if total > used:
        parts.append(jnp.zeros((N_DEV, total - used, LANES), F32))
    return jnp.concatenate(parts, axis=1)


def _pad_lanes(a):
    return jnp.pad(a, ((0, 0), (0, LANES - a.shape[1])))


def _pair_layouts(c):
    s = c.shape[0]
    by_pair = c.T.reshape(N_PAIRS, 2, s)
    c_col = jnp.repeat(by_pair.transpose(0, 2, 1), HEAD_DIM, axis=2)
    c_row = jnp.pad(by_pair, ((0, 0), (0, 6), (0, 0)))
    return c_col, c_row


def _forward_backward(x, mem, target, wts, small):
    n_a = wts["w_in_a"].shape[0]
    n_b = wts["w_in_b"].shape[0]
    depth = n_a + n_b
    w_kv = wts["w_kv_shared"]
    w_kv_kv = w_kv[:, :2 * MIX_W]
    w_kv_f = _pad_lanes(w_kv[:, 2 * MIX_W:])
    b_f = _pad_lanes(small["b_f"].reshape(1, -1))

    saved = []
    shared = None
    h = x
    for l in range(depth):
        is_a = l < n_a
        if l == n_a:
            hs = _rmsnorm_fwd(h, small["kv_norm_g"], name="kv_norm")
            kv = _mm(hs, w_kv_kv, name="kv_proj", out_dtype=BF16)
            f = _mm(hs, w_kv_f, name="gate_proj")
            c = _gate_fwd(f, b_f, name="gate_cumsum")
            c_col, c_row = _pair_layouts(c[:, :N_MIX_HEADS])
            shared = dict(h=h, hs=hs, kv=kv, f=f, c_col=c_col, c_row=c_row)
        hn = _rmsnorm_fwd(h, small["norm1_g"][l], name=f"norm1_{l}")
        memn = _rmsnorm_fwd(mem, small["mem_norm_g"][l], name=f"mem_norm_{l}")
        mkv = _mm(memn, wts["w_mem_kv"][l], name=f"mem_kv_{l}", out_dtype=BF16)
        if is_a:
            proj = _mm(hn, wts["w_in_a"][l], name=f"in_proj_{l}", out_dtype=BF16)
            mix, stat = _sb_fwd(proj, name=f"sb_fwd_{l}")
            q_block = 3 * MIX_W // MEM_W
        else:
            proj = _mm(hn, wts["w_in_b"][l - n_a], name=f"in_proj_{l}", out_dtype=BF16)
            mix, stat = _fox_fwd(proj, shared["kv"], shared["c_col"], shared["c_row"], name=f"fox_fwd_{l}")
            q_block = MIX_W // MEM_W
        mem_out = _mem_fwd(proj, q_block, mkv, name=f"mem_fwd_{l}")
        merged = jnp.concatenate([mix, mem_out], axis=1)
        h_mid = _mm(merged, wts["w_o"][l], name=f"o_proj_{l}", res=h)
        h2n = _rmsnorm_fwd(h_mid, small["norm2_g"][l], name=f"norm2_{l}")
        u, act = _mm(h2n, wts["w_mlp1"][l], name=f"mlp1_{l}", epilogue="relu2")
        h_out = _mm(act, wts["w_mlp2"][l], name=f"mlp2_{l}", res=h_mid)
        saved.append(dict(h=h, hn=hn, memn=memn, mkv=mkv, proj=proj, stat=stat, merged=merged, h_mid=h_mid, h2n=h2n, u=u, act=act,
                          q_block=q_block))
        h = h_out

    loss, dh, dg_final = _final_loss(h, target, small["final_norm_g"], name="final_loss")

    g_w = {k: [None] * wts[k].shape[0] for k in ("w_in_a", "w_in_b", "w_mem_kv", "w_o", "w_mlp1", "w_mlp2")}
    g_n = {k: [None] * depth for k in ("norm1_g", "mem_norm_g", "norm2_g")}
    dk_sh = dv_sh = dc_sh = dcq_sh = None
    for l in reversed(range(depth)):
        sv = saved[l]
        is_a = l < n_a
        du = _mm(dh, wts["w_mlp2"][l], name=f"d_act_{l}", trans_b=True, epilogue="drelu2", u=sv["u"], out_dtype=BF16)
        g_w["w_mlp2"][l] = _mm_tn(sv["act"], dh, name=f"dw_mlp2_{l}")
        g_w["w_mlp1"][l] = _mm_tn(sv["h2n"], du, name=f"dw_mlp1_{l}")
        dh2n = _mm(du, wts["w_mlp1"][l], name=f"d_h2n_{l}", trans_b=True)
        dh_mid, g_n["norm2_g"][l] = _rmsnorm_bwd(dh2n, sv["h_mid"], small["norm2_g"][l], dh, name=f"norm2_bwd_{l}")
        dmerged = _mm(dh_mid, wts["w_o"][l], name=f"d_merged_{l}", trans_b=True, out_dtype=BF16)
        g_w["w_o"][l] = _mm_tn(sv["merged"], dh_mid, name=f"dw_o_{l}")
        dqm, dmkv = _mem_bwd(sv["proj"], sv["q_block"], sv["mkv"], dmerged, name=f"mem_bwd_{l}")
        if is_a:
            dq, dk, dv = _sb_bwd(sv["proj"], dmerged, sv["stat"], name=f"sb_bwd_{l}")
            dproj = jnp.concatenate([dq, dk.astype(BF16), dv.astype(BF16), dqm], axis=1)
            w_in, key, idx = wts["w_in_a"][l], "w_in_a", l
        else:
            dq, dk, dv, dc, dcq = _fox_bwd(sv["proj"], shared["kv"], shared["c_col"], shared["c_row"], sv["merged"], dmerged, sv["stat"],
                                           name=f"fox_bwd_{l}")
            dk_sh = dk if dk_sh is None else dk_sh + dk
            dv_sh = dv if dv_sh is None else dv_sh + dv
            dc_sh = dc if dc_sh is None else dc_sh + dc
            dcq_sh = dcq if dcq_sh is None else dcq_sh + dcq
            dproj = jnp.concatenate([dq, dqm], axis=1)
            w_in, key, idx = wts["w_in_b"][l - n_a], "w_in_b", l - n_a
        g_w[key][idx] = _mm_tn(sv["hn"], dproj, name=f"dw_in_{l}")
        dhn = _mm(dproj, w_in, name=f"d_hn_{l}", trans_b=True)
        dh, g_n["norm1_g"][l] = _rmsnorm_bwd(dhn, sv["h"], small["norm1_g"][l], dh_mid, name=f"norm1_bwd_{l}")
        g_w["w_mem_kv"][l] = _mm_tn(sv["memn"], dmkv, name=f"dw_mem_kv_{l}")
        dmemn = _mm(dmkv, wts["w_mem_kv"][l], name=f"d_memn_{l}", trans_b=True)
        _, g_n["mem_norm_g"][l] = _rmsnorm_bwd(dmemn, mem, small["mem_norm_g"][l], None, name=f"mem_norm_bwd_{l}")
        if l == n_a:
            s_len = x.shape[0]
            dc_query = dcq_sh[:, :, ::HEAD_DIM].transpose(1, 0, 2).reshape(s_len, N_MIX_HEADS)
            dc_tok = _pad_lanes(dc_sh[:, :2, :].reshape(N_MIX_HEADS, s_len).T + dc_query)
            df, db = _gate_bwd(shared["f"], b_f, dc_tok, name="gate_bwd")
            dkv = jnp.concatenate([dk_sh, dv_sh], axis=1)
            dw_kv_kv = _mm_tn(shared["hs"], dkv, name="dw_kv")
            dw_kv_f = _mm_tn(shared["hs"], df, name="dw_gate")
            dhs = _mm(dkv, w_kv_kv, name="d_hs_kv", trans_b=True)
            dhs = _mm(df, w_kv_f, name="d_hs_gate", trans_b=True, res=dhs)
            dh, dg_kv = _rmsnorm_bwd(dhs, shared["h"], small["kv_norm_g"], dh, name="kv_norm_bwd")
            g_kv = jnp.concatenate([dw_kv_kv, dw_kv_f[:, :N_KV_F]], axis=1)

    grads = {k: jnp.stack(v) for k, v in g_w.items()}
    grads["w_kv_shared"] = g_kv
    d = x.shape[1]
    small_g = dict(
        norm1_g=jnp.concatenate(g_n["norm1_g"], axis=0), mem_norm_g=jnp.concatenate(g_n["mem_norm_g"], axis=0),
        norm2_g=jnp.concatenate(g_n["norm2_g"], axis=0), kv_norm_g=dg_kv.reshape(d), b_f=db[0, :N_KV_F], final_norm_g=dg_final.reshape(d))
    return loss, dh, grads, small_g


def _pack_small(vals, d):
    rows = []
    for name in REPLICATED:
        a = vals[name].astype(F32)
        if name == "b_f":
            a = jnp.pad(a, (0, d - a.shape[0]))
        rows.append(a.reshape(-1, d))
    packed = jnp.concatenate(rows, axis=0)
    pad = -packed.shape[0] % 8
    return jnp.pad(packed, ((0, pad), (0, 0)))


def _unpack_small(packed, shapes):
    out, off = {}, 0
    for name in REPLICATED:
        shape = shapes[name]
        if name == "b_f":
            out[name] = packed[off, :shape[0]]
            off += 1
        else:
            n = math.prod(shape) // packed.shape[1]
            out[name] = packed[off:off + n].reshape(shape)
            off += n
    return out


def kernel(x, mem, norm1_g, w_in_a, w_in_b, w_mem_kv, mem_norm_g, w_o, norm2_g, w_mlp1, w_mlp2, kv_norm_g, w_kv_shared, b_f, final_norm_g, loss_target, m_norm1_g, m_w_in_a, m_w_in_b, m_w_mem_kv, m_mem_norm_g, m_w_o, m_norm2_g, m_w_mlp1, m_w_mlp2, m_kv_norm_g, m_w_kv_shared, m_b_f, m_final_norm_g, v_norm1_g, v_w_in_a, v_w_in_b, v_w_mem_kv, v_mem_norm_g, v_w_o, v_norm2_g, v_w_mlp1, v_w_mlp2, v_kv_norm_g, v_w_kv_shared, v_b_f, v_final_norm_g):
    w = dict(norm1_g=norm1_g, w_in_a=w_in_a, w_in_b=w_in_b, w_mem_kv=w_mem_kv, mem_norm_g=mem_norm_g, w_o=w_o, norm2_g=norm2_g,
             w_mlp1=w_mlp1, w_mlp2=w_mlp2, kv_norm_g=kv_norm_g, w_kv_shared=w_kv_shared, b_f=b_f, final_norm_g=final_norm_g)
    m = dict(norm1_g=m_norm1_g, w_in_a=m_w_in_a, w_in_b=m_w_in_b, w_mem_kv=m_w_mem_kv, mem_norm_g=m_mem_norm_g, w_o=m_w_o,
             norm2_g=m_norm2_g, w_mlp1=m_w_mlp1, w_mlp2=m_w_mlp2, kv_norm_g=m_kv_norm_g, w_kv_shared=m_w_kv_shared, b_f=m_b_f,
             final_norm_g=m_final_norm_g)
    v = dict(norm1_g=v_norm1_g, w_in_a=v_w_in_a, w_in_b=v_w_in_b, w_mem_kv=v_w_mem_kv, mem_norm_g=v_mem_norm_g, w_o=v_w_o,
             norm2_g=v_norm2_g, w_mlp1=v_w_mlp1, w_mlp2=v_w_mlp2, kv_norm_g=v_kv_norm_g, w_kv_shared=v_w_kv_shared, b_f=v_b_f,
             final_norm_g=v_final_norm_g)
    d = x.shape[-1]
    layout, total = _layout({name: w[name].shape for name, _ in SHARDED})

    gathered = _all_gather(_pack_shards(layout, total, w, BF16), name="gather_weights").reshape(N_DEV, total, LANES)
    wts = {name: _unpack_full(layout, gathered, name, axis) for name, axis in SHARDED}
    small = {name: w[name] for name in REPLICATED}

    loss, grad_x, grads, small_g = _forward_backward(x[0], mem[0], loss_target[0], wts, small)

    parts = _exchange_partials(_pack_full(layout, total, grads), name="exchange_grads")
    g_flat, d_flat, m_flat, v_flat = _adamw(parts, _pack_shards(layout, total, w, F32), _pack_shards(layout, total, m, F32),
                                            _pack_shards(layout, total, v, F32), name="adamw_sharded")

    small_packed = _pack_small(small_g, d)
    n_small = small_packed.shape[0]
    small_parts = _all_gather(small_packed, name="gather_small_grads").reshape(N_DEV, n_small, d)
    gs, ds_, ms, vs = _adamw(small_parts, _pack_small(w, d), _pack_small(m, d), _pack_small(v, d), name="adamw_replicated")

    shapes = {name: w[name].shape for name in REPLICATED}
    out_g, out_d, out_m, out_v = {}, {}, {}, {}
    for flat, small_flat, out in ((g_flat, gs, out_g), (d_flat, ds_, out_d), (m_flat, ms, out_m), (v_flat, vs, out_v)):
        for name, _ in SHARDED:
            out[name] = _unpack_shard(layout, flat, name)
        out.update(_unpack_small(small_flat, shapes))

    loss_total = lax.psum(loss[0, 0], ("x", "y", "c"))
    return (loss_total, grad_x[None], *[out_g[n] for n in WEIGHT_ORDER], *[out_d[n] for n in WEIGHT_ORDER],
            *[out_m[n] for n in WEIGHT_ORDER], *[out_v[n] for n in WEIGHT_ORDER])
```

```python
import functools
import math

import jax
import jax.numpy as jnp
from jax import lax
from jax.experimental import pallas as pl
from jax.experimental.pallas import tpu as pltpu

F32 = jnp.float32
BF16 = jnp.bfloat16

N_DEV = 8
HEAD_DIM = 64
N_MIX_HEADS = 8
N_MEM_HEADS = 4
MIX_W = N_MIX_HEADS * HEAD_DIM
MEM_W = N_MEM_HEADS * HEAD_DIM
N_PAIRS = N_MIX_HEADS // 2
LANES = 128
ATT_B = 512
CUM_SUB = 256
EPS = 1e-6
NEG_INF = -1e30
QK_SCALE = 1.0 / math.sqrt(HEAD_DIM)
LOG2E = 1.4426950408889634
N_KV_F = 8

ADAM_LR = 0.001
ADAM_B1 = 0.9
ADAM_B2 = 0.999
ADAM_EPS = 1e-08
ADAM_WD = 0.01
ADAM_STEP = 10

VMEM_LIMIT = 56 * 1024 * 1024
PACK_ROW_ALIGN = 16
PACK_BLOCK_ROWS = 1024

MESH = pl.DeviceIdType.MESH

SHARDED = (("w_in_a", 2), ("w_in_b", 1), ("w_mem_kv", 1), ("w_o", 2), ("w_mlp1", 2), ("w_mlp2", 1), ("w_kv_shared", 1))
REPLICATED = ("norm1_g", "mem_norm_g", "norm2_g", "kv_norm_g", "b_f", "final_norm_g")
WEIGHT_ORDER = ("norm1_g", "w_in_a", "w_in_b", "w_mem_kv", "mem_norm_g", "w_o", "norm2_g", "w_mlp1", "w_mlp2",
                "kv_norm_g", "w_kv_shared", "b_f", "final_norm_g")


def _cparams(sem=None):
    return pltpu.CompilerParams(dimension_semantics=sem, vmem_limit_bytes=VMEM_LIMIT)


def _pick(n, cands):
    for c in cands:
        if c <= n and n % c == 0:
            return c
    return n


def _dot(a, b, dims):
    return lax.dot_general(a, b, (dims, ((), ())), preferred_element_type=F32)


NN = ((1,), (0,))
NT = ((1,), (1,))
TN = ((0,), (0,))


def _mm(a, b, *, name, trans_b=False, out_dtype=F32, res=None, epilogue=None, u=None):
    m, k = a.shape
    n = b.shape[0] if trans_b else b.shape[1]
    tm = _pick(m, (1024, 512, 256, 128))
    tn = _pick(n, (896, 768, 512, 384, 256, 128))
    tk = _pick(k, (1024, 896, 768, 512, 256, 128))
    nk = k // tk

    def body(*refs):
        a_ref, b_ref = refs[0], refs[1]
        pos = 2
        res_ref = u_ref = None
        if res is not None:
            res_ref = refs[pos]
            pos += 1
        if u is not None:
            u_ref = refs[pos]
            pos += 1
        outs, acc_ref = refs[pos:-1], refs[-1]
        kk = pl.program_id(2)

        @pl.when(kk == 0)
        def _():
            acc_ref[...] = jnp.zeros_like(acc_ref)

        acc_ref[...] += _dot(a_ref[...].astype(BF16), b_ref[...].astype(BF16), NT if trans_b else NN)

        @pl.when(kk == nk - 1)
        def _():
            acc = acc_ref[...]
            if res_ref is not None:
                acc = res_ref[...] + acc
            if epilogue == "relu2":
                outs[0][...] = acc.astype(BF16)
                r = jnp.maximum(acc, 0.0)
                outs[1][...] = (r * r).astype(BF16)
            elif epilogue == "drelu2":
                outs[0][...] = (acc * (2.0 * jnp.maximum(u_ref[...], 0.0))).astype(out_dtype)
            else:
                outs[0][...] = acc.astype(out_dtype)

    in_specs = [pl.BlockSpec((tm, tk), lambda i, j, kk: (i, kk)),
                pl.BlockSpec((tn, tk), lambda i, j, kk: (j, kk)) if trans_b else pl.BlockSpec((tk, tn), lambda i, j, kk: (kk, j))]
    args = [a, b]
    tile = pl.BlockSpec((tm, tn), lambda i, j, kk: (i, j))
    if res is not None:
        in_specs.append(tile)
        args.append(res)
    if u is not None:
        in_specs.append(tile)
        args.append(u)
    if epilogue == "relu2":
        out_shape = (jax.ShapeDtypeStruct((m, n), BF16), jax.ShapeDtypeStruct((m, n), BF16))
        out_specs = (tile, tile)
    else:
        out_shape = (jax.ShapeDtypeStruct((m, n), out_dtype),)
        out_specs = (tile,)
    outs = pl.pallas_call(
        body, name=name, grid=(m // tm, n // tn, nk), in_specs=in_specs, out_specs=out_specs, out_shape=out_shape,
        scratch_shapes=[pltpu.VMEM((tm, tn), F32)],
        compiler_params=_cparams(("parallel", "parallel", "arbitrary")),
    )(*args)
    return outs if epilogue == "relu2" else outs[0]


def _mm_tn(x, dy, *, name):
    m, k1 = x.shape
    n = dy.shape[1]
    t1 = _pick(k1, (1024, 896, 768, 512, 256, 128))
    tn = _pick(n, (1024, 896, 768, 512, 256, 128))
    tm = _pick(m, (512, 256, 128))
    nm = m // tm

    def body(x_ref, dy_ref, o_ref):
        mm = pl.program_id(2)

        @pl.when(mm == 0)
        def _():
            o_ref[...] = jnp.zeros_like(o_ref)

        o_ref[...] += _dot(x_ref[...].astype(BF16), dy_ref[...].astype(BF16), TN)

    return pl.pallas_call(
        body, name=name, grid=(k1 // t1, n // tn, nm),
        in_specs=[pl.BlockSpec((tm, t1), lambda i, j, mm: (mm, i)), pl.BlockSpec((tm, tn), lambda i, j, mm: (mm, j))],
        out_specs=pl.BlockSpec((t1, tn), lambda i, j, mm: (i, j)),
        out_shape=jax.ShapeDtypeStruct((k1, n), F32),
        compiler_params=_cparams(("parallel", "parallel", "arbitrary")),
    )(x, dy)


def _rmsnorm_fwd(x, g, *, name):
    s, d = x.shape
    tm = _pick(s, (512, 256, 128))

    def body(x_ref, g_ref, o_ref):
        xf = x_ref[...]
        r = lax.rsqrt(jnp.mean(xf * xf, axis=-1, keepdims=True) + EPS)
        o_ref[...] = (xf * r * g_ref[...]).astype(BF16)

    return pl.pallas_call(
        body, name=name, grid=(s // tm,),
        in_specs=[pl.BlockSpec((tm, d), lambda i: (i, 0)), pl.BlockSpec((1, d), lambda i: (0, 0))],
        out_specs=pl.BlockSpec((tm, d), lambda i: (i, 0)),
        out_shape=jax.ShapeDtypeStruct((s, d), BF16),
        compiler_params=_cparams(("parallel",)),
    )(x, g.reshape(1, d))


def _rmsnorm_bwd(dy, x, g, dres, *, name):
    s, d = x.shape
    tm = _pick(s, (256, 128))

    def body(*refs):
        if dres is None:
            dy_ref, x_ref, g_ref, dx_ref, dg_ref = refs
            dres_ref = None
        else:
            dy_ref, x_ref, g_ref, dres_ref, dx_ref, dg_ref = refs
        i = pl.program_id(0)
        xf = x_ref[...]
        dyv = dy_ref[...]
        r = lax.rsqrt(jnp.mean(xf * xf, axis=-1, keepdims=True) + EPS)
        xh = xf * r
        dyg = dyv * g_ref[...]
        dx = r * (dyg - xh * jnp.mean(dyg * xh, axis=-1, keepdims=True))
        dx_ref[...] = dx if dres_ref is None else dres_ref[...] + dx

        @pl.when(i == 0)
        def _():
            dg_ref[...] = jnp.zeros_like(dg_ref)

        dg_ref[...] += jnp.sum(dyv * xh, axis=0, keepdims=True)

    row = pl.BlockSpec((tm, d), lambda i: (i, 0))
    vec = pl.BlockSpec((1, d), lambda i: (0, 0))
    in_specs = [row, row, vec] + ([] if dres is None else [row])
    args = [dy, x, g.reshape(1, d)] + ([] if dres is None else [dres])
    return pl.pallas_call(
        body, name=name, grid=(s // tm,), in_specs=in_specs, out_specs=(row, vec),
        out_shape=(jax.ShapeDtypeStruct((s, d), F32), jax.ShapeDtypeStruct((1, d), F32)),
        compiler_params=_cparams(("arbitrary",)),
    )(*args)


def _final_loss(h, target, g, *, name):
    s, d = h.shape
    tm = _pick(s, (256, 128))

    def body(h_ref, t_ref, g_ref, loss_ref, dh_ref, dg_ref):
        i = pl.program_id(0)
        xf = h_ref[...]
        gv = g_ref[...]
        r = lax.rsqrt(jnp.mean(xf * xf, axis=-1, keepdims=True) + EPS)
        xh = xf * r
        err = xh * gv - t_ref[...]
        part = 0.5 * jnp.sum(jnp.mean(err * err, axis=-1, keepdims=True), axis=0, keepdims=True)
        dyv = err * (1.0 / d)
        dyg = dyv * gv
        dh_ref[...] = r * (dyg - xh * jnp.mean(dyg * xh, axis=-1, keepdims=True))

        @pl.when(i == 0)
        def _():
            dg_ref[...] = jnp.zeros_like(dg_ref)
            loss_ref[...] = jnp.zeros_like(loss_ref)

        dg_ref[...] += jnp.sum(dyv * xh, axis=0, keepdims=True)
        loss_ref[...] += jnp.broadcast_to(part, loss_ref.shape)

    row = pl.BlockSpec((tm, d), lambda i: (i, 0))
    vec = pl.BlockSpec((1, d), lambda i: (0, 0))
    return pl.pallas_call(
        body, name=name, grid=(s // tm,), in_specs=[row, row, vec],
        out_specs=(pl.BlockSpec((1, LANES), lambda i: (0, 0)), row, vec),
        out_shape=(jax.ShapeDtypeStruct((1, LANES), F32), jax.ShapeDtypeStruct((s, d), F32), jax.ShapeDtypeStruct((1, d), F32)),
        compiler_params=_cparams(("arbitrary",)),
    )(h, target, g.reshape(1, d))


def _resident(shape, index_map):
    return pl.BlockSpec(shape, index_map, pipeline_mode=pl.Buffered(1))


def _head_lanes(rows):
    lane = lax.broadcasted_iota(jnp.int32, (rows, LANES), 1)
    return [lane < HEAD_DIM, lane >= HEAD_DIM]


def _tri(b, cmp):
    row = lax.broadcasted_iota(jnp.int32, (b, b), 0)
    col = lax.broadcasted_iota(jnp.int32, (b, b), 1)
    return cmp(row, col)


def _twice(mask):
    return jnp.concatenate([mask, mask], axis=0)


def _stack_heads(x, heads):
    zero = jnp.zeros_like(x)
    return jnp.concatenate([jnp.where(heads[0], x, zero), jnp.where(heads[1], x, zero)], axis=0)


def _unstack_heads(x2, heads):
    b = x2.shape[0] // 2
    return jnp.where(heads[0], x2[:b], x2[b:])


def _stack_stat(stat):
    return jnp.concatenate([stat[:, 0:1], stat[:, HEAD_DIM:HEAD_DIM + 1]], axis=0)


def _unstack_stat(col, heads):
    b = col.shape[0] // 2
    return jnp.where(heads[0], jnp.broadcast_to(col[:b], (b, LANES)), jnp.broadcast_to(col[b:], (b, LANES)))


def _tri_dot(x, tri_bf16):
    return _dot(x.astype(BF16), tri_bf16, NN)


def _prefix_sums(x, tri_bf16, inclusive):
    sub = tri_bf16.shape[0]
    outs, carry = [], None
    for c in range(x.shape[1] // sub):
        xs = x[:, c * sub:(c + 1) * sub]
        loc = _tri_dot(xs, tri_bf16)
        outs.append(loc if carry is None else loc + carry)
        tot = loc[:, sub - 1:sub] if inclusive else loc[:, sub - 1:sub] + xs[:, sub - 1:sub]
        carry = tot if carry is None else carry + tot
    return (outs[0] if len(outs) == 1 else jnp.concatenate(outs, axis=1)), carry


def _suffix_sums(x, tri_bf16):
    sub = tri_bf16.shape[0]
    n = x.shape[1] // sub
    outs, carry = [None] * n, None
    for c in reversed(range(n)):
        xs = x[:, c * sub:(c + 1) * sub]
        loc = _tri_dot(xs, tri_bf16)
        outs[c] = loc if carry is None else loc + carry
        tot = loc[:, 0:1] + xs[:, 0:1].astype(BF16).astype(F32)
        carry = tot if carry is None else carry + tot
    return (outs[0] if n == 1 else jnp.concatenate(outs, axis=1)), carry


def _softplus2(z):
    z2 = z * LOG2E
    e = jnp.exp2(-jnp.abs(z2))
    return z2, jnp.maximum(z2, 0.0) + jnp.log2(1.0 + e), e


def _block_rows(j, b):
    return pl.ds(pl.multiple_of(j * b, b), b)


def _sb_fwd(proj, *, name):
    s = proj.shape[0]
    b = _pick(s, (ATT_B, 128))

    def body(q_ref, k_ref, v_ref, o_ref, tot_ref, acc_ref):
        i = pl.program_id(1)
        heads = _head_lanes(b)
        suffix = _tri(min(b, CUM_SUB), lambda r, c: r > c).astype(BF16)
        strict = _twice(_tri(b, lambda r, c: c < r))
        q2 = _stack_heads(q_ref[...] * QK_SCALE, heads)

        def tile(j, a, masked):
            rows = _block_rows(j, b)
            z2, sp, _ = _softplus2(_dot(q2, k_ref[rows, :], NT))
            if masked:
                sp = jnp.where(strict, sp, 0.0)
            rsum, total = _suffix_sums(sp, suffix)
            w = jnp.exp2((z2 - sp) - (a + rsum))
            if masked:
                w = jnp.where(strict, w, 0.0)
            acc_ref[...] += _dot(w.astype(BF16), v_ref[rows, :], NN)
            return a + total

        acc_ref[...] = jnp.zeros_like(acc_ref)
        a = tile(i, jnp.zeros((2 * b, 1), F32), True)
        a = lax.fori_loop(0, i, lambda jj, a: tile(i - 1 - jj, a, False), a)
        o_ref[...] = _unstack_heads(acc_ref[...], heads).astype(BF16)
        tot_ref[0] = _unstack_stat(a, heads)

    qblk = pl.BlockSpec((b, LANES), lambda p, i: (i, p))
    return pl.pallas_call(
        body, name=name, grid=(N_PAIRS, s // b),
        in_specs=[qblk, _resident((s, LANES), lambda p, i: (0, N_PAIRS + p)), _resident((s, LANES), lambda p, i: (0, 2 * N_PAIRS + p))],
        out_specs=(qblk, pl.BlockSpec((1, b, LANES), lambda p, i: (p, i, 0))),
        out_shape=(jax.ShapeDtypeStruct((s, MIX_W), BF16), jax.ShapeDtypeStruct((N_PAIRS, s, LANES), F32)),
        scratch_shapes=[pltpu.VMEM((2 * b, LANES), F32)],
        compiler_params=_cparams(("parallel", "arbitrary")),
    )(proj, proj, proj)


def _sb_bwd(proj, dmerged, tot, *, name):
    s = proj.shape[0]
    b = _pick(s, (ATT_B, 128))

    def body(q_ref, k_ref, v_ref, do_ref, tot_ref, dq_ref, dk_ref, dv_ref, dq_acc):
        i = pl.program_id(1)

        @pl.when(i == 0)
        def _():
            dk_ref[...] = jnp.zeros_like(dk_ref)
            dv_ref[...] = jnp.zeros_like(dv_ref)

        heads = _head_lanes(b)
        incl = _tri(min(b, CUM_SUB), lambda r, c: r <= c).astype(BF16)
        excl = _tri(min(b, CUM_SUB), lambda r, c: r < c).astype(BF16)
        strict = _twice(_tri(b, lambda r, c: c < r))
        q2 = _stack_heads(q_ref[...] * QK_SCALE, heads)
        do2 = _stack_heads(do_ref[...], heads)
        tot2 = _stack_stat(tot_ref[0])

        def tile(j, pre, gpre, masked):
            rows = _block_rows(j, b)
            kb = k_ref[rows, :]
            z2, sp, e = _softplus2(_dot(q2, kb, NT))
            sig = jnp.where(z2 >= 0.0, 1.0, e) * pl.reciprocal(1.0 + e, approx=True)
            if masked:
                sp = jnp.where(strict, sp, 0.0)
            pin, ptot = _prefix_sums(sp, incl, True)
            w = jnp.exp2((z2 - sp) + (pin + (pre - tot2)))
            if masked:
                w = jnp.where(strict, w, 0.0)
            gw = _dot(do2, v_ref[rows, :], NT) * w
            gex, gtot = _prefix_sums(gw, excl, False)
            dz = gw * (1.0 - sig) - sig * (gpre + gex)
            if masked:
                dz = jnp.where(strict, dz, 0.0)
            dzb = dz.astype(BF16)
            dq_acc[...] += _dot(dzb, kb, NN)
            dk_ref[rows, :] += _dot(dzb, q2, TN)
            dv_ref[rows, :] += _dot(w.astype(BF16), do2, TN)
            return pre + ptot, gpre + gtot

        dq_acc[...] = jnp.zeros_like(dq_acc)
        zero = jnp.zeros((2 * b, 1), F32)
        pre, gpre = lax.fori_loop(0, i, lambda j, c: tile(j, c[0], c[1], False), (zero, zero))
        tile(i, pre, gpre, True)
        dq_ref[...] = (_unstack_heads(dq_acc[...], heads) * QK_SCALE).astype(BF16)

    qblk = pl.BlockSpec((b, LANES), lambda p, i: (i, p))
    full = _resident((s, LANES), lambda p, i: (0, p))
    return pl.pallas_call(
        body, name=name, grid=(N_PAIRS, s // b),
        in_specs=[qblk, _resident((s, LANES), lambda p, i: (0, N_PAIRS + p)), _resident((s, LANES), lambda p, i: (0, 2 * N_PAIRS + p)),
                  qblk, pl.BlockSpec((1, b, LANES), lambda p, i: (p, i, 0))],
        out_specs=(qblk, full, full),
        out_shape=(jax.ShapeDtypeStruct((s, MIX_W), BF16), jax.ShapeDtypeStruct((s, MIX_W), F32), jax.ShapeDtypeStruct((s, MIX_W), F32)),
        scratch_shapes=[pltpu.VMEM((2 * b, LANES), F32)],
        compiler_params=_cparams(("parallel", "arbitrary")),
    )(proj, proj, proj, dmerged, tot)


def _fox_fwd(proj, kv, c_col, c_row, *, name):
    s = proj.shape[0]
    b = _pick(s, (ATT_B, 128))

    def body(q_ref, k_ref, v_ref, cc_ref, cr_ref, o_ref, lse_ref, acc_ref):
        i = pl.program_id(1)
        heads = _head_lanes(b)
        causal = _twice(_tri(b, lambda r, c: c <= r))
        top = lax.broadcasted_iota(jnp.int32, (2 * b, b), 0) < b
        q2 = _stack_heads(q_ref[...] * QK_SCALE, heads)
        c_t = _stack_stat(cc_ref[0])

        def tile(j, m, l, masked):
            rows = _block_rows(j, b)
            c_s = jnp.where(top, cr_ref[0, 0:1, rows], cr_ref[0, 1:2, rows])
            sc = _dot(q2, k_ref[rows, :], NT) + (c_t - c_s)
            if masked:
                sc = jnp.where(causal, sc, NEG_INF)
            m_new = jnp.maximum(m, jnp.max(sc, axis=1, keepdims=True))
            p = jnp.exp(sc - m_new)
            alpha = jnp.exp(m - m_new)
            acc_ref[...] = alpha * acc_ref[...] + _dot(p.astype(BF16), v_ref[rows, :], NN)
            return m_new, alpha * l + jnp.sum(p, axis=1, keepdims=True)

        acc_ref[...] = jnp.zeros_like(acc_ref)
        init = (jnp.full((2 * b, 1), NEG_INF, F32), jnp.zeros((2 * b, 1), F32))
        m, l = lax.fori_loop(0, i, lambda j, c: tile(j, c[0], c[1], False), init)
        m, l = tile(i, m, l, True)
        o_ref[...] = _unstack_heads(acc_ref[...] * (1.0 / l), heads).astype(BF16)
        lse_ref[0] = _unstack_stat(m + jnp.log(l), heads)

    qblk = pl.BlockSpec((b, LANES), lambda p, i: (i, p))
    stat = pl.BlockSpec((1, b, LANES), lambda p, i: (p, i, 0))
    return pl.pallas_call(
        body, name=name, grid=(N_PAIRS, s // b),
        in_specs=[qblk, _resident((s, LANES), lambda p, i: (0, p)), _resident((s, LANES), lambda p, i: (0, N_PAIRS + p)),
                  stat, _resident((1, 8, s), lambda p, i: (p, 0, 0))],
        out_specs=(qblk, stat),
        out_shape=(jax.ShapeDtypeStruct((s, MIX_W), BF16), jax.ShapeDtypeStruct((N_PAIRS, s, LANES), F32)),
        scratch_shapes=[pltpu.VMEM((2 * b, LANES), F32)],
        compiler_params=_cparams(("parallel", "arbitrary")),
    )(proj, kv, kv, c_col, c_row)


def _fox_bwd(proj, kv, c_col, c_row, merged, dmerged, lse, *, name):
    s = proj.shape[0]
    b = _pick(s, (ATT_B, 128))

    def body(q_ref, k_ref, v_ref, cc_ref, cr_ref, o_ref, do_ref, lse_ref, dq_ref, dk_ref, dv_ref, dc_ref, dcq_ref, dq_acc):
        i = pl.program_id(1)

        @pl.when(i == 0)
        def _():
            dk_ref[...] = jnp.zeros_like(dk_ref)
            dv_ref[...] = jnp.zeros_like(dv_ref)
            dc_ref[...] = jnp.zeros_like(dc_ref)

        heads = _head_lanes(b)
        causal = _twice(_tri(b, lambda r, c: c <= r))
        top = lax.broadcasted_iota(jnp.int32, (2 * b, b), 0) < b
        q2 = _stack_heads(q_ref[...] * QK_SCALE, heads)
        dov = do_ref[...]
        do2 = _stack_heads(dov, heads)
        prod = dov.astype(F32) * o_ref[...].astype(F32)
        delta = jnp.concatenate([jnp.sum(jnp.where(heads[hh], prod, 0.0), axis=1, keepdims=True) for hh in range(2)], axis=0)
        c_t = _stack_stat(cc_ref[0])
        lse_t = _stack_stat(lse_ref[0])

        def tile(j, rsum, masked):
            rows = _block_rows(j, b)
            kb = k_ref[rows, :]
            c_s = jnp.where(top, cr_ref[0, 0:1, rows], cr_ref[0, 1:2, rows])
            sc = _dot(q2, kb, NT) + (c_t - c_s)
            p = jnp.exp(sc - lse_t)
            if masked:
                p = jnp.where(causal, p, 0.0)
            ds = p * (_dot(do2, v_ref[rows, :], NT) - delta)
            dsb = ds.astype(BF16)
            dq_acc[...] += _dot(dsb, kb, NN)
            dk_ref[rows, :] += _dot(dsb, q2, TN)
            dv_ref[rows, :] += _dot(p.astype(BF16), do2, TN)
            dc_ref[0, 0:1, rows] -= jnp.sum(ds[:b], axis=0, keepdims=True)
            dc_ref[0, 1:2, rows] -= jnp.sum(ds[b:], axis=0, keepdims=True)
            return rsum + jnp.sum(ds, axis=1, keepdims=True)

        dq_acc[...] = jnp.zeros_like(dq_acc)
        rsum = lax.fori_loop(0, i, lambda j, r: tile(j, r, False), jnp.zeros((2 * b, 1), F32))
        rsum = tile(i, rsum, True)
        dq_ref[...] = (_unstack_heads(dq_acc[...], heads) * QK_SCALE).astype(BF16)
        dcq_ref[0] = _unstack_stat(rsum, heads)

    qblk = pl.BlockSpec((b, LANES), lambda p, i: (i, p))
    stat = pl.BlockSpec((1, b, LANES), lambda p, i: (p, i, 0))
    crow = _resident((1, 8, s), lambda p, i: (p, 0, 0))
    full = _resident((s, LANES), lambda p, i: (0, p))
    return pl.pallas_call(
        body, name=name, grid=(N_PAIRS, s // b),
        in_specs=[qblk, full, _resident((s, LANES), lambda p, i: (0, N_PAIRS + p)), stat, crow, qblk, qblk, stat],
        out_specs=(qblk, full, full, crow, stat),
        out_shape=(jax.ShapeDtypeStruct((s, MIX_W), BF16), jax.ShapeDtypeStruct((s, MIX_W), F32), jax.ShapeDtypeStruct((s, MIX_W), F32),
                   jax.ShapeDtypeStruct((N_PAIRS, 8, s), F32), jax.ShapeDtypeStruct((N_PAIRS, s, LANES), F32)),
        scratch_shapes=[pltpu.VMEM((2 * b, LANES), F32)],
        compiler_params=_cparams(("parallel", "arbitrary")),
    )(proj, kv, kv, c_col, c_row, merged, dmerged, lse)


MEM_TQ = 256


def _mem_fwd(proj, q_col_block, mkv, *, name):
    s = proj.shape[0]
    tq = _pick(s, (MEM_TQ, 128))
    n_mem = mkv.shape[0]

    def body(q_ref, mkv_ref, o_ref):
        heads = _head_lanes(tq)
        for pp in range(MEM_W // LANES):
            cols = slice(pp * LANES, (pp + 1) * LANES)
            qv = q_ref[:, cols] * QK_SCALE
            mk = mkv_ref[:, pp * LANES:(pp + 1) * LANES]
            mv = mkv_ref[:, MEM_W + pp * LANES:MEM_W + (pp + 1) * LANES]
            o_sel = None
            for hh in range(2):
                qm = jnp.where(heads[hh], qv, jnp.zeros_like(qv))
                sc = _dot(qm, mk, NT)
                p = jnp.exp(sc - jnp.max(sc, axis=1, keepdims=True))
                p = p / jnp.sum(p, axis=1, keepdims=True)
                out = _dot(p.astype(BF16), mv, NN)
                o_sel = out if hh == 0 else jnp.where(heads[0], o_sel, out)
            o_ref[:, cols] = o_sel.astype(BF16)

    return pl.pallas_call(
        body, name=name, grid=(s // tq,),
        in_specs=[pl.BlockSpec((tq, MEM_W), lambda i: (i, q_col_block)), pl.BlockSpec((n_mem, 2 * MEM_W), lambda i: (0, 0))],
        out_specs=pl.BlockSpec((tq, MEM_W), lambda i: (i, 0)),
        out_shape=jax.ShapeDtypeStruct((s, MEM_W), BF16),
        compiler_params=_cparams(("parallel",)),
    )(proj, mkv)


def _mem_bwd(proj, q_col_block, mkv, dmerged, *, name):
    s = proj.shape[0]
    tq = _pick(s, (MEM_TQ, 128))
    n_mem = mkv.shape[0]

    def body(q_ref, mkv_ref, do_ref, dq_ref, dmkv_ref):
        i = pl.program_id(0)

        @pl.when(i == 0)
        def _():
            dmkv_ref[...] = jnp.zeros_like(dmkv_ref)

        heads = _head_lanes(tq)
        for pp in range(MEM_W // LANES):
            cols = slice(pp * LANES, (pp + 1) * LANES)
            vcols = slice(MEM_W + pp * LANES, MEM_W + (pp + 1) * LANES)
            qv = q_ref[:, cols] * QK_SCALE
            dov = do_ref[:, cols]
            mk = mkv_ref[:, cols]
            mv = mkv_ref[:, vcols]
            dq_sel = None
            for hh in range(2):
                qm = jnp.where(heads[hh], qv, jnp.zeros_like(qv))
                dom = jnp.where(heads[hh], dov, jnp.zeros_like(dov))
                sc = _dot(qm, mk, NT)
                p = jnp.exp(sc - jnp.max(sc, axis=1, keepdims=True))
                p = p / jnp.sum(p, axis=1, keepdims=True)
                dp = _dot(dom, mv, NT)
                ds = p * (dp - jnp.sum(p * dp, axis=1, keepdims=True))
                dsb = ds.astype(BF16)
                dq = _dot(dsb, mk, NN)
                dmkv_ref[:, cols] += _dot(dsb, qm, TN)
                dmkv_ref[:, vcols] += _dot(p.astype(BF16), dom, TN)
                dq_sel = dq if hh == 0 else jnp.where(heads[0], dq_sel, dq)
            dq_ref[:, cols] = (dq_sel * QK_SCALE).astype(BF16)

    return pl.pallas_call(
        body, name=name, grid=(s // tq,),
        in_specs=[pl.BlockSpec((tq, MEM_W), lambda i: (i, q_col_block)), pl.BlockSpec((n_mem, 2 * MEM_W), lambda i: (0, 0)),
                  pl.BlockSpec((tq, MEM_W), lambda i: (i, MIX_W // MEM_W))],
        out_specs=(pl.BlockSpec((tq, MEM_W), lambda i: (i, 0)), pl.BlockSpec((n_mem, 2 * MEM_W), lambda i: (0, 0))),
        out_shape=(jax.ShapeDtypeStruct((s, MEM_W), BF16), jax.ShapeDtypeStruct((n_mem, 2 * MEM_W), F32)),
        compiler_params=_cparams(("arbitrary",)),
    )(proj, mkv, dmerged)


GATE_TB = 256


def _split3_dot(tri_bf16, x):
    x1 = x.astype(BF16)
    r1 = x - x1.astype(F32)
    x2 = r1.astype(BF16)
    x3 = (r1 - x2.astype(F32)).astype(BF16)
    return _dot(tri_bf16, x1, NN) + _dot(tri_bf16, x2, NN) + _dot(tri_bf16, x3, NN)


def _gate_fwd(f, b, *, name):
    s = f.shape[0]
    tb = _pick(s, (GATE_TB, 128))

    def body(f_ref, b_ref, c_ref, carry_ref):
        i = pl.program_id(0)

        @pl.when(i == 0)
        def _():
            carry_ref[...] = jnp.zeros_like(carry_ref)

        x = f_ref[...] + b_ref[...]
        lf = jnp.minimum(x, 0.0) - jnp.log1p(jnp.exp(-jnp.abs(x)))
        row = lax.broadcasted_iota(jnp.int32, (tb, tb), 0)
        col = lax.broadcasted_iota(jnp.int32, (tb, tb), 1)
        lower = (col <= row).astype(BF16)
        c = carry_ref[...] + _split3_dot(lower, lf)
        c_ref[...] = c
        carry_ref[...] = c[tb - 1:tb, :]

    return pl.pallas_call(
        body, name=name, grid=(s // tb,),
        in_specs=[pl.BlockSpec((tb, LANES), lambda i: (i, 0)), pl.BlockSpec((1, LANES), lambda i: (0, 0))],
        out_specs=pl.BlockSpec((tb, LANES), lambda i: (i, 0)),
        out_shape=jax.ShapeDtypeStruct((s, LANES), F32),
        scratch_shapes=[pltpu.VMEM((1, LANES), F32)],
        compiler_params=_cparams(("arbitrary",)),
    )(f, b)


def _gate_bwd(f, b, dc, *, name):
    s = f.shape[0]
    tb = _pick(s, (GATE_TB, 128))
    nb = s // tb

    def body(f_ref, b_ref, dc_ref, df_ref, db_ref, carry_ref):
        i = pl.program_id(0)

        @pl.when(i == 0)
        def _():
            carry_ref[...] = jnp.zeros_like(carry_ref)
            db_ref[...] = jnp.zeros_like(db_ref)

        row = lax.broadcasted_iota(jnp.int32, (tb, tb), 0)
        col = lax.broadcasted_iota(jnp.int32, (tb, tb), 1)
        upper = (col >= row).astype(BF16)
        dlf = carry_ref[...] + _split3_dot(upper, dc_ref[...])
        carry_ref[...] = dlf[0:1, :]
        x = f_ref[...] + b_ref[...]
        e = jnp.exp(-jnp.abs(x))
        one_minus_sig = jnp.where(x >= 0.0, e, 1.0) / (1.0 + e)
        df = dlf * one_minus_sig
        df_ref[...] = df
        db_ref[...] += jnp.sum(df, axis=0, keepdims=True)

    rev = pl.BlockSpec((tb, LANES), lambda i: (nb - 1 - i, 0))
    vec = pl.BlockSpec((1, LANES), lambda i: (0, 0))
    return pl.pallas_call(
        body, name=name, grid=(nb,), in_specs=[rev, vec, rev], out_specs=(rev, vec),
        out_shape=(jax.ShapeDtypeStruct((s, LANES), F32), jax.ShapeDtypeStruct((1, LANES), F32)),
        scratch_shapes=[pltpu.VMEM((1, LANES), F32)],
        compiler_params=_cparams(("arbitrary",)),
    )(f, b, dc)


def _all_gather(x, *, name):
    r, cdim = x.shape

    def body(x_ref, out_ref, send_sems, recv_sems, local_sem):
        mx, my, mc = lax.axis_index("x"), lax.axis_index("y"), lax.axis_index("c")
        me, sibling = (mx, my, mc), (mx, my, 1 - mc)
        chips = [(1 - mx, my), (mx, 1 - my), (1 - mx, 1 - my)]

        def rows(px, py, pc):
            return out_ref.at[pl.ds((4 * px + 2 * py + pc) * r, r), :]

        def copy(k, block, to, src=None):
            return pltpu.make_async_remote_copy(
                src_ref=rows(*block) if src is None else src, dst_ref=rows(*block),
                send_sem=send_sems.at[k], recv_sem=recv_sems.at[k], device_id=to, device_id_type=MESH)

        mine = pltpu.make_async_copy(x_ref, rows(*me), local_sem)
        mine.start()
        first = [copy(0, me, sibling, src=x_ref)]
        first += [copy(1 + j, me, (*chip, mc), src=x_ref) for j, chip in enumerate(chips)]
        for cp in first:
            cp.start()
        passed = [copy(4 + j, (*chip, mc), sibling) for j, chip in enumerate(chips)]
        for j, chip in enumerate(chips):
            copy(1 + j, (*chip, mc), me).wait_recv()
            passed[j].start()
        copy(0, sibling, me).wait_recv()
        for j, chip in enumerate(chips):
            copy(4 + j, (*chip, 1 - mc), me).wait_recv()
        for cp in first + passed:
            cp.wait_send()
        mine.wait()

    return pl.pallas_call(
        body, name=name,
        in_specs=[pl.BlockSpec(memory_space=pl.ANY)], out_specs=pl.BlockSpec(memory_space=pl.ANY),
        out_shape=jax.ShapeDtypeStruct((N_DEV * r, cdim), x.dtype),
        scratch_shapes=[pltpu.SemaphoreType.DMA((7,)), pltpu.SemaphoreType.DMA((7,)), pltpu.SemaphoreType.DMA],
    )(x)


def _exchange_partials(g, *, name):
    def body(g_ref, out_ref, send_sems, recv_sems, local_sem):
        mx, my, mc = lax.axis_index("x"), lax.axis_index("y"), lax.axis_index("c")
        me = 4 * mx + 2 * my + mc
        mine = pltpu.make_async_copy(g_ref.at[me], out_ref.at[me], local_sem)
        mine.start()
        copies = []
        for mask in range(1, N_DEV):
            px = 1 - mx if mask & 4 else mx
            py = 1 - my if mask & 2 else my
            pc = 1 - mc if mask & 1 else mc
            cp = pltpu.make_async_remote_copy(
                src_ref=g_ref.at[4 * px + 2 * py + pc], dst_ref=out_ref.at[me],
                send_sem=send_sems.at[mask - 1], recv_sem=recv_sems.at[mask - 1], device_id=(px, py, pc), device_id_type=MESH)
            cp.start()
            copies.append(cp)
        for cp in copies:
            cp.wait()
        mine.wait()

    return pl.pallas_call(
        body, name=name,
        in_specs=[pl.BlockSpec(memory_space=pl.ANY)], out_specs=pl.BlockSpec(memory_space=pl.ANY),
        out_shape=jax.ShapeDtypeStruct(g.shape, g.dtype),
        scratch_shapes=[pltpu.SemaphoreType.DMA((7,)), pltpu.SemaphoreType.DMA((7,)), pltpu.SemaphoreType.DMA],
    )(g)


def _adamw(parts, w, m, v, *, name):
    r, cdim = w.shape
    tr = _pick(r, (PACK_BLOCK_ROWS, 512, 256, 128, 64, 32, 16, 8))
    c1 = 1.0 / (1.0 - ADAM_B1 ** ADAM_STEP)
    c2 = 1.0 / (1.0 - ADAM_B2 ** ADAM_STEP)

    def body(p_ref, w_ref, m_ref, v_ref, g_ref, d_ref, nm_ref, nv_ref):
        g = p_ref[0].astype(F32)
        for dev in range(1, N_DEV):
            g = g + p_ref[dev].astype(F32)
        mn = ADAM_B1 * m_ref[...] + (1.0 - ADAM_B1) * g
        vn = ADAM_B2 * v_ref[...] + (1.0 - ADAM_B2) * (g * g)
        g_ref[...] = g
        nm_ref[...] = mn
        nv_ref[...] = vn
        d_ref[...] = -ADAM_LR * ((mn * c1) / (jnp.sqrt(vn * c2) + ADAM_EPS) + ADAM_WD * w_ref[...])

    blk = pl.BlockSpec((tr, cdim), lambda i: (i, 0))
    shape = jax.ShapeDtypeStruct((r, cdim), F32)
    return pl.pallas_call(
        body, name=name, grid=(r // tr,),
        in_specs=[pl.BlockSpec((N_DEV, tr, cdim), lambda i: (0, i, 0)), blk, blk, blk],
        out_specs=(blk, blk, blk, blk), out_shape=(shape, shape, shape, shape),
        compiler_params=_cparams(("parallel",)),
    )(parts, w, m, v)


def _rows_of(shape):
    n = math.prod(shape)
    assert n % LANES == 0, shape
    rows = n // LANES
    return -(-rows // PACK_ROW_ALIGN) * PACK_ROW_ALIGN


def _layout(shard_shapes):
    out, off = {}, 0
    for name, _ in SHARDED:
        rows = _rows_of(shard_shapes[name])
        out[name] = (off, rows, tuple(shard_shapes[name]))
        off += rows
    return out, -(-off // PACK_BLOCK_ROWS) * PACK_BLOCK_ROWS


def _pack_shards(layout, total, arrays, dtype):
    parts = []
    for name, _ in SHARDED:
        _, rows, _ = layout[name]
        flat = arrays[name].astype(dtype).reshape(-1, LANES)
        parts.append(jnp.pad(flat, ((0, rows - flat.shape[0]), (0, 0))))
    used = sum(p.shape[0] for p in parts)
    if total > used:
        parts.append(jnp.zeros((total - used, LANES), dtype))
    return jnp.concatenate(parts, axis=0)


def _unpack_shard(layout, flat, name):
    off, _, shape = layout[name]
    n = math.prod(shape) // LANES
    return flat[off:off + n].reshape(shape)


def _unpack_full(layout, gathered, name, axis):
    off, _, shape = layout[name]
    n = math.prod(shape) // LANES
    blocks = gathered[:, off:off + n].reshape((N_DEV,) + shape)
    blocks = jnp.moveaxis(blocks, 0, axis)
    return blocks.reshape(shape[:axis] + (N_DEV * shape[axis],) + shape[axis + 1:])


def _pack_full(layout, total, grads):
    parts = []
    for name, axis in SHARDED:
        _, rows, shape = layout[name]
        g = grads[name]
        blocks = g.reshape(shape[:axis] + (N_DEV, shape[axis]) + shape[axis + 1:])
        blocks = jnp.moveaxis(blocks, axis, 0).reshape(N_DEV, -1, LANES)
        parts.append(jnp.pad(blocks, ((0, 0), (0, rows - blocks.shape[1]), (0, 0))))
    used = sum(p.shape[1] for p in parts)
    if total > used:
        parts.append(jnp.zeros((N_DEV, total - used, LANES), F32))
    return jnp.concatenate(parts, axis=1)


def _pad_lanes(a):
    return jnp.pad(a, ((0, 0), (0, LANES - a.shape[1])))


def _pair_layouts(c):
    s = c.shape[0]
    by_pair = c.T.reshape(N_PAIRS, 2, s)
    c_col = jnp.repeat(by_pair.transpose(0, 2, 1), HEAD_DIM, axis=2)
    c_row = jnp.pad(by_pair, ((0, 0), (0, 6), (0, 0)))
    return c_col, c_row


def _forward_backward(x, mem, target, wts, small):
    n_a = wts["w_in_a"].shape[0]
    n_b = wts["w_in_b"].shape[0]
    depth = n_a + n_b
    w_kv = wts["w_kv_shared"]
    w_kv_kv = w_kv[:, :2 * MIX_W]
    w_kv_f = _pad_lanes(w_kv[:, 2 * MIX_W:])
    b_f = _pad_lanes(small["b_f"].reshape(1, -1))

    saved = []
    shared = None
    h = x
    for l in range(depth):
        is_a = l < n_a
        if l == n_a:
            hs = _rmsnorm_fwd(h, small["kv_norm_g"], name="kv_norm")
            kv = _mm(hs, w_kv_kv, name="kv_proj", out_dtype=BF16)
            f = _mm(hs, w_kv_f, name="gate_proj")
            c = _gate_fwd(f, b_f, name="gate_cumsum")
            c_col, c_row = _pair_layouts(c[:, :N_MIX_HEADS])
            shared = dict(h=h, hs=hs, kv=kv, f=f, c_col=c_col, c_row=c_row)
        hn = _rmsnorm_fwd(h, small["norm1_g"][l], name=f"norm1_{l}")
        memn = _rmsnorm_fwd(mem, small["mem_norm_g"][l], name=f"mem_norm_{l}")
        mkv = _mm(memn, wts["w_mem_kv"][l], name=f"mem_kv_{l}", out_dtype=BF16)
        if is_a:
            proj = _mm(hn, wts["w_in_a"][l], name=f"in_proj_{l}", out_dtype=BF16)
            mix, stat = _sb_fwd(proj, name=f"sb_fwd_{l}")
            q_block = 3 * MIX_W // MEM_W
        else:
            proj = _mm(hn, wts["w_in_b"][l - n_a], name=f"in_proj_{l}", out_dtype=BF16)
            mix, stat = _fox_fwd(proj, shared["kv"], shared["c_col"], shared["c_row"], name=f"fox_fwd_{l}")
            q_block = MIX_W // MEM_W
        mem_out = _mem_fwd(proj, q_block, mkv, name=f"mem_fwd_{l}")
        merged = jnp.concatenate([mix, mem_out], axis=1)
        h_mid = _mm(merged, wts["w_o"][l], name=f"o_proj_{l}", res=h)
        h2n = _rmsnorm_fwd(h_mid, small["norm2_g"][l], name=f"norm2_{l}")
        u, act = _mm(h2n, wts["w_mlp1"][l], name=f"mlp1_{l}", epilogue="relu2")
        h_out = _mm(act, wts["w_mlp2"][l], name=f"mlp2_{l}", res=h_mid)
        saved.append(dict(h=h, hn=hn, memn=memn, mkv=mkv, proj=proj, stat=stat, merged=merged, h_mid=h_mid, h2n=h2n, u=u, act=act,
                          q_block=q_block))
        h = h_out

    loss, dh, dg_final = _final_loss(h, target, small["final_norm_g"], name="final_loss")

    g_w = {k: [None] * wts[k].shape[0] for k in ("w_in_a", "w_in_b", "w_mem_kv", "w_o", "w_mlp1", "w_mlp2")}
    g_n = {k: [None] * depth for k in ("norm1_g", "mem_norm_g", "norm2_g")}
    dk_sh = dv_sh = dc_sh = dcq_sh = None
    for l in reversed(range(depth)):
        sv = saved[l]
        is_a = l < n_a
        du = _mm(dh, wts["w_mlp2"][l], name=f"d_act_{l}", trans_b=True, epilogue="drelu2", u=sv["u"], out_dtype=BF16)
        g_w["w_mlp2"][l] = _mm_tn(sv["act"], dh, name=f"dw_mlp2_{l}")
        g_w["w_mlp1"][l] = _mm_tn(sv["h2n"], du, name=f"dw_mlp1_{l}")
        dh2n = _mm(du, wts["w_mlp1"][l], name=f"d_h2n_{l}", trans_b=True)
        dh_mid, g_n["norm2_g"][l] = _rmsnorm_bwd(dh2n, sv["h_mid"], small["norm2_g"][l], dh, name=f"norm2_bwd_{l}")
        dmerged = _mm(dh_mid, wts["w_o"][l], name=f"d_merged_{l}", trans_b=True, out_dtype=BF16)
        g_w["w_o"][l] = _mm_tn(sv["merged"], dh_mid, name=f"dw_o_{l}")
        dqm, dmkv = _mem_bwd(sv["proj"], sv["q_block"], sv["mkv"], dmerged, name=f"mem_bwd_{l}")
        if is_a:
            dq, dk, dv = _sb_bwd(sv["proj"], dmerged, sv["stat"], name=f"sb_bwd_{l}")
            dproj = jnp.concatenate([dq, dk.astype(BF16), dv.astype(BF16), dqm], axis=1)
            w_in, key, idx = wts["w_in_a"][l], "w_in_a", l
        else:
            dq, dk, dv, dc, dcq = _fox_bwd(sv["proj"], shared["kv"], shared["c_col"], shared["c_row"], sv["merged"], dmerged, sv["stat"],
                                           name=f"fox_bwd_{l}")
            dk_sh = dk if dk_sh is None else dk_sh + dk
            dv_sh = dv if dv_sh is None else dv_sh + dv
            dc_sh = dc if dc_sh is None else dc_sh + dc
            dcq_sh = dcq if dcq_sh is None else dcq_sh + dcq
            dproj = jnp.concatenate([dq, dqm], axis=1)
            w_in, key, idx = wts["w_in_b"][l - n_a], "w_in_b", l - n_a
        g_w[key][idx] = _mm_tn(sv["hn"], dproj, name=f"dw_in_{l}")
        dhn = _mm(dproj, w_in, name=f"d_hn_{l}", trans_b=True)
        dh, g_n["norm1_g"][l] = _rmsnorm_bwd(dhn, sv["h"], small["norm1_g"][l], dh_mid, name=f"norm1_bwd_{l}")
        g_w["w_mem_kv"][l] = _mm_tn(sv["memn"], dmkv, name=f"dw_mem_kv_{l}")
        dmemn = _mm(dmkv, wts["w_mem_kv"][l], name=f"d_memn_{l}", trans_b=True)
        _, g_n["mem_norm_g"][l] = _rmsnorm_bwd(dmemn, mem, small["mem_norm_g"][l], None, name=f"mem_norm_bwd_{l}")
        if l == n_a:
            s_len = x.shape[0]
            dc_query = dcq_sh[:, :, ::HEAD_DIM].transpose(1, 0, 2).reshape(s_len, N_MIX_HEADS)
            dc_tok = _pad_lanes(dc_sh[:, :2, :].reshape(N_MIX_HEADS, s_len).T + dc_query)
            df, db = _gate_bwd(shared["f"], b_f, dc_tok, name="gate_bwd")
            dkv = jnp.concatenate([dk_sh, dv_sh], axis=1)
            dw_kv_kv = _mm_tn(shared["hs"], dkv, name="dw_kv")
            dw_kv_f = _mm_tn(shared["hs"], df, name="dw_gate")
            dhs = _mm(dkv, w_kv_kv, name="d_hs_kv", trans_b=True)
            dhs = _mm(df, w_kv_f, name="d_hs_gate", trans_b=True, res=dhs)
            dh, dg_kv = _rmsnorm_bwd(dhs, shared["h"], small["kv_norm_g"], dh, name="kv_norm_bwd")
            g_kv = jnp.concatenate([dw_kv_kv, dw_kv_f[:, :N_KV_F]], axis=1)

    grads = {k: jnp.stack(v) for k, v in g_w.items()}
    grads["w_kv_shared"] = g_kv
    d = x.shape[1]
    small_g = dict(
        norm1_g=jnp.concatenate(g_n["norm1_g"], axis=0), mem_norm_g=jnp.concatenate(g_n["mem_norm_g"], axis=0),
        norm2_g=jnp.concatenate(g_n["norm2_g"], axis=0), kv_norm_g=dg_kv.reshape(d), b_f=db[0, :N_KV_F], final_norm_g=dg_final.reshape(d))
    return loss, dh, grads, small_g


def _pack_small(vals, d):
    rows = []
    for name in REPLICATED:
        a = vals[name].astype(F32)
        if name == "b_f":
            a = jnp.pad(a, (0, d - a.shape[0]))
        rows.append(a.reshape(-1, d))
    packed = jnp.concatenate(rows, axis=0)
    pad = -packed.shape[0] % 8
    return jnp.pad(packed, ((0, pad), (0, 0)))


def _unpack_small(packed, shapes):
    out, off = {}, 0
    for name in REPLICATED:
        shape = shapes[name]
        if name == "b_f":
            out[name] = packed[off, :shape[0]]
            off += 1
        else:
            n = math.prod(shape) // packed.shape[1]
            out[name] = packed[off:off + n].reshape(shape)
            off += n
    return out


def kernel(x, mem, norm1_g, w_in_a, w_in_b, w_mem_kv, mem_norm_g, w_o, norm2_g, w_mlp1, w_mlp2, kv_norm_g, w_kv_shared, b_f, final_norm_g, loss_target, m_norm1_g, m_w_in_a, m_w_in_b, m_w_mem_kv, m_mem_norm_g, m_w_o, m_norm2_g, m_w_mlp1, m_w_mlp2, m_kv_norm_g, m_w_kv_shared, m_b_f, m_final_norm_g, v_norm1_g, v_w_in_a, v_w_in_b, v_w_mem_kv, v_mem_norm_g, v_w_o, v_norm2_g, v_w_mlp1, v_w_mlp2, v_kv_norm_g, v_w_kv_shared, v_b_f, v_final_norm_g):
    w = dict(norm1_g=norm1_g, w_in_a=w_in_a, w_in_b=w_in_b, w_mem_kv=w_mem_kv, mem_norm_g=mem_norm_g, w_o=w_o, norm2_g=norm2_g,
             w_mlp1=w_mlp1, w_mlp2=w_mlp2, kv_norm_g=kv_norm_g, w_kv_shared=w_kv_shared, b_f=b_f, final_norm_g=final_norm_g)
    m = dict(norm1_g=m_norm1_g, w_in_a=m_w_in_a, w_in_b=m_w_in_b, w_mem_kv=m_w_mem_kv, mem_norm_g=m_mem_norm_g, w_o=m_w_o,
             norm2_g=m_norm2_g, w_mlp1=m_w_mlp1, w_mlp2=m_w_mlp2, kv_norm_g=m_kv_norm_g, w_kv_shared=m_w_kv_shared, b_f=m_b_f,
             final_norm_g=m_final_norm_g)
    v = dict(norm1_g=v_norm1_g, w_in_a=v_w_in_a, w_in_b=v_w_in_b, w_mem_kv=v_w_mem_kv, mem_norm_g=v_mem_norm_g, w_o=v_w_o,
             norm2_g=v_norm2_g, w_mlp1=v_w_mlp1, w_mlp2=v_w_mlp2, kv_norm_g=v_kv_norm_g, w_kv_shared=v_w_kv_shared, b_f=v_b_f,
             final_norm_g=v_final_norm_g)
    d = x.shape[-1]
    layout, total = _layout({name: w[name].shape for name, _ in SHARDED})

    gathered = _all_gather(_pack_shards(layout, total, w, BF16), name="gather_weights").reshape(N_DEV, total, LANES)
    wts = {name: _unpack_full(layout, gathered, name, axis) for name, axis in SHARDED}
    small = {name: w[name] for name in REPLICATED}

    loss, grad_x, grads, small_g = _forward_backward(x[0], mem[0], loss_target[0], wts, small)

    parts = _exchange_partials(_pack_full(layout, total, grads).astype(BF16), name="exchange_grads")
    g_flat, d_flat, m_flat, v_flat = _adamw(parts, _pack_shards(layout, total, w, F32), _pack_shards(layout, total, m, F32),
                                            _pack_shards(layout, total, v, F32), name="adamw_sharded")

    small_packed = _pack_small(small_g, d)
    n_small = small_packed.shape[0]
    small_parts = _all_gather(small_packed, name="gather_small_grads").reshape(N_DEV, n_small, d)
    gs, ds_, ms, vs = _adamw(small_parts, _pack_small(w, d), _pack_small(m, d), _pack_small(v, d), name="adamw_replicated")

    shapes = {name: w[name].shape for name in REPLICATED}
    out_g, out_d, out_m, out_v = {}, {}, {}, {}
    for flat, small_flat, out in ((g_flat, gs, out_g), (d_flat, ds_, out_d), (m_flat, ms, out_m), (v_flat, vs, out_v)):
        for name, _ in SHARDED:
            out[name] = _unpack_shard(layout, flat, name)
        out.update(_unpack_small(small_flat, shapes))

    loss_total = lax.psum(loss[0, 0], ("x", "y", "c"))
    return (loss_total, grad_x[None], *[out_g[n] for n in WEIGHT_ORDER], *[out_d[n] for n in WEIGHT_ORDER],
            *[out_m[n] for n in WEIGHT_ORDER], *[out_v[n] for n in WEIGHT_ORDER])
```

```python
import functools
import math

import jax
import jax.numpy as jnp
from jax import lax
from jax.experimental import pallas as pl
from jax.experimental.pallas import tpu as pltpu

F32 = jnp.float32
BF16 = jnp.bfloat16

N_DEV = 8
HEAD_DIM = 64
N_MIX_HEADS = 8
N_MEM_HEADS = 4
MIX_W = N_MIX_HEADS * HEAD_DIM
MEM_W = N_MEM_HEADS * HEAD_DIM
N_PAIRS = N_MIX_HEADS // 2
LANES = 128
ATT_B = 512
CUM_SUB = 256
SB_CUT = 64.0
FOX_CUT = 45.0
EPS = 1e-6
NEG_INF = -1e30
QK_SCALE = 1.0 / math.sqrt(HEAD_DIM)
LOG2E = 1.4426950408889634
N_KV_F = 8

ADAM_LR = 0.001
ADAM_B1 = 0.9
ADAM_B2 = 0.999
ADAM_EPS = 1e-08
ADAM_WD = 0.01
ADAM_STEP = 10

VMEM_LIMIT = 56 * 1024 * 1024
PACK_ROW_ALIGN = 16
PACK_BLOCK_ROWS = 1024

MESH = pl.DeviceIdType.MESH

SHARDED = (("w_in_a", 2), ("w_in_b", 1), ("w_mem_kv", 1), ("w_o", 2), ("w_mlp1", 2), ("w_mlp2", 1), ("w_kv_shared", 1))
REPLICATED = ("norm1_g", "mem_norm_g", "norm2_g", "kv_norm_g", "b_f", "final_norm_g")
WEIGHT_ORDER = ("norm1_g", "w_in_a", "w_in_b", "w_mem_kv", "mem_norm_g", "w_o", "norm2_g", "w_mlp1", "w_mlp2",
                "kv_norm_g", "w_kv_shared", "b_f", "final_norm_g")


def _cparams(sem=None):
    return pltpu.CompilerParams(dimension_semantics=sem, vmem_limit_bytes=VMEM_LIMIT)


def _pick(n, cands):
    for c in cands:
        if c <= n and n % c == 0:
            return c
    return n


def _dot(a, b, dims):
    return lax.dot_general(a, b, (dims, ((), ())), preferred_element_type=F32)


NN = ((1,), (0,))
NT = ((1,), (1,))
TN = ((0,), (0,))


def _mm(a, b, *, name, trans_b=False, out_dtype=F32, res=None, epilogue=None, u=None):
    m, k = a.shape
    n = b.shape[0] if trans_b else b.shape[1]
    tm = _pick(m, (1024, 512, 256, 128))
    tn = _pick(n, (896, 768, 512, 384, 256, 128))
    tk = _pick(k, (1024, 896, 768, 512, 256, 128))
    nk = k // tk

    def body(*refs):
        a_ref, b_ref = refs[0], refs[1]
        pos = 2
        res_ref = u_ref = None
        if res is not None:
            res_ref = refs[pos]
            pos += 1
        if u is not None:
            u_ref = refs[pos]
            pos += 1
        outs, acc_ref = refs[pos:-1], refs[-1]
        kk = pl.program_id(2)

        @pl.when(kk == 0)
        def _():
            acc_ref[...] = jnp.zeros_like(acc_ref)

        acc_ref[...] += _dot(a_ref[...].astype(BF16), b_ref[...].astype(BF16), NT if trans_b else NN)

        @pl.when(kk == nk - 1)
        def _():
            acc = acc_ref[...]
            if res_ref is not None:
                acc = res_ref[...] + acc
            if epilogue == "relu2":
                outs[0][...] = acc.astype(BF16)
                r = jnp.maximum(acc, 0.0)
                outs[1][...] = (r * r).astype(BF16)
            elif epilogue == "drelu2":
                outs[0][...] = (acc * (2.0 * jnp.maximum(u_ref[...], 0.0))).astype(out_dtype)
            else:
                outs[0][...] = acc.astype(out_dtype)

    in_specs = [pl.BlockSpec((tm, tk), lambda i, j, kk: (i, kk)),
                pl.BlockSpec((tn, tk), lambda i, j, kk: (j, kk)) if trans_b else pl.BlockSpec((tk, tn), lambda i, j, kk: (kk, j))]
    args = [a, b]
    tile = pl.BlockSpec((tm, tn), lambda i, j, kk: (i, j))
    if res is not None:
        in_specs.append(tile)
        args.append(res)
    if u is not None:
        in_specs.append(tile)
        args.append(u)
    if epilogue == "relu2":
        out_shape = (jax.ShapeDtypeStruct((m, n), BF16), jax.ShapeDtypeStruct((m, n), BF16))
        out_specs = (tile, tile)
    else:
        out_shape = (jax.ShapeDtypeStruct((m, n), out_dtype),)
        out_specs = (tile,)
    outs = pl.pallas_call(
        body, name=name, grid=(m // tm, n // tn, nk), in_specs=in_specs, out_specs=out_specs, out_shape=out_shape,
        scratch_shapes=[pltpu.VMEM((tm, tn), F32)],
        compiler_params=_cparams(("parallel", "parallel", "arbitrary")),
    )(*args)
    return outs if epilogue == "relu2" else outs[0]


def _mm_tn(x, dy, *, name):
    m, k1 = x.shape
    n = dy.shape[1]
    t1 = _pick(k1, (1024, 896, 768, 512, 256, 128))
    tn = _pick(n, (1024, 896, 768, 512, 256, 128))
    tm = _pick(m, (512, 256, 128))
    nm = m // tm

    def body(x_ref, dy_ref, o_ref):
        mm = pl.program_id(2)

        @pl.when(mm == 0)
        def _():
            o_ref[...] = jnp.zeros_like(o_ref)

        o_ref[...] += _dot(x_ref[...].astype(BF16), dy_ref[...].astype(BF16), TN)

    return pl.pallas_call(
        body, name=name, grid=(k1 // t1, n // tn, nm),
        in_specs=[pl.BlockSpec((tm, t1), lambda i, j, mm: (mm, i)), pl.BlockSpec((tm, tn), lambda i, j, mm: (mm, j))],
        out_specs=pl.BlockSpec((t1, tn), lambda i, j, mm: (i, j)),
        out_shape=jax.ShapeDtypeStruct((k1, n), F32),
        compiler_params=_cparams(("parallel", "parallel", "arbitrary")),
    )(x, dy)


def _rmsnorm_fwd(x, g, *, name):
    s, d = x.shape
    tm = _pick(s, (512, 256, 128))

    def body(x_ref, g_ref, o_ref):
        xf = x_ref[...]
        r = lax.rsqrt(jnp.mean(xf * xf, axis=-1, keepdims=True) + EPS)
        o_ref[...] = (xf * r * g_ref[...]).astype(BF16)

    return pl.pallas_call(
        body, name=name, grid=(s // tm,),
        in_specs=[pl.BlockSpec((tm, d), lambda i: (i, 0)), pl.BlockSpec((1, d), lambda i: (0, 0))],
        out_specs=pl.BlockSpec((tm, d), lambda i: (i, 0)),
        out_shape=jax.ShapeDtypeStruct((s, d), BF16),
        compiler_params=_cparams(("parallel",)),
    )(x, g.reshape(1, d))


def _rmsnorm_bwd(dy, x, g, dres, *, name):
    s, d = x.shape
    tm = _pick(s, (256, 128))

    def body(*refs):
        if dres is None:
            dy_ref, x_ref, g_ref, dx_ref, dg_ref = refs
            dres_ref = None
        else:
            dy_ref, x_ref, g_ref, dres_ref, dx_ref, dg_ref = refs
        i = pl.program_id(0)
        xf = x_ref[...]
        dyv = dy_ref[...]
        r = lax.rsqrt(jnp.mean(xf * xf, axis=-1, keepdims=True) + EPS)
        xh = xf * r
        dyg = dyv * g_ref[...]
        dx = r * (dyg - xh * jnp.mean(dyg * xh, axis=-1, keepdims=True))
        dx_ref[...] = dx if dres_ref is None else dres_ref[...] + dx

        @pl.when(i == 0)
        def _():
            dg_ref[...] = jnp.zeros_like(dg_ref)

        dg_ref[...] += jnp.sum(dyv * xh, axis=0, keepdims=True)

    row = pl.BlockSpec((tm, d), lambda i: (i, 0))
    vec = pl.BlockSpec((1, d), lambda i: (0, 0))
    in_specs = [row, row, vec] + ([] if dres is None else [row])
    args = [dy, x, g.reshape(1, d)] + ([] if dres is None else [dres])
    return pl.pallas_call(
        body, name=name, grid=(s // tm,), in_specs=in_specs, out_specs=(row, vec),
        out_shape=(jax.ShapeDtypeStruct((s, d), F32), jax.ShapeDtypeStruct((1, d), F32)),
        compiler_params=_cparams(("arbitrary",)),
    )(*args)


def _final_loss(h, target, g, *, name):
    s, d = h.shape
    tm = _pick(s, (256, 128))

    def body(h_ref, t_ref, g_ref, loss_ref, dh_ref, dg_ref):
        i = pl.program_id(0)
        xf = h_ref[...]
        gv = g_ref[...]
        r = lax.rsqrt(jnp.mean(xf * xf, axis=-1, keepdims=True) + EPS)
        xh = xf * r
        err = xh * gv - t_ref[...]
        part = 0.5 * jnp.sum(jnp.mean(err * err, axis=-1, keepdims=True), axis=0, keepdims=True)
        dyv = err * (1.0 / d)
        dyg = dyv * gv
        dh_ref[...] = r * (dyg - xh * jnp.mean(dyg * xh, axis=-1, keepdims=True))

        @pl.when(i == 0)
        def _():
            dg_ref[...] = jnp.zeros_like(dg_ref)
            loss_ref[...] = jnp.zeros_like(loss_ref)

        dg_ref[...] += jnp.sum(dyv * xh, axis=0, keepdims=True)
        loss_ref[...] += jnp.broadcast_to(part, loss_ref.shape)

    row = pl.BlockSpec((tm, d), lambda i: (i, 0))
    vec = pl.BlockSpec((1, d), lambda i: (0, 0))
    return pl.pallas_call(
        body, name=name, grid=(s // tm,), in_specs=[row, row, vec],
        out_specs=(pl.BlockSpec((1, LANES), lambda i: (0, 0)), row, vec),
        out_shape=(jax.ShapeDtypeStruct((1, LANES), F32), jax.ShapeDtypeStruct((s, d), F32), jax.ShapeDtypeStruct((1, d), F32)),
        compiler_params=_cparams(("arbitrary",)),
    )(h, target, g.reshape(1, d))


def _resident(shape, index_map):
    return pl.BlockSpec(shape, index_map, pipeline_mode=pl.Buffered(1))


def _head_lanes(rows):
    lane = lax.broadcasted_iota(jnp.int32, (rows, LANES), 1)
    return [lane < HEAD_DIM, lane >= HEAD_DIM]


def _tri(b, cmp):
    row = lax.broadcasted_iota(jnp.int32, (b, b), 0)
    col = lax.broadcasted_iota(jnp.int32, (b, b), 1)
    return cmp(row, col)


def _twice(mask):
    return jnp.concatenate([mask, mask], axis=0)


def _stack_heads(x, heads):
    zero = jnp.zeros_like(x)
    return jnp.concatenate([jnp.where(heads[0], x, zero), jnp.where(heads[1], x, zero)], axis=0)


def _unstack_heads(x2, heads):
    b = x2.shape[0] // 2
    return jnp.where(heads[0], x2[:b], x2[b:])


def _stack_stat(stat):
    return jnp.concatenate([stat[:, 0:1], stat[:, HEAD_DIM:HEAD_DIM + 1]], axis=0)


def _unstack_stat(col, heads):
    b = col.shape[0] // 2
    return jnp.where(heads[0], jnp.broadcast_to(col[:b], (b, LANES)), jnp.broadcast_to(col[b:], (b, LANES)))


def _tri_dot(x, tri_bf16):
    return _dot(x.astype(BF16), tri_bf16, NN)


def _prefix_sums(x, tri_bf16, inclusive):
    sub = tri_bf16.shape[0]
    outs, carry = [], None
    for c in range(x.shape[1] // sub):
        xs = x[:, c * sub:(c + 1) * sub]
        loc = _tri_dot(xs, tri_bf16)
        outs.append(loc if carry is None else loc + carry)
        tot = loc[:, sub - 1:sub] if inclusive else loc[:, sub - 1:sub] + xs[:, sub - 1:sub]
        carry = tot if carry is None else carry + tot
    return (outs[0] if len(outs) == 1 else jnp.concatenate(outs, axis=1)), carry


def _suffix_sums(x, tri_bf16):
    sub = tri_bf16.shape[0]
    n = x.shape[1] // sub
    outs, carry = [None] * n, None
    for c in reversed(range(n)):
        xs = x[:, c * sub:(c + 1) * sub]
        loc = _tri_dot(xs, tri_bf16)
        outs[c] = loc if carry is None else loc + carry
        tot = loc[:, 0:1] + xs[:, 0:1].astype(BF16).astype(F32)
        carry = tot if carry is None else carry + tot
    return (outs[0] if n == 1 else jnp.concatenate(outs, axis=1)), carry


def _softplus2(z):
    z2 = z * LOG2E
    neg_abs = lax.bitcast_convert_type(lax.bitcast_convert_type(z2, jnp.uint32) | jnp.uint32(0x80000000), F32)
    return z2, jnp.maximum(z2, 0.0) + jnp.log2(1.0 + jnp.exp2(neg_abs))


def _block_rows(j, b):
    return pl.ds(pl.multiple_of(j * b, b), b)


def _sb_fwd(proj, *, name):
    s = proj.shape[0]
    b = _pick(s, (ATT_B, 128))

    def body(q_ref, k_ref, v_ref, o_ref, tot_ref, first_ref, acc_ref):
        i = pl.program_id(1)
        heads = _head_lanes(b)
        suffix = _tri(min(b, CUM_SUB), lambda r, c: r > c).astype(BF16)
        strict = _twice(_tri(b, lambda r, c: c < r))
        q2 = _stack_heads(q_ref[...] * QK_SCALE, heads)

        def tile(j, a, masked):
            rows = _block_rows(j, b)
            z2, sp = _softplus2(_dot(q2, k_ref[rows, :], NT))
            if masked:
                sp = jnp.where(strict, sp, 0.0)
            rsum, total = _suffix_sums(sp, suffix)
            w = jnp.exp2((z2 - sp) - (a + rsum))
            if masked:
                w = jnp.where(strict, w, 0.0)
            acc_ref[...] += _dot(w.astype(BF16), v_ref[rows, :], NN)
            return a + total

        acc_ref[...] = jnp.zeros_like(acc_ref)
        a = tile(i, jnp.zeros((2 * b, 1), F32), True)

        def more(c):
            return jnp.logical_and(c[0] < i, c[2] < SB_CUT)

        def step(c):
            a = tile(i - 1 - c[0], c[1], False)
            return c[0] + 1, a, jnp.min(a)

        done, a, _ = lax.while_loop(more, step, (jnp.int32(0), a, jnp.min(a)))
        o_ref[...] = _unstack_heads(acc_ref[...], heads).astype(BF16)
        tot_ref[0] = _unstack_stat(a, heads)
        first_ref[pl.program_id(0), i] = i - done

    qblk = pl.BlockSpec((b, LANES), lambda p, i: (i, p))
    return pl.pallas_call(
        body, name=name, grid=(N_PAIRS, s // b),
        in_specs=[qblk, _resident((s, LANES), lambda p, i: (0, N_PAIRS + p)), _resident((s, LANES), lambda p, i: (0, 2 * N_PAIRS + p))],
        out_specs=(qblk, pl.BlockSpec((1, b, LANES), lambda p, i: (p, i, 0)), pl.BlockSpec(memory_space=pltpu.SMEM)),
        out_shape=(jax.ShapeDtypeStruct((s, MIX_W), BF16), jax.ShapeDtypeStruct((N_PAIRS, s, LANES), F32),
                   jax.ShapeDtypeStruct((N_PAIRS, s // b), jnp.int32)),
        scratch_shapes=[pltpu.VMEM((2 * b, LANES), F32)],
        compiler_params=_cparams(("arbitrary", "arbitrary")),
    )(proj, proj, proj)


def _sb_bwd(proj, dmerged, tot, first, *, name):
    s = proj.shape[0]
    b = _pick(s, (ATT_B, 128))

    def body(q_ref, k_ref, v_ref, do_ref, tot_ref, first_ref, dq_ref, dk_ref, dv_ref, dq_acc):
        i = pl.program_id(1)

        @pl.when(i == 0)
        def _():
            dk_ref[...] = jnp.zeros_like(dk_ref)
            dv_ref[...] = jnp.zeros_like(dv_ref)

        heads = _head_lanes(b)
        incl = _tri(min(b, CUM_SUB), lambda r, c: r <= c).astype(BF16)
        excl = _tri(min(b, CUM_SUB), lambda r, c: r < c).astype(BF16)
        strict = _twice(_tri(b, lambda r, c: c < r))
        q2 = _stack_heads(q_ref[...] * QK_SCALE, heads)
        do2 = _stack_heads(do_ref[...], heads)
        tot2 = _stack_stat(tot_ref[0])

        def tile(j, pre, gpre, masked):
            rows = _block_rows(j, b)
            kb = k_ref[rows, :]
            z2, sp = _softplus2(_dot(q2, kb, NT))
            oms = jnp.exp2(-sp)
            if masked:
                sp = jnp.where(strict, sp, 0.0)
            pin, ptot = _prefix_sums(sp, incl, True)
            w = jnp.exp2((z2 - sp) + (pin + (pre - tot2)))
            if masked:
                w = jnp.where(strict, w, 0.0)
            gw = _dot(do2, v_ref[rows, :], NT) * w
            gex, gtot = _prefix_sums(gw, excl, False)
            dz = gw * oms - (1.0 - oms) * (gpre + gex)
            if masked:
                dz = jnp.where(strict, dz, 0.0)
            dzb = dz.astype(BF16)
            dq_acc[...] += _dot(dzb, kb, NN)
            dk_ref[rows, :] += _dot(dzb, q2, TN)
            dv_ref[rows, :] += _dot(w.astype(BF16), do2, TN)
            return pre + ptot, gpre + gtot

        dq_acc[...] = jnp.zeros_like(dq_acc)
        zero = jnp.zeros((2 * b, 1), F32)
        pre, gpre = lax.fori_loop(first_ref[pl.program_id(0), i], i, lambda j, c: tile(j, c[0], c[1], False), (zero, zero))
        tile(i, pre, gpre, True)
        dq_ref[...] = (_unstack_heads(dq_acc[...], heads) * QK_SCALE).astype(BF16)

    qblk = pl.BlockSpec((b, LANES), lambda p, i: (i, p))
    full = _resident((s, LANES), lambda p, i: (0, p))
    return pl.pallas_call(
        body, name=name, grid=(N_PAIRS, s // b),
        in_specs=[qblk, _resident((s, LANES), lambda p, i: (0, N_PAIRS + p)), _resident((s, LANES), lambda p, i: (0, 2 * N_PAIRS + p)),
                  qblk, pl.BlockSpec((1, b, LANES), lambda p, i: (p, i, 0)), pl.BlockSpec(memory_space=pltpu.SMEM)],
        out_specs=(qblk, full, full),
        out_shape=(jax.ShapeDtypeStruct((s, MIX_W), BF16), jax.ShapeDtypeStruct((s, MIX_W), F32), jax.ShapeDtypeStruct((s, MIX_W), F32)),
        scratch_shapes=[pltpu.VMEM((2 * b, LANES), F32)],
        compiler_params=_cparams(("parallel", "arbitrary")),
    )(proj, proj, proj, dmerged, tot, first)


def _fox_fwd(proj, kv, c_col, c_row, k_max, *, name):
    s = proj.shape[0]
    b = _pick(s, (ATT_B, 128))

    def body(q_ref, k_ref, v_ref, cc_ref, cr_ref, km_ref, o_ref, lse_ref, first_ref, acc_ref):
        i = pl.program_id(1)
        heads = _head_lanes(b)
        causal = _twice(_tri(b, lambda r, c: c <= r))
        top = lax.broadcasted_iota(jnp.int32, (2 * b, b), 0) < b
        q2 = _stack_heads(q_ref[...] * QK_SCALE, heads)
        c_t = _stack_stat(cc_ref[0])
        qf = q2.astype(F32)
        kmv = km_ref[0]
        z_max = jnp.sqrt(jnp.sum(qf * qf, axis=1, keepdims=True)) * jnp.where(top[:, 0:1], kmv[0:1, 0:1], kmv[1:2, 0:1]) * 1.001

        def tile(j, m, l, masked):
            rows = _block_rows(j, b)
            gate = c_t - jnp.where(top, cr_ref[0, 0:1, rows], cr_ref[0, 1:2, rows])
            sc = _dot(q2, k_ref[rows, :], NT) + gate
            if masked:
                sc = jnp.where(causal, sc, NEG_INF)
            m_new = jnp.maximum(m, jnp.max(sc, axis=1, keepdims=True))
            p = jnp.exp(sc - m_new)
            alpha = jnp.exp(m - m_new)
            acc_ref[...] = alpha * acc_ref[...] + _dot(p.astype(BF16), v_ref[rows, :], NN)
            return m_new, alpha * l + jnp.sum(p, axis=1, keepdims=True), jnp.max(z_max + gate[:, 0:1] - m_new)

        acc_ref[...] = jnp.zeros_like(acc_ref)
        m, l, slack = tile(i, jnp.full((2 * b, 1), NEG_INF, F32), jnp.zeros((2 * b, 1), F32), True)

        def more(c):
            return jnp.logical_and(c[0] < i, c[3] > -FOX_CUT)

        def step(c):
            m, l, slack = tile(i - 1 - c[0], c[1], c[2], False)
            return c[0] + 1, m, l, slack

        done, m, l, _ = lax.while_loop(more, step, (jnp.int32(0), m, l, slack))
        o_ref[...] = _unstack_heads(acc_ref[...] * (1.0 / l), heads).astype(BF16)
        lse_ref[0] = _unstack_stat(m + jnp.log(l), heads)
        first_ref[pl.program_id(0), i] = i - done

    qblk = pl.BlockSpec((b, LANES), lambda p, i: (i, p))
    stat = pl.BlockSpec((1, b, LANES), lambda p, i: (p, i, 0))
    return pl.pallas_call(
        body, name=name, grid=(N_PAIRS, s // b),
        in_specs=[qblk, _resident((s, LANES), lambda p, i: (0, p)), _resident((s, LANES), lambda p, i: (0, N_PAIRS + p)),
                  stat, _resident((1, 8, s), lambda p, i: (p, 0, 0)), pl.BlockSpec((1, 8, LANES), lambda p, i: (p, 0, 0))],
        out_specs=(qblk, stat, pl.BlockSpec(memory_space=pltpu.SMEM)),
        out_shape=(jax.ShapeDtypeStruct((s, MIX_W), BF16), jax.ShapeDtypeStruct((N_PAIRS, s, LANES), F32),
                   jax.ShapeDtypeStruct((N_PAIRS, s // b), jnp.int32)),
        scratch_shapes=[pltpu.VMEM((2 * b, LANES), F32)],
        compiler_params=_cparams(("arbitrary", "arbitrary")),
    )(proj, kv, kv, c_col, c_row, k_max)


def _fox_bwd(proj, kv, c_col, c_row, merged, dmerged, lse, first, *, name):
    s = proj.shape[0]
    b = _pick(s, (ATT_B, 128))

    def body(q_ref, k_ref, v_ref, cc_ref, cr_ref, o_ref, do_ref, lse_ref, first_ref, dq_ref, dk_ref, dv_ref, dc_ref, dcq_ref, dq_acc):
        i = pl.program_id(1)

        @pl.when(i == 0)
        def _():
            dk_ref[...] = jnp.zeros_like(dk_ref)
            dv_ref[...] = jnp.zeros_like(dv_ref)
            dc_ref[...] = jnp.zeros_like(dc_ref)

        heads = _head_lanes(b)
        causal = _twice(_tri(b, lambda r, c: c <= r))
        top = lax.broadcasted_iota(jnp.int32, (2 * b, b), 0) < b
        q2 = _stack_heads(q_ref[...] * QK_SCALE, heads)
        dov = do_ref[...]
        do2 = _stack_heads(dov, heads)
        prod = dov.astype(F32) * o_ref[...].astype(F32)
        delta = jnp.concatenate([jnp.sum(jnp.where(heads[hh], prod, 0.0), axis=1, keepdims=True) for hh in range(2)], axis=0)
        c_t = _stack_stat(cc_ref[0])
        lse_t = _stack_stat(lse_ref[0])

        def tile(j, rsum, masked):
            rows = _block_rows(j, b)
            kb = k_ref[rows, :]
            c_s = jnp.where(top, cr_ref[0, 0:1, rows], cr_ref[0, 1:2, rows])
            sc = _dot(q2, kb, NT) + (c_t - c_s)
            p = jnp.exp(sc - lse_t)
            if masked:
                p = jnp.where(causal, p, 0.0)
            ds = p * (_dot(do2, v_ref[rows, :], NT) - delta)
            dsb = ds.astype(BF16)
            dq_acc[...] += _dot(dsb, kb, NN)
            dk_ref[rows, :] += _dot(dsb, q2, TN)
            dv_ref[rows, :] += _dot(p.astype(BF16), do2, TN)
            dc_ref[0, 0:1, rows] -= jnp.sum(ds[:b], axis=0, keepdims=True)
            dc_ref[0, 1:2, rows] -= jnp.sum(ds[b:], axis=0, keepdims=True)
            return rsum + jnp.sum(ds, axis=1, keepdims=True)

        dq_acc[...] = jnp.zeros_like(dq_acc)
        rsum = lax.fori_loop(first_ref[pl.program_id(0), i], i, lambda j, r: tile(j, r, False), jnp.zeros((2 * b, 1), F32))
        rsum = tile(i, rsum, True)
        dq_ref[...] = (_unstack_heads(dq_acc[...], heads) * QK_SCALE).astype(BF16)
        dcq_ref[0] = _unstack_stat(rsum, heads)

    qblk = pl.BlockSpec((b, LANES), lambda p, i: (i, p))
    stat = pl.BlockSpec((1, b, LANES), lambda p, i: (p, i, 0))
    crow = _resident((1, 8, s), lambda p, i: (p, 0, 0))
    full = _resident((s, LANES), lambda p, i: (0, p))
    return pl.pallas_call(
        body, name=name, grid=(N_PAIRS, s // b),
        in_specs=[qblk, full, _resident((s, LANES), lambda p, i: (0, N_PAIRS + p)), stat, crow, qblk, qblk, stat,
                  pl.BlockSpec(memory_space=pltpu.SMEM)],
        out_specs=(qblk, full, full, crow, stat),
        out_shape=(jax.ShapeDtypeStruct((s, MIX_W), BF16), jax.ShapeDtypeStruct((s, MIX_W), F32), jax.ShapeDtypeStruct((s, MIX_W), F32),
                   jax.ShapeDtypeStruct((N_PAIRS, 8, s), F32), jax.ShapeDtypeStruct((N_PAIRS, s, LANES), F32)),
        scratch_shapes=[pltpu.VMEM((2 * b, LANES), F32)],
        compiler_params=_cparams(("parallel", "arbitrary")),
    )(proj, kv, kv, c_col, c_row, merged, dmerged, lse, first)


MEM_TQ = 256


def _mem_fwd(proj, q_col_block, mkv, *, name):
    s = proj.shape[0]
    tq = _pick(s, (MEM_TQ, 128))
    n_mem = mkv.shape[0]

    def body(q_ref, mkv_ref, o_ref):
        heads = _head_lanes(tq)
        for pp in range(MEM_W // LANES):
            cols = slice(pp * LANES, (pp + 1) * LANES)
            qv = q_ref[:, cols] * QK_SCALE
            mk = mkv_ref[:, pp * LANES:(pp + 1) * LANES]
            mv = mkv_ref[:, MEM_W + pp * LANES:MEM_W + (pp + 1) * LANES]
            o_sel = None
            for hh in range(2):
                qm = jnp.where(heads[hh], qv, jnp.zeros_like(qv))
                sc = _dot(qm, mk, NT)
                p = jnp.exp(sc - jnp.max(sc, axis=1, keepdims=True))
                p = p / jnp.sum(p, axis=1, keepdims=True)
                out = _dot(p.astype(BF16), mv, NN)
                o_sel = out if hh == 0 else jnp.where(heads[0], o_sel, out)
            o_ref[:, cols] = o_sel.astype(BF16)

    return pl.pallas_call(
        body, name=name, grid=(s // tq,),
        in_specs=[pl.BlockSpec((tq, MEM_W), lambda i: (i, q_col_block)), pl.BlockSpec((n_mem, 2 * MEM_W), lambda i: (0, 0))],
        out_specs=pl.BlockSpec((tq, MEM_W), lambda i: (i, 0)),
        out_shape=jax.ShapeDtypeStruct((s, MEM_W), BF16),
        compiler_params=_cparams(("parallel",)),
    )(proj, mkv)


def _mem_bwd(proj, q_col_block, mkv, dmerged, *, name):
    s = proj.shape[0]
    tq = _pick(s, (MEM_TQ, 128))
    n_mem = mkv.shape[0]

    def body(q_ref, mkv_ref, do_ref, dq_ref, dmkv_ref):
        i = pl.program_id(0)

        @pl.when(i == 0)
        def _():
            dmkv_ref[...] = jnp.zeros_like(dmkv_ref)

        heads = _head_lanes(tq)
        for pp in range(MEM_W // LANES):
            cols = slice(pp * LANES, (pp + 1) * LANES)
            vcols = slice(MEM_W + pp * LANES, MEM_W + (pp + 1) * LANES)
            qv = q_ref[:, cols] * QK_SCALE
            dov = do_ref[:, cols]
            mk = mkv_ref[:, cols]
            mv = mkv_ref[:, vcols]
            dq_sel = None
            for hh in range(2):
                qm = jnp.where(heads[hh], qv, jnp.zeros_like(qv))
                dom = jnp.where(heads[hh], dov, jnp.zeros_like(dov))
                sc = _dot(qm, mk, NT)
                p = jnp.exp(sc - jnp.max(sc, axis=1, keepdims=True))
                p = p / jnp.sum(p, axis=1, keepdims=True)
                dp = _dot(dom, mv, NT)
                ds = p * (dp - jnp.sum(p * dp, axis=1, keepdims=True))
                dsb = ds.astype(BF16)
                dq = _dot(dsb, mk, NN)
                dmkv_ref[:, cols] += _dot(dsb, qm, TN)
                dmkv_ref[:, vcols] += _dot(p.astype(BF16), dom, TN)
                dq_sel = dq if hh == 0 else jnp.where(heads[0], dq_sel, dq)
            dq_ref[:, cols] = (dq_sel * QK_SCALE).astype(BF16)

    return pl.pallas_call(
        body, name=name, grid=(s // tq,),
        in_specs=[pl.BlockSpec((tq, MEM_W), lambda i: (i, q_col_block)), pl.BlockSpec((n_mem, 2 * MEM_W), lambda i: (0, 0)),
                  pl.BlockSpec((tq, MEM_W), lambda i: (i, MIX_W // MEM_W))],
        out_specs=(pl.BlockSpec((tq, MEM_W), lambda i: (i, 0)), pl.BlockSpec((n_mem, 2 * MEM_W), lambda i: (0, 0))),
        out_shape=(jax.ShapeDtypeStruct((s, MEM_W), BF16), jax.ShapeDtypeStruct((n_mem, 2 * MEM_W), F32)),
        compiler_params=_cparams(("arbitrary",)),
    )(proj, mkv, dmerged)


GATE_TB = 256


def _split3_dot(tri_bf16, x):
    x1 = x.astype(BF16)
    r1 = x - x1.astype(F32)
    x2 = r1.astype(BF16)
    x3 = (r1 - x2.astype(F32)).astype(BF16)
    return _dot(tri_bf16, x1, NN) + _dot(tri_bf16, x2, NN) + _dot(tri_bf16, x3, NN)


def _gate_fwd(f, b, *, name):
    s = f.shape[0]
    tb = _pick(s, (GATE_TB, 128))

    def body(f_ref, b_ref, c_ref, carry_ref):
        i = pl.program_id(0)

        @pl.when(i == 0)
        def _():
            carry_ref[...] = jnp.zeros_like(carry_ref)

        x = f_ref[...] + b_ref[...]
        lf = jnp.minimum(x, 0.0) - jnp.log1p(jnp.exp(-jnp.abs(x)))
        row = lax.broadcasted_iota(jnp.int32, (tb, tb), 0)
        col = lax.broadcasted_iota(jnp.int32, (tb, tb), 1)
        lower = (col <= row).astype(BF16)
        c = carry_ref[...] + _split3_dot(lower, lf)
        c_ref[...] = c
        carry_ref[...] = c[tb - 1:tb, :]

    return pl.pallas_call(
        body, name=name, grid=(s // tb,),
        in_specs=[pl.BlockSpec((tb, LANES), lambda i: (i, 0)), pl.BlockSpec((1, LANES), lambda i: (0, 0))],
        out_specs=pl.BlockSpec((tb, LANES), lambda i: (i, 0)),
        out_shape=jax.ShapeDtypeStruct((s, LANES), F32),
        scratch_shapes=[pltpu.VMEM((1, LANES), F32)],
        compiler_params=_cparams(("arbitrary",)),
    )(f, b)


def _gate_bwd(f, b, dc, *, name):
    s = f.shape[0]
    tb = _pick(s, (GATE_TB, 128))
    nb = s // tb

    def body(f_ref, b_ref, dc_ref, df_ref, db_ref, carry_ref):
        i = pl.program_id(0)

        @pl.when(i == 0)
        def _():
            carry_ref[...] = jnp.zeros_like(carry_ref)
            db_ref[...] = jnp.zeros_like(db_ref)

        row = lax.broadcasted_iota(jnp.int32, (tb, tb), 0)
        col = lax.broadcasted_iota(jnp.int32, (tb, tb), 1)
        upper = (col >= row).astype(BF16)
        dlf = carry_ref[...] + _split3_dot(upper, dc_ref[...])
        carry_ref[...] = dlf[0:1, :]
        x = f_ref[...] + b_ref[...]
        e = jnp.exp(-jnp.abs(x))
        one_minus_sig = jnp.where(x >= 0.0, e, 1.0) / (1.0 + e)
        df = dlf * one_minus_sig
        df_ref[...] = df
        db_ref[...] += jnp.sum(df, axis=0, keepdims=True)

    rev = pl.BlockSpec((tb, LANES), lambda i: (nb - 1 - i, 0))
    vec = pl.BlockSpec((1, LANES), lambda i: (0, 0))
    return pl.pallas_call(
        body, name=name, grid=(nb,), in_specs=[rev, vec, rev], out_specs=(rev, vec),
        out_shape=(jax.ShapeDtypeStruct((s, LANES), F32), jax.ShapeDtypeStruct((1, LANES), F32)),
        scratch_shapes=[pltpu.VMEM((1, LANES), F32)],
        compiler_params=_cparams(("arbitrary",)),
    )(f, b, dc)


def _all_gather(x, *, name):
    r, cdim = x.shape

    def body(x_ref, out_ref, send_sems, recv_sems, local_sem):
        mx, my, mc = lax.axis_index("x"), lax.axis_index("y"), lax.axis_index("c")
        me, sibling = (mx, my, mc), (mx, my, 1 - mc)
        chips = [(1 - mx, my), (mx, 1 - my), (1 - mx, 1 - my)]

        def rows(px, py, pc):
            return out_ref.at[pl.ds((4 * px + 2 * py + pc) * r, r), :]

        def copy(k, block, to, src=None):
            return pltpu.make_async_remote_copy(
                src_ref=rows(*block) if src is None else src, dst_ref=rows(*block),
                send_sem=send_sems.at[k], recv_sem=recv_sems.at[k], device_id=to, device_id_type=MESH)

        mine = pltpu.make_async_copy(x_ref, rows(*me), local_sem)
        mine.start()
        first = [copy(0, me, sibling, src=x_ref)]
        first += [copy(1 + j, me, (*chip, mc), src=x_ref) for j, chip in enumerate(chips)]
        for cp in first:
            cp.start()
        passed = [copy(4 + j, (*chip, mc), sibling) for j, chip in enumerate(chips)]
        for j, chip in enumerate(chips):
            copy(1 + j, (*chip, mc), me).wait_recv()
            passed[j].start()
        copy(0, sibling, me).wait_recv()
        for j, chip in enumerate(chips):
            copy(4 + j, (*chip, 1 - mc), me).wait_recv()
        for cp in first + passed:
            cp.wait_send()
        mine.wait()

    return pl.pallas_call(
        body, name=name,
        in_specs=[pl.BlockSpec(memory_space=pl.ANY)], out_specs=pl.BlockSpec(memory_space=pl.ANY),
        out_shape=jax.ShapeDtypeStruct((N_DEV * r, cdim), x.dtype),
        scratch_shapes=[pltpu.SemaphoreType.DMA((7,)), pltpu.SemaphoreType.DMA((7,)), pltpu.SemaphoreType.DMA],
    )(x)


def _exchange_partials(g, *, name):
    def body(g_ref, out_ref, send_sems, recv_sems, local_sem):
        mx, my, mc = lax.axis_index("x"), lax.axis_index("y"), lax.axis_index("c")
        me = 4 * mx + 2 * my + mc
        mine = pltpu.make_async_copy(g_ref.at[me], out_ref.at[me], local_sem)
        mine.start()
        copies = []
        for mask in range(1, N_DEV):
            px = 1 - mx if mask & 4 else mx
            py = 1 - my if mask & 2 else my
            pc = 1 - mc if mask & 1 else mc
            cp = pltpu.make_async_remote_copy(
                src_ref=g_ref.at[4 * px + 2 * py + pc], dst_ref=out_ref.at[me],
                send_sem=send_sems.at[mask - 1], recv_sem=recv_sems.at[mask - 1], device_id=(px, py, pc), device_id_type=MESH)
            cp.start()
            copies.append(cp)
        for cp in copies:
            cp.wait()
        mine.wait()

    return pl.pallas_call(
        body, name=name,
        in_specs=[pl.BlockSpec(memory_space=pl.ANY)], out_specs=pl.BlockSpec(memory_space=pl.ANY),
        out_shape=jax.ShapeDtypeStruct(g.shape, g.dtype),
        scratch_shapes=[pltpu.SemaphoreType.DMA((7,)), pltpu.SemaphoreType.DMA((7,)), pltpu.SemaphoreType.DMA],
    )(g)


def _adamw(parts, w, m, v, *, name):
    r, cdim = w.shape
    tr = _pick(r, (PACK_BLOCK_ROWS, 512, 256, 128, 64, 32, 16, 8))
    c1 = 1.0 / (1.0 - ADAM_B1 ** ADAM_STEP)
    c2 = 1.0 / (1.0 - ADAM_B2 ** ADAM_STEP)

    def body(p_ref, w_ref, m_ref, v_ref, g_ref, d_ref, nm_ref, nv_ref):
        g = p_ref[0].astype(F32)
        for dev in range(1, N_DEV):
            g = g + p_ref[dev].astype(F32)
        mn = ADAM_B1 * m_ref[...] + (1.0 - ADAM_B1) * g
        vn = ADAM_B2 * v_ref[...] + (1.0 - ADAM_B2) * (g * g)
        g_ref[...] = g
        nm_ref[...] = mn
        nv_ref[...] = vn
        d_ref[...] = -ADAM_LR * ((mn * c1) / (jnp.sqrt(vn * c2) + ADAM_EPS) + ADAM_WD * w_ref[...])

    blk = pl.BlockSpec((tr, cdim), lambda i: (i, 0))
    shape = jax.ShapeDtypeStruct((r, cdim), F32)
    return pl.pallas_call(
        body, name=name, grid=(r // tr,),
        in_specs=[pl.BlockSpec((N_DEV, tr, cdim), lambda i: (0, i, 0)), blk, blk, blk],
        out_specs=(blk, blk, blk, blk), out_shape=(shape, shape, shape, shape),
        compiler_params=_cparams(("parallel",)),
    )(parts, w, m, v)


def _rows_of(shape):
    n = math.prod(shape)
    assert n % LANES == 0, shape
    rows = n // LANES
    return -(-rows // PACK_ROW_ALIGN) * PACK_ROW_ALIGN


def _layout(shard_shapes):
    out, off = {}, 0
    for name, _ in SHARDED:
        rows = _rows_of(shard_shapes[name])
        out[name] = (off, rows, tuple(shard_shapes[name]))
        off += rows
    return out, -(-off // PACK_BLOCK_ROWS) * PACK_BLOCK_ROWS


def _pack_shards(layout, total, arrays, dtype):
    parts = []
    for name, _ in SHARDED:
        _, rows, _ = layout[name]
        flat = arrays[name].astype(dtype).reshape(-1, LANES)
        parts.append(jnp.pad(flat, ((0, rows - flat.shape[0]), (0, 0))))
    used = sum(p.shape[0] for p in parts)
    if total > used:
        parts.append(jnp.zeros((total - used, LANES), dtype))
    return jnp.concatenate(parts, axis=0)


def _unpack_shard(layout, flat, name):
    off, _, shape = layout[name]
    n = math.prod(shape) // LANES
    return flat[off:off + n].reshape(shape)


def _unpack_full(layout, gathered, name, axis):
    off, _, shape = layout[name]
    n = math.prod(shape) // LANES
    blocks = gathered[:, off:off + n].reshape((N_DEV,) + shape)
    blocks = jnp.moveaxis(blocks, 0, axis)
    return blocks.reshape(shape[:axis] + (N_DEV * shape[axis],) + shape[axis + 1:])


def _pack_full(layout, total, grads):
    parts = []
    for name, axis in SHARDED:
        _, rows, shape = layout[name]
        g = grads[name]
        blocks = g.reshape(shape[:axis] + (N_DEV, shape[axis]) + shape[axis + 1:])
        blocks = jnp.moveaxis(blocks, axis, 0).reshape(N_DEV, -1, LANES)
        parts.append(jnp.pad(blocks, ((0, 0), (0, rows - blocks.shape[1]), (0, 0))))
    used = sum(p.shape[1] for p in parts)
    if total > used:
        parts.append(jnp.zeros((N_DEV, total - used, LANES), F32))
    return jnp.concatenate(parts, axis=1)


def _pad_lanes(a):
    return jnp.pad(a, ((0, 0), (0, LANES - a.shape[1])))


def _pair_layouts(c):
    s = c.shape[0]
    by_pair = c.T.reshape(N_PAIRS, 2, s)
    c_col = jnp.repeat(by_pair.transpose(0, 2, 1), HEAD_DIM, axis=2)
    c_row = jnp.pad(by_pair, ((0, 0), (0, 6), (0, 0)))
    return c_col, c_row


def _key_norm_bound(k):
    norms = jnp.sqrt(jnp.max(jnp.sum(jnp.square(k.astype(F32)).reshape(k.shape[0], N_MIX_HEADS, HEAD_DIM), axis=2), axis=0))
    rows = jnp.pad(norms.reshape(N_PAIRS, 2), ((0, 0), (0, 6)))
    return jnp.broadcast_to(rows[:, :, None], (N_PAIRS, 8, LANES))


def _forward_backward(x, mem, target, wts, small):
    n_a = wts["w_in_a"].shape[0]
    n_b = wts["w_in_b"].shape[0]
    depth = n_a + n_b
    w_kv = wts["w_kv_shared"]
    w_kv_kv = w_kv[:, :2 * MIX_W]
    w_kv_f = _pad_lanes(w_kv[:, 2 * MIX_W:])
    b_f = _pad_lanes(small["b_f"].reshape(1, -1))

    saved = []
    shared = None
    h = x
    for l in range(depth):
        is_a = l < n_a
        if l == n_a:
            hs = _rmsnorm_fwd(h, small["kv_norm_g"], name="kv_norm")
            kv = _mm(hs, w_kv_kv, name="kv_proj", out_dtype=BF16)
            f = _mm(hs, w_kv_f, name="gate_proj")
            c = _gate_fwd(f, b_f, name="gate_cumsum")
            c_col, c_row = _pair_layouts(c[:, :N_MIX_HEADS])
            shared = dict(h=h, hs=hs, kv=kv, f=f, c_col=c_col, c_row=c_row, k_max=_key_norm_bound(kv[:, :MIX_W]))
        hn = _rmsnorm_fwd(h, small["norm1_g"][l], name=f"norm1_{l}")
        memn = _rmsnorm_fwd(mem, small["mem_norm_g"][l], name=f"mem_norm_{l}")
        mkv = _mm(memn, wts["w_mem_kv"][l], name=f"mem_kv_{l}", out_dtype=BF16)
        if is_a:
            proj = _mm(hn, wts["w_in_a"][l], name=f"in_proj_{l}", out_dtype=BF16)
            mix, stat, first = _sb_fwd(proj, name=f"sb_fwd_{l}")
            q_block = 3 * MIX_W // MEM_W
        else:
            proj = _mm(hn, wts["w_in_b"][l - n_a], name=f"in_proj_{l}", out_dtype=BF16)
            mix, stat, first = _fox_fwd(proj, shared["kv"], shared["c_col"], shared["c_row"], shared["k_max"], name=f"fox_fwd_{l}")
            q_block = MIX_W // MEM_W
        mem_out = _mem_fwd(proj, q_block, mkv, name=f"mem_fwd_{l}")
        merged = jnp.concatenate([mix, mem_out], axis=1)
        h_mid = _mm(merged, wts["w_o"][l], name=f"o_proj_{l}", res=h)
        h2n = _rmsnorm_fwd(h_mid, small["norm2_g"][l], name=f"norm2_{l}")
        u, act = _mm(h2n, wts["w_mlp1"][l], name=f"mlp1_{l}", epilogue="relu2")
        h_out = _mm(act, wts["w_mlp2"][l], name=f"mlp2_{l}", res=h_mid)
        saved.append(dict(h=h, hn=hn, memn=memn, mkv=mkv, proj=proj, stat=stat, first=first, merged=merged, h_mid=h_mid, h2n=h2n, u=u, act=act,
                          q_block=q_block))
        h = h_out

    loss, dh, dg_final = _final_loss(h, target, small["final_norm_g"], name="final_loss")

    g_w = {k: [None] * wts[k].shape[0] for k in ("w_in_a", "w_in_b", "w_mem_kv", "w_o", "w_mlp1", "w_mlp2")}
    g_n = {k: [None] * depth for k in ("norm1_g", "mem_norm_g", "norm2_g")}
    dk_sh = dv_sh = dc_sh = dcq_sh = None
    for l in reversed(range(depth)):
        sv = saved[l]
        is_a = l < n_a
        du = _mm(dh, wts["w_mlp2"][l], name=f"d_act_{l}", trans_b=True, epilogue="drelu2", u=sv["u"], out_dtype=BF16)
        g_w["w_mlp2"][l] = _mm_tn(sv["act"], dh, name=f"dw_mlp2_{l}")
        g_w["w_mlp1"][l] = _mm_tn(sv["h2n"], du, name=f"dw_mlp1_{l}")
        dh2n = _mm(du, wts["w_mlp1"][l], name=f"d_h2n_{l}", trans_b=True)
        dh_mid, g_n["norm2_g"][l] = _rmsnorm_bwd(dh2n, sv["h_mid"], small["norm2_g"][l], dh, name=f"norm2_bwd_{l}")
        dmerged = _mm(dh_mid, wts["w_o"][l], name=f"d_merged_{l}", trans_b=True, out_dtype=BF16)
        g_w["w_o"][l] = _mm_tn(sv["merged"], dh_mid, name=f"dw_o_{l}")
        dqm, dmkv = _mem_bwd(sv["proj"], sv["q_block"], sv["mkv"], dmerged, name=f"mem_bwd_{l}")
        if is_a:
            dq, dk, dv = _sb_bwd(sv["proj"], dmerged, sv["stat"], sv["first"], name=f"sb_bwd_{l}")
            dproj = jnp.concatenate([dq, dk.astype(BF16), dv.astype(BF16), dqm], axis=1)
            w_in, key, idx = wts["w_in_a"][l], "w_in_a", l
        else:
            dq, dk, dv, dc, dcq = _fox_bwd(sv["proj"], shared["kv"], shared["c_col"], shared["c_row"], sv["merged"], dmerged, sv["stat"],
                                           sv["first"], name=f"fox_bwd_{l}")
            dk_sh = dk if dk_sh is None else dk_sh + dk
            dv_sh = dv if dv_sh is None else dv_sh + dv
            dc_sh = dc if dc_sh is None else dc_sh + dc
            dcq_sh = dcq if dcq_sh is None else dcq_sh + dcq
            dproj = jnp.concatenate([dq, dqm], axis=1)
            w_in, key, idx = wts["w_in_b"][l - n_a], "w_in_b", l - n_a
        g_w[key][idx] = _mm_tn(sv["hn"], dproj, name=f"dw_in_{l}")
        dhn = _mm(dproj, w_in, name=f"d_hn_{l}", trans_b=True)
        dh, g_n["norm1_g"][l] = _rmsnorm_bwd(dhn, sv["h"], small["norm1_g"][l], dh_mid, name=f"norm1_bwd_{l}")
        g_w["w_mem_kv"][l] = _mm_tn(sv["memn"], dmkv, name=f"dw_mem_kv_{l}")
        dmemn = _mm(dmkv, wts["w_mem_kv"][l], name=f"d_memn_{l}", trans_b=True)
        _, g_n["mem_norm_g"][l] = _rmsnorm_bwd(dmemn, mem, small["mem_norm_g"][l], None, name=f"mem_norm_bwd_{l}")
        if l == n_a:
            s_len = x.shape[0]
            dc_query = dcq_sh[:, :, ::HEAD_DIM].transpose(1, 0, 2).reshape(s_len, N_MIX_HEADS)
            dc_tok = _pad_lanes(dc_sh[:, :2, :].reshape(N_MIX_HEADS, s_len).T + dc_query)
            df, db = _gate_bwd(shared["f"], b_f, dc_tok, name="gate_bwd")
            dkv = jnp.concatenate([dk_sh, dv_sh], axis=1)
            dw_kv_kv = _mm_tn(shared["hs"], dkv, name="dw_kv")
            dw_kv_f = _mm_tn(shared["hs"], df, name="dw_gate")
            dhs = _mm(dkv, w_kv_kv, name="d_hs_kv", trans_b=True)
            dhs = _mm(df, w_kv_f, name="d_hs_gate", trans_b=True, res=dhs)
            dh, dg_kv = _rmsnorm_bwd(dhs, shared["h"], small["kv_norm_g"], dh, name="kv_norm_bwd")
            g_kv = jnp.concatenate([dw_kv_kv, dw_kv_f[:, :N_KV_F]], axis=1)

    grads = {k: jnp.stack(v) for k, v in g_w.items()}
    grads["w_kv_shared"] = g_kv
    d = x.shape[1]
    small_g = dict(
        norm1_g=jnp.concatenate(g_n["norm1_g"], axis=0), mem_norm_g=jnp.concatenate(g_n["mem_norm_g"], axis=0),
        norm2_g=jnp.concatenate(g_n["norm2_g"], axis=0), kv_norm_g=dg_kv.reshape(d), b_f=db[0, :N_KV_F], final_norm_g=dg_final.reshape(d))
    return loss, dh, grads, small_g


def _pack_small(vals, d):
    rows = []
    for name in REPLICATED:
        a = vals[name].astype(F32)
        if name == "b_f":
            a = jnp.pad(a, (0, d - a.shape[0]))
        rows.append(a.reshape(-1, d))
    packed = jnp.concatenate(rows, axis=0)
    pad = -packed.shape[0] % 8
    return jnp.pad(packed, ((0, pad), (0, 0)))


def _unpack_small(packed, shapes):
    out, off = {}, 0
    for name in REPLICATED:
        shape = shapes[name]
        if name == "b_f":
            out[name] = packed[off, :shape[0]]
            off += 1
        else:
            n = math.prod(shape) // packed.shape[1]
            out[name] = packed[off:off + n].reshape(shape)
            off += n
    return out


def kernel(x, mem, norm1_g, w_in_a, w_in_b, w_mem_kv, mem_norm_g, w_o, norm2_g, w_mlp1, w_mlp2, kv_norm_g, w_kv_shared, b_f, final_norm_g, loss_target, m_norm1_g, m_w_in_a, m_w_in_b, m_w_mem_kv, m_mem_norm_g, m_w_o, m_norm2_g, m_w_mlp1, m_w_mlp2, m_kv_norm_g, m_w_kv_shared, m_b_f, m_final_norm_g, v_norm1_g, v_w_in_a, v_w_in_b, v_w_mem_kv, v_mem_norm_g, v_w_o, v_norm2_g, v_w_mlp1, v_w_mlp2, v_kv_norm_g, v_w_kv_shared, v_b_f, v_final_norm_g):
    w = dict(norm1_g=norm1_g, w_in_a=w_in_a, w_in_b=w_in_b, w_mem_kv=w_mem_kv, mem_norm_g=mem_norm_g, w_o=w_o, norm2_g=norm2_g,
             w_mlp1=w_mlp1, w_mlp2=w_mlp2, kv_norm_g=kv_norm_g, w_kv_shared=w_kv_shared, b_f=b_f, final_norm_g=final_norm_g)
    m = dict(norm1_g=m_norm1_g, w_in_a=m_w_in_a, w_in_b=m_w_in_b, w_mem_kv=m_w_mem_kv, mem_norm_g=m_mem_norm_g, w_o=m_w_o,
             norm2_g=m_norm2_g, w_mlp1=m_w_mlp1, w_mlp2=m_w_mlp2, kv_norm_g=m_kv_norm_g, w_kv_shared=m_w_kv_shared, b_f=m_b_f,
             final_norm_g=m_final_norm_g)
    v = dict(norm1_g=v_norm1_g, w_in_a=v_w_in_a, w_in_b=v_w_in_b, w_mem_kv=v_w_mem_kv, mem_norm_g=v_mem_norm_g, w_o=v_w_o,
             norm2_g=v_norm2_g, w_mlp1=v_w_mlp1, w_mlp2=v_w_mlp2, kv_norm_g=v_kv_norm_g, w_kv_shared=v_w_kv_shared, b_f=v_b_f,
             final_norm_g=v_final_norm_g)
    d = x.shape[-1]
    layout, total = _layout({name: w[name].shape for name, _ in SHARDED})

    gathered = _all_gather(_pack_shards(layout, total, w, BF16), name="gather_weights").reshape(N_DEV, total, LANES)
    wts = {name: _unpack_full(layout, gathered, name, axis) for name, axis in SHARDED}
    small = {name: w[name] for name in REPLICATED}

    loss, grad_x, grads, small_g = _forward_backward(x[0], mem[0], loss_target[0], wts, small)

    parts = _exchange_partials(_pack_full(layout, total, grads).astype(BF16), name="exchange_grads")
    g_flat, d_flat, m_flat, v_flat = _adamw(parts, _pack_shards(layout, total, w, F32), _pack_shards(layout, total, m, F32),
                                            _pack_shards(layout, total, v, F32), name="adamw_sharded")

    small_packed = _pack_small(small_g, d)
    n_small = small_packed.shape[0]
    small_parts = _all_gather(small_packed, name="gather_small_grads").reshape(N_DEV, n_small, d)
    gs, ds_, ms, vs = _adamw(small_parts, _pack_small(w, d), _pack_small(m, d), _pack_small(v, d), name="adamw_replicated")

    shapes = {name: w[name].shape for name in REPLICATED}
    out_g, out_d, out_m, out_v = {}, {}, {}, {}
    for flat, small_flat, out in ((g_flat, gs, out_g), (d_flat, ds_, out_d), (m_flat, ms, out_m), (v_flat, vs, out_v)):
        for name, _ in SHARDED:
            out[name] = _unpack_shard(layout, flat, name)
        out.update(_unpack_small(small_flat, shapes))

    loss_total = lax.psum(loss[0, 0], ("x", "y", "c"))
    return (loss_total, grad_x[None], *[out_g[n] for n in WEIGHT_ORDER], *[out_d[n] for n in WEIGHT_ORDER],
            *[out_m[n] for n in WEIGHT_ORDER], *[out_v[n] for n in WEIGHT_ORDER])
```

```python
import functools
import math

import jax
import jax.numpy as jnp
from jax import lax
from jax.experimental import pallas as pl
from jax.experimental.pallas import tpu as pltpu

F32 = jnp.float32
BF16 = jnp.bfloat16

N_DEV = 8
HEAD_DIM = 64
N_MIX_HEADS = 8
N_MEM_HEADS = 4
MIX_W = N_MIX_HEADS * HEAD_DIM
MEM_W = N_MEM_HEADS * HEAD_DIM
N_PAIRS = N_MIX_HEADS // 2
LANES = 128
SB_B = 256
FOX_B = 512
CUM_SUB = 256
SB_CUT = 64.0
FOX_CUT = 45.0
EPS = 1e-6
NEG_INF = -1e30
QK_SCALE = 1.0 / math.sqrt(HEAD_DIM)
LOG2E = 1.4426950408889634
N_KV_F = 8

ADAM_LR = 0.001
ADAM_B1 = 0.9
ADAM_B2 = 0.999
ADAM_EPS = 1e-08
ADAM_WD = 0.01
ADAM_STEP = 10

VMEM_LIMIT = 56 * 1024 * 1024
PACK_ROW_ALIGN = 16
PACK_BLOCK_ROWS = 1024

MESH = pl.DeviceIdType.MESH

SHARDED = (("w_in_a", 2), ("w_in_b", 1), ("w_mem_kv", 1), ("w_o", 2), ("w_mlp1", 2), ("w_mlp2", 1), ("w_kv_shared", 1))
REPLICATED = ("norm1_g", "mem_norm_g", "norm2_g", "kv_norm_g", "b_f", "final_norm_g")
WEIGHT_ORDER = ("norm1_g", "w_in_a", "w_in_b", "w_mem_kv", "mem_norm_g", "w_o", "norm2_g", "w_mlp1", "w_mlp2",
                "kv_norm_g", "w_kv_shared", "b_f", "final_norm_g")


def _cparams(sem=None):
    return pltpu.CompilerParams(dimension_semantics=sem, vmem_limit_bytes=VMEM_LIMIT)


def _resident(shape, index_map):
    return pl.BlockSpec(shape, index_map, pipeline_mode=pl.Buffered(1))


def _pick(n, cands):
    for c in cands:
        if c <= n and n % c == 0:
            return c
    return n


def _dot(a, b, dims):
    return lax.dot_general(a, b, (dims, ((), ())), preferred_element_type=F32)


NN = ((1,), (0,))
NT = ((1,), (1,))
TN = ((0,), (0,))


MM_CHUNK = 512


def _mm(a, b, *, name, trans_b=False, out_dtype=F32, res=None, epilogue=None, u=None):
    m, k = a.shape
    n = b.shape[0] if trans_b else b.shape[1]
    tm = _pick(m, (512, 256, 128))
    tn = _pick(n, (MM_CHUNK, 384, 256, 128))

    def body(*refs):
        a_ref, b_ref = refs[0], refs[1]
        pos = 2
        res_ref = u_ref = None
        if res is not None:
            res_ref = refs[pos]
            pos += 1
        if u is not None:
            u_ref = refs[pos]
            pos += 1
        outs = refs[pos:]
        av = a_ref[...].astype(BF16)
        for c in range(n // tn):
            cols = slice(c * tn, (c + 1) * tn)
            if trans_b:
                acc = _dot(av, b_ref[cols, :].astype(BF16), NT)
            else:
                acc = _dot(av, b_ref[:, cols].astype(BF16), NN)
            if res_ref is not None:
                acc = res_ref[:, cols] + acc
            if epilogue == "relu2":
                outs[0][:, cols] = acc.astype(BF16)
                r = jnp.maximum(acc, 0.0)
                outs[1][:, cols] = (r * r).astype(BF16)
            elif epilogue == "drelu2":
                outs[0][:, cols] = (acc * (2.0 * jnp.maximum(u_ref[:, cols], 0.0))).astype(out_dtype)
            else:
                outs[0][:, cols] = acc.astype(out_dtype)

    row = pl.BlockSpec((tm, n), lambda i: (i, 0))
    in_specs = [pl.BlockSpec((tm, k), lambda i: (i, 0)), _resident(b.shape, lambda i: (0, 0))]
    args = [a, b]
    if res is not None:
        in_specs.append(row)
        args.append(res)
    if u is not None:
        in_specs.append(row)
        args.append(u)
    if epilogue == "relu2":
        out_shape = (jax.ShapeDtypeStruct((m, n), BF16), jax.ShapeDtypeStruct((m, n), BF16))
        out_specs = (row, row)
    else:
        out_shape = (jax.ShapeDtypeStruct((m, n), out_dtype),)
        out_specs = (row,)
    outs = pl.pallas_call(
        body, name=name, grid=(m // tm,), in_specs=in_specs, out_specs=out_specs, out_shape=out_shape,
        compiler_params=_cparams(("parallel",)),
    )(*args)
    return outs if epilogue == "relu2" else outs[0]


def _mm_tn(x, dy, *, name):
    m, k1 = x.shape
    n = dy.shape[1]
    t1 = _pick(k1, (1024, 896, 768, 512, 256, 128))
    tn = _pick(n, (1024, 896, 768, 512, 256, 128))
    tm = _pick(m, (2048, 1024, 512, 256, 128))
    nm = m // tm

    def body(x_ref, dy_ref, o_ref):
        mm = pl.program_id(2)

        @pl.when(mm == 0)
        def _():
            o_ref[...] = jnp.zeros_like(o_ref)

        o_ref[...] += _dot(x_ref[...].astype(BF16), dy_ref[...].astype(BF16), TN)

    return pl.pallas_call(
        body, name=name, grid=(k1 // t1, n // tn, nm),
        in_specs=[pl.BlockSpec((tm, t1), lambda i, j, mm: (mm, i)), pl.BlockSpec((tm, tn), lambda i, j, mm: (mm, j))],
        out_specs=pl.BlockSpec((t1, tn), lambda i, j, mm: (i, j)),
        out_shape=jax.ShapeDtypeStruct((k1, n), F32),
        compiler_params=_cparams(("parallel", "parallel", "arbitrary")),
    )(x, dy)


def _rmsnorm_fwd(x, g, *, name):
    s, d = x.shape
    tm = _pick(s, (512, 256, 128))

    def body(x_ref, g_ref, o_ref):
        xf = x_ref[...]
        r = lax.rsqrt(jnp.mean(xf * xf, axis=-1, keepdims=True) + EPS)
        o_ref[...] = (xf * r * g_ref[...]).astype(BF16)

    return pl.pallas_call(
        body, name=name, grid=(s // tm,),
        in_specs=[pl.BlockSpec((tm, d), lambda i: (i, 0)), pl.BlockSpec((1, d), lambda i: (0, 0))],
        out_specs=pl.BlockSpec((tm, d), lambda i: (i, 0)),
        out_shape=jax.ShapeDtypeStruct((s, d), BF16),
        compiler_params=_cparams(("parallel",)),
    )(x, g.reshape(1, d))


def _rmsnorm_bwd(dy, x, g, dres, *, name):
    s, d = x.shape
    tm = _pick(s, (256, 128))

    def body(*refs):
        if dres is None:
            dy_ref, x_ref, g_ref, dx_ref, dg_ref = refs
            dres_ref = None
        else:
            dy_ref, x_ref, g_ref, dres_ref, dx_ref, dg_ref = refs
        i = pl.program_id(0)
        xf = x_ref[...]
        dyv = dy_ref[...]
        r = lax.rsqrt(jnp.mean(xf * xf, axis=-1, keepdims=True) + EPS)
        xh = xf * r
        dyg = dyv * g_ref[...]
        dx = r * (dyg - xh * jnp.mean(dyg * xh, axis=-1, keepdims=True))
        dx_ref[...] = dx if dres_ref is None else dres_ref[...] + dx

        @pl.when(i == 0)
        def _():
            dg_ref[...] = jnp.zeros_like(dg_ref)

        dg_ref[...] += jnp.sum(dyv * xh, axis=0, keepdims=True)

    row = pl.BlockSpec((tm, d), lambda i: (i, 0))
    vec = pl.BlockSpec((1, d), lambda i: (0, 0))
    in_specs = [row, row, vec] + ([] if dres is None else [row])
    args = [dy, x, g.reshape(1, d)] + ([] if dres is None else [dres])
    return pl.pallas_call(
        body, name=name, grid=(s // tm,), in_specs=in_specs, out_specs=(row, vec),
        out_shape=(jax.ShapeDtypeStruct((s, d), F32), jax.ShapeDtypeStruct((1, d), F32)),
        compiler_params=_cparams(("arbitrary",)),
    )(*args)


def _final_loss(h, target, g, *, name):
    s, d = h.shape
    tm = _pick(s, (256, 128))

    def body(h_ref, t_ref, g_ref, loss_ref, dh_ref, dg_ref):
        i = pl.program_id(0)
        xf = h_ref[...]
        gv = g_ref[...]
        r = lax.rsqrt(jnp.mean(xf * xf, axis=-1, keepdims=True) + EPS)
        xh = xf * r
        err = xh * gv - t_ref[...]
        part = 0.5 * jnp.sum(jnp.mean(err * err, axis=-1, keepdims=True), axis=0, keepdims=True)
        dyv = err * (1.0 / d)
        dyg = dyv * gv
        dh_ref[...] = r * (dyg - xh * jnp.mean(dyg * xh, axis=-1, keepdims=True))

        @pl.when(i == 0)
        def _():
            dg_ref[...] = jnp.zeros_like(dg_ref)
            loss_ref[...] = jnp.zeros_like(loss_ref)

        dg_ref[...] += jnp.sum(dyv * xh, axis=0, keepdims=True)
        loss_ref[...] += jnp.broadcast_to(part, loss_ref.shape)

    row = pl.BlockSpec((tm, d), lambda i: (i, 0))
    vec = pl.BlockSpec((1, d), lambda i: (0, 0))
    return pl.pallas_call(
        body, name=name, grid=(s // tm,), in_specs=[row, row, vec],
        out_specs=(pl.BlockSpec((1, LANES), lambda i: (0, 0)), row, vec),
        out_shape=(jax.ShapeDtypeStruct((1, LANES), F32), jax.ShapeDtypeStruct((s, d), F32), jax.ShapeDtypeStruct((1, d), F32)),
        compiler_params=_cparams(("arbitrary",)),
    )(h, target, g.reshape(1, d))


def _head_lanes(rows):
    lane = lax.broadcasted_iota(jnp.int32, (rows, LANES), 1)
    return [lane < HEAD_DIM, lane >= HEAD_DIM]


def _tri(b, cmp):
    row = lax.broadcasted_iota(jnp.int32, (b, b), 0)
    col = lax.broadcasted_iota(jnp.int32, (b, b), 1)
    return cmp(row, col)


def _twice(mask):
    return jnp.concatenate([mask, mask], axis=0)


def _stack_heads(x, heads):
    zero = jnp.zeros_like(x)
    return jnp.concatenate([jnp.where(heads[0], x, zero), jnp.where(heads[1], x, zero)], axis=0)


def _unstack_heads(x2, heads):
    b = x2.shape[0] // 2
    return jnp.where(heads[0], x2[:b], x2[b:])


def _stack_stat(stat):
    return jnp.concatenate([stat[:, 0:1], stat[:, HEAD_DIM:HEAD_DIM + 1]], axis=0)


def _unstack_stat(col, heads):
    b = col.shape[0] // 2
    return jnp.where(heads[0], jnp.broadcast_to(col[:b], (b, LANES)), jnp.broadcast_to(col[b:], (b, LANES)))


def _tri_dot(x, tri_bf16):
    return _dot(x.astype(BF16), tri_bf16, NN)


def _prefix_sums(x, tri_bf16, inclusive):
    sub = tri_bf16.shape[0]
    outs, carry = [], None
    for c in range(x.shape[1] // sub):
        xs = x[:, c * sub:(c + 1) * sub]
        loc = _tri_dot(xs, tri_bf16)
        outs.append(loc if carry is None else loc + carry)
        tot = loc[:, sub - 1:sub] if inclusive else loc[:, sub - 1:sub] + xs[:, sub - 1:sub]
        carry = tot if carry is None else carry + tot
    return (outs[0] if len(outs) == 1 else jnp.concatenate(outs, axis=1)), carry


def _suffix_sums(x, tri_bf16):
    sub = tri_bf16.shape[0]
    n = x.shape[1] // sub
    outs, carry = [None] * n, None
    for c in reversed(range(n)):
        xs = x[:, c * sub:(c + 1) * sub]
        loc = _tri_dot(xs, tri_bf16)
        outs[c] = loc if carry is None else loc + carry
        tot = loc[:, 0:1] + xs[:, 0:1].astype(BF16).astype(F32)
        carry = tot if carry is None else carry + tot
    return (outs[0] if n == 1 else jnp.concatenate(outs, axis=1)), carry


def _softplus2(z):
    z2 = z * LOG2E
    neg_abs = lax.bitcast_convert_type(lax.bitcast_convert_type(z2, jnp.uint32) | jnp.uint32(0x80000000), F32)
    return z2, jnp.maximum(z2, 0.0) + jnp.log2(1.0 + jnp.exp2(neg_abs))


def _block_rows(j, b):
    return pl.ds(pl.multiple_of(j * b, b), b)


def _sb_fwd(proj, *, name):
    s = proj.shape[0]
    b = _pick(s, (SB_B, 128))

    def body(q_ref, k_ref, v_ref, o_ref, tot_ref, first_ref, acc_ref):
        i = pl.program_id(1)
        heads = _head_lanes(b)
        suffix = _tri(min(b, CUM_SUB), lambda r, c: r > c).astype(BF16)
        strict = _twice(_tri(b, lambda r, c: c < r))
        q2 = _stack_heads(q_ref[...] * QK_SCALE, heads)

        def tile(j, a, masked):
            rows = _block_rows(j, b)
            z2, sp = _softplus2(_dot(q2, k_ref[rows, :], NT))
            if masked:
                sp = jnp.where(strict, sp, 0.0)
            rsum, total = _suffix_sums(sp, suffix)
            w = jnp.exp2((z2 - sp) - (a + rsum))
            if masked:
                w = jnp.where(strict, w, 0.0)
            acc_ref[...] += _dot(w.astype(BF16), v_ref[rows, :], NN)
            return a + total

        acc_ref[...] = jnp.zeros_like(acc_ref)
        a = tile(i, jnp.zeros((2 * b, 1), F32), True)

        def more(c):
            return jnp.logical_and(c[0] < i, c[2] < SB_CUT)

        def step(c):
            a = tile(i - 1 - c[0], c[1], False)
            return c[0] + 1, a, jnp.min(a)

        done, a, _ = lax.while_loop(more, step, (jnp.int32(0), a, jnp.min(a)))
        o_ref[...] = _unstack_heads(acc_ref[...], heads).astype(BF16)
        tot_ref[0] = _unstack_stat(a, heads)
        first_ref[pl.program_id(0), i] = i - done

    qblk = pl.BlockSpec((b, LANES), lambda p, i: (i, p))
    return pl.pallas_call(
        body, name=name, grid=(N_PAIRS, s // b),
        in_specs=[qblk, _resident((s, LANES), lambda p, i: (0, N_PAIRS + p)), _resident((s, LANES), lambda p, i: (0, 2 * N_PAIRS + p))],
        out_specs=(qblk, pl.BlockSpec((1, b, LANES), lambda p, i: (p, i, 0)), pl.BlockSpec(memory_space=pltpu.SMEM)),
        out_shape=(jax.ShapeDtypeStruct((s, MIX_W), BF16), jax.ShapeDtypeStruct((N_PAIRS, s, LANES), F32),
                   jax.ShapeDtypeStruct((N_PAIRS, s // b), jnp.int32)),
        scratch_shapes=[pltpu.VMEM((2 * b, LANES), F32)],
        compiler_params=_cparams(("arbitrary", "arbitrary")),
    )(proj, proj, proj)


def _sb_bwd(proj, dmerged, tot, first, *, name):
    s = proj.shape[0]
    b = _pick(s, (SB_B, 128))

    def body(q_ref, k_ref, v_ref, do_ref, tot_ref, first_ref, dq_ref, dk_ref, dv_ref, dq_acc):
        i = pl.program_id(1)

        @pl.when(i == 0)
        def _():
            dk_ref[...] = jnp.zeros_like(dk_ref)
            dv_ref[...] = jnp.zeros_like(dv_ref)

        heads = _head_lanes(b)
        incl = _tri(min(b, CUM_SUB), lambda r, c: r <= c).astype(BF16)
        excl = _tri(min(b, CUM_SUB), lambda r, c: r < c).astype(BF16)
        strict = _twice(_tri(b, lambda r, c: c < r))
        q2 = _stack_heads(q_ref[...] * QK_SCALE, heads)
        do2 = _stack_heads(do_ref[...], heads)
        tot2 = _stack_stat(tot_ref[0])

        def tile(j, pre, gpre, masked):
            rows = _block_rows(j, b)
            kb = k_ref[rows, :]
            z2, sp = _softplus2(_dot(q2, kb, NT))
            oms = jnp.exp2(-sp)
            if masked:
                sp = jnp.where(strict, sp, 0.0)
            pin, ptot = _prefix_sums(sp, incl, True)
            w = jnp.exp2((z2 - sp) + (pin + (pre - tot2)))
            if masked:
                w = jnp.where(strict, w, 0.0)
            gw = _dot(do2, v_ref[rows, :], NT) * w
            gex, gtot = _prefix_sums(gw, excl, False)
            dz = gw * oms - (1.0 - oms) * (gpre + gex)
            if masked:
                dz = jnp.where(strict, dz, 0.0)
            dzb = dz.astype(BF16)
            dq_acc[...] += _dot(dzb, kb, NN)
            dk_ref[rows, :] += _dot(dzb, q2, TN)
            dv_ref[rows, :] += _dot(w.astype(BF16), do2, TN)
            return pre + ptot, gpre + gtot

        dq_acc[...] = jnp.zeros_like(dq_acc)
        zero = jnp.zeros((2 * b, 1), F32)
        pre, gpre = lax.fori_loop(first_ref[pl.program_id(0), i], i, lambda j, c: tile(j, c[0], c[1], False), (zero, zero))
        tile(i, pre, gpre, True)
        dq_ref[...] = (_unstack_heads(dq_acc[...], heads) * QK_SCALE).astype(BF16)

    qblk = pl.BlockSpec((b, LANES), lambda p, i: (i, p))
    full = _resident((s, LANES), lambda p, i: (0, p))
    return pl.pallas_call(
        body, name=name, grid=(N_PAIRS, s // b),
        in_specs=[qblk, _resident((s, LANES), lambda p, i: (0, N_PAIRS + p)), _resident((s, LANES), lambda p, i: (0, 2 * N_PAIRS + p)),
                  qblk, pl.BlockSpec((1, b, LANES), lambda p, i: (p, i, 0)), pl.BlockSpec(memory_space=pltpu.SMEM)],
        out_specs=(qblk, full, full),
        out_shape=(jax.ShapeDtypeStruct((s, MIX_W), BF16), jax.ShapeDtypeStruct((s, MIX_W), F32), jax.ShapeDtypeStruct((s, MIX_W), F32)),
        scratch_shapes=[pltpu.VMEM((2 * b, LANES), F32)],
        compiler_params=_cparams(("parallel", "arbitrary")),
    )(proj, proj, proj, dmerged, tot, first)


def _fox_fwd(proj, kv, c_col, c_row, k_max, *, name):
    s = proj.shape[0]
    b = _pick(s, (FOX_B, 128))

    def body(q_ref, k_ref, v_ref, cc_ref, cr_ref, km_ref, o_ref, lse_ref, first_ref, acc_ref):
        i = pl.program_id(1)
        heads = _head_lanes(b)
        causal = _twice(_tri(b, lambda r, c: c <= r))
        top = lax.broadcasted_iota(jnp.int32, (2 * b, b), 0) < b
        q2 = _stack_heads(q_ref[...] * QK_SCALE, heads)
        c_t = _stack_stat(cc_ref[0])
        qf = q2.astype(F32)
        kmv = km_ref[0]
        z_max = jnp.sqrt(jnp.sum(qf * qf, axis=1, keepdims=True)) * jnp.where(top[:, 0:1], kmv[0:1, 0:1], kmv[1:2, 0:1]) * 1.001

        def tile(j, m, l, masked):
            rows = _block_rows(j, b)
            gate = c_t - jnp.where(top, cr_ref[0, 0:1, rows], cr_ref[0, 1:2, rows])
            sc = _dot(q2, k_ref[rows, :], NT) + gate
            if masked:
                sc = jnp.where(causal, sc, NEG_INF)
            m_new = jnp.maximum(m, jnp.max(sc, axis=1, keepdims=True))
            p = jnp.exp(sc - m_new)
            alpha = jnp.exp(m - m_new)
            acc_ref[...] = alpha * acc_ref[...] + _dot(p.astype(BF16), v_ref[rows, :], NN)
            return m_new, alpha * l + jnp.sum(p, axis=1, keepdims=True), jnp.max(z_max + gate[:, 0:1] - m_new)

        acc_ref[...] = jnp.zeros_like(acc_ref)
        m, l, slack = tile(i, jnp.full((2 * b, 1), NEG_INF, F32), jnp.zeros((2 * b, 1), F32), True)

        def more(c):
            return jnp.logical_and(c[0] < i, c[3] > -FOX_CUT)

        def step(c):
            m, l, slack = tile(i - 1 - c[0], c[1], c[2], False)
            return c[0] + 1, m, l, slack

        done, m, l, _ = lax.while_loop(more, step, (jnp.int32(0), m, l, slack))
        o_ref[...] = _unstack_heads(acc_ref[...] * (1.0 / l), heads).astype(BF16)
        lse_ref[0] = _unstack_stat(m + jnp.log(l), heads)
        first_ref[pl.program_id(0), i] = i - done

    qblk = pl.BlockSpec((b, LANES), lambda p, i: (i, p))
    stat = pl.BlockSpec((1, b, LANES), lambda p, i: (p, i, 0))
    return pl.pallas_call(
        body, name=name, grid=(N_PAIRS, s // b),
        in_specs=[qblk, _resident((s, LANES), lambda p, i: (0, p)), _resident((s, LANES), lambda p, i: (0, N_PAIRS + p)),
                  stat, _resident((1, 8, s), lambda p, i: (p, 0, 0)), pl.BlockSpec((1, 8, LANES), lambda p, i: (p, 0, 0))],
        out_specs=(qblk, stat, pl.BlockSpec(memory_space=pltpu.SMEM)),
        out_shape=(jax.ShapeDtypeStruct((s, MIX_W), BF16), jax.ShapeDtypeStruct((N_PAIRS, s, LANES), F32),
                   jax.ShapeDtypeStruct((N_PAIRS, s // b), jnp.int32)),
        scratch_shapes=[pltpu.VMEM((2 * b, LANES), F32)],
        compiler_params=_cparams(("arbitrary", "arbitrary")),
    )(proj, kv, kv, c_col, c_row, k_max)


def _fox_bwd(proj, kv, c_col, c_row, merged, dmerged, lse, first, *, name):
    s = proj.shape[0]
    b = _pick(s, (FOX_B, 128))

    def body(q_ref, k_ref, v_ref, cc_ref, cr_ref, o_ref, do_ref, lse_ref, first_ref, dq_ref, dk_ref, dv_ref, dc_ref, dcq_ref, dq_acc):
        i = pl.program_id(1)

        @pl.when(i == 0)
        def _():
            dk_ref[...] = jnp.zeros_like(dk_ref)
            dv_ref[...] = jnp.zeros_like(dv_ref)
            dc_ref[...] = jnp.zeros_like(dc_ref)

        heads = _head_lanes(b)
        causal = _twice(_tri(b, lambda r, c: c <= r))
        top = lax.broadcasted_iota(jnp.int32, (2 * b, b), 0) < b
        q2 = _stack_heads(q_ref[...] * QK_SCALE, heads)
        dov = do_ref[...]
        do2 = _stack_heads(dov, heads)
        prod = dov.astype(F32) * o_ref[...].astype(F32)
        delta = jnp.concatenate([jnp.sum(jnp.where(heads[hh], prod, 0.0), axis=1, keepdims=True) for hh in range(2)], axis=0)
        c_t = _stack_stat(cc_ref[0])
        lse_t = _stack_stat(lse_ref[0])

        def tile(j, rsum, masked):
            rows = _block_rows(j, b)
            kb = k_ref[rows, :]
            c_s = jnp.where(top, cr_ref[0, 0:1, rows], cr_ref[0, 1:2, rows])
            sc = _dot(q2, kb, NT) + (c_t - c_s)
            p = jnp.exp(sc - lse_t)
            if masked:
                p = jnp.where(causal, p, 0.0)
            ds = p * (_dot(do2, v_ref[rows, :], NT) - delta)
            dsb = ds.astype(BF16)
            dq_acc[...] += _dot(dsb, kb, NN)
            dk_ref[rows, :] += _dot(dsb, q2, TN)
            dv_ref[rows, :] += _dot(p.astype(BF16), do2, TN)
            dc_ref[0, 0:1, rows] -= jnp.sum(ds[:b], axis=0, keepdims=True)
            dc_ref[0, 1:2, rows] -= jnp.sum(ds[b:], axis=0, keepdims=True)
            return rsum + jnp.sum(ds, axis=1, keepdims=True)

        dq_acc[...] = jnp.zeros_like(dq_acc)
        rsum = lax.fori_loop(first_ref[pl.program_id(0), i], i, lambda j, r: tile(j, r, False), jnp.zeros((2 * b, 1), F32))
        rsum = tile(i, rsum, True)
        dq_ref[...] = (_unstack_heads(dq_acc[...], heads) * QK_SCALE).astype(BF16)
        dcq_ref[0] = _unstack_stat(rsum, heads)

    qblk = pl.BlockSpec((b, LANES), lambda p, i: (i, p))
    stat = pl.BlockSpec((1, b, LANES), lambda p, i: (p, i, 0))
    crow = _resident((1, 8, s), lambda p, i: (p, 0, 0))
    full = _resident((s, LANES), lambda p, i: (0, p))
    return pl.pallas_call(
        body, name=name, grid=(N_PAIRS, s // b),
        in_specs=[qblk, full, _resident((s, LANES), lambda p, i: (0, N_PAIRS + p)), stat, crow, qblk, qblk, stat,
                  pl.BlockSpec(memory_space=pltpu.SMEM)],
        out_specs=(qblk, full, full, crow, stat),
        out_shape=(jax.ShapeDtypeStruct((s, MIX_W), BF16), jax.ShapeDtypeStruct((s, MIX_W), F32), jax.ShapeDtypeStruct((s, MIX_W), F32),
                   jax.ShapeDtypeStruct((N_PAIRS, 8, s), F32), jax.ShapeDtypeStruct((N_PAIRS, s, LANES), F32)),
        scratch_shapes=[pltpu.VMEM((2 * b, LANES), F32)],
        compiler_params=_cparams(("parallel", "arbitrary")),
    )(proj, kv, kv, c_col, c_row, merged, dmerged, lse, first)


MEM_TQ = 256


def _mem_fwd(proj, q_col_block, mkv, *, name):
    s = proj.shape[0]
    tq = _pick(s, (MEM_TQ, 128))
    n_mem = mkv.shape[0]

    def body(q_ref, mkv_ref, o_ref):
        heads = _head_lanes(tq)
        for pp in range(MEM_W // LANES):
            cols = slice(pp * LANES, (pp + 1) * LANES)
            qv = q_ref[:, cols] * QK_SCALE
            mk = mkv_ref[:, pp * LANES:(pp + 1) * LANES]
            mv = mkv_ref[:, MEM_W + pp * LANES:MEM_W + (pp + 1) * LANES]
            o_sel = None
            for hh in range(2):
                qm = jnp.where(heads[hh], qv, jnp.zeros_like(qv))
                sc = _dot(qm, mk, NT)
                p = jnp.exp(sc - jnp.max(sc, axis=1, keepdims=True))
                p = p / jnp.sum(p, axis=1, keepdims=True)
                out = _dot(p.astype(BF16), mv, NN)
                o_sel = out if hh == 0 else jnp.where(heads[0], o_sel, out)
            o_ref[:, cols] = o_sel.astype(BF16)

    return pl.pallas_call(
        body, name=name, grid=(s // tq,),
        in_specs=[pl.BlockSpec((tq, MEM_W), lambda i: (i, q_col_block)), pl.BlockSpec((n_mem, 2 * MEM_W), lambda i: (0, 0))],
        out_specs=pl.BlockSpec((tq, MEM_W), lambda i: (i, 0)),
        out_shape=jax.ShapeDtypeStruct((s, MEM_W), BF16),
        compiler_params=_cparams(("parallel",)),
    )(proj, mkv)


def _mem_bwd(proj, q_col_block, mkv, dmerged, *, name):
    s = proj.shape[0]
    tq = _pick(s, (MEM_TQ, 128))
    n_mem = mkv.shape[0]

    def body(q_ref, mkv_ref, do_ref, dq_ref, dmkv_ref):
        i = pl.program_id(0)

        @pl.when(i == 0)
        def _():
            dmkv_ref[...] = jnp.zeros_like(dmkv_ref)

        heads = _head_lanes(tq)
        for pp in range(MEM_W // LANES):
            cols = slice(pp * LANES, (pp + 1) * LANES)
            vcols = slice(MEM_W + pp * LANES, MEM_W + (pp + 1) * LANES)
            qv = q_ref[:, cols] * QK_SCALE
            dov = do_ref[:, cols]
            mk = mkv_ref[:, cols]
            mv = mkv_ref[:, vcols]
            dq_sel = None
            for hh in range(2):
                qm = jnp.where(heads[hh], qv, jnp.zeros_like(qv))
                dom = jnp.where(heads[hh], dov, jnp.zeros_like(dov))
                sc = _dot(qm, mk, NT)
                p = jnp.exp(sc - jnp.max(sc, axis=1, keepdims=True))
                p = p / jnp.sum(p, axis=1, keepdims=True)
                dp = _dot(dom, mv, NT)
                ds = p * (dp - jnp.sum(p * dp, axis=1, keepdims=True))
                dsb = ds.astype(BF16)
                dq = _dot(dsb, mk, NN)
                dmkv_ref[:, cols] += _dot(dsb, qm, TN)
                dmkv_ref[:, vcols] += _dot(p.astype(BF16), dom, TN)
                dq_sel = dq if hh == 0 else jnp.where(heads[0], dq_sel, dq)
            dq_ref[:, cols] = (dq_sel * QK_SCALE).astype(BF16)

    return pl.pallas_call(
        body, name=name, grid=(s // tq,),
        in_specs=[pl.BlockSpec((tq, MEM_W), lambda i: (i, q_col_block)), pl.BlockSpec((n_mem, 2 * MEM_W), lambda i: (0, 0)),
                  pl.BlockSpec((tq, MEM_W), lambda i: (i, MIX_W // MEM_W))],
        out_specs=(pl.BlockSpec((tq, MEM_W), lambda i: (i, 0)), pl.BlockSpec((n_mem, 2 * MEM_W), lambda i: (0, 0))),
        out_shape=(jax.ShapeDtypeStruct((s, MEM_W), BF16), jax.ShapeDtypeStruct((n_mem, 2 * MEM_W), F32)),
        compiler_params=_cparams(("arbitrary",)),
    )(proj, mkv, dmerged)


GATE_TB = 256


def _split3_dot(tri_bf16, x):
    x1 = x.astype(BF16)
    r1 = x - x1.astype(F32)
    x2 = r1.astype(BF16)
    x3 = (r1 - x2.astype(F32)).astype(BF16)
    return _dot(tri_bf16, x1, NN) + _dot(tri_bf16, x2, NN) + _dot(tri_bf16, x3, NN)


def _gate_fwd(f, b, *, name):
    s = f.shape[0]
    tb = _pick(s, (GATE_TB, 128))

    def body(f_ref, b_ref, c_ref, carry_ref):
        i = pl.program_id(0)

        @pl.when(i == 0)
        def _():
            carry_ref[...] = jnp.zeros_like(carry_ref)

        x = f_ref[...] + b_ref[...]
        lf = jnp.minimum(x, 0.0) - jnp.log1p(jnp.exp(-jnp.abs(x)))
        row = lax.broadcasted_iota(jnp.int32, (tb, tb), 0)
        col = lax.broadcasted_iota(jnp.int32, (tb, tb), 1)
        lower = (col <= row).astype(BF16)
        c = carry_ref[...] + _split3_dot(lower, lf)
        c_ref[...] = c
        carry_ref[...] = c[tb - 1:tb, :]

    return pl.pallas_call(
        body, name=name, grid=(s // tb,),
        in_specs=[pl.BlockSpec((tb, LANES), lambda i: (i, 0)), pl.BlockSpec((1, LANES), lambda i: (0, 0))],
        out_specs=pl.BlockSpec((tb, LANES), lambda i: (i, 0)),
        out_shape=jax.ShapeDtypeStruct((s, LANES), F32),
        scratch_shapes=[pltpu.VMEM((1, LANES), F32)],
        compiler_params=_cparams(("arbitrary",)),
    )(f, b)


def _gate_bwd(f, b, dc, *, name):
    s = f.shape[0]
    tb = _pick(s, (GATE_TB, 128))
    nb = s // tb

    def body(f_ref, b_ref, dc_ref, df_ref, db_ref, carry_ref):
        i = pl.program_id(0)

        @pl.when(i == 0)
        def _():
            carry_ref[...] = jnp.zeros_like(carry_ref)
            db_ref[...] = jnp.zeros_like(db_ref)

        row = lax.broadcasted_iota(jnp.int32, (tb, tb), 0)
        col = lax.broadcasted_iota(jnp.int32, (tb, tb), 1)
        upper = (col >= row).astype(BF16)
        dlf = carry_ref[...] + _split3_dot(upper, dc_ref[...])
        carry_ref[...] = dlf[0:1, :]
        x = f_ref[...] + b_ref[...]
        e = jnp.exp(-jnp.abs(x))
        one_minus_sig = jnp.where(x >= 0.0, e, 1.0) / (1.0 + e)
        df = dlf * one_minus_sig
        df_ref[...] = df
        db_ref[...] += jnp.sum(df, axis=0, keepdims=True)

    rev = pl.BlockSpec((tb, LANES), lambda i: (nb - 1 - i, 0))
    vec = pl.BlockSpec((1, LANES), lambda i: (0, 0))
    return pl.pallas_call(
        body, name=name, grid=(nb,), in_specs=[rev, vec, rev], out_specs=(rev, vec),
        out_shape=(jax.ShapeDtypeStruct((s, LANES), F32), jax.ShapeDtypeStruct((1, LANES), F32)),
        scratch_shapes=[pltpu.VMEM((1, LANES), F32)],
        compiler_params=_cparams(("arbitrary",)),
    )(f, b, dc)


def _all_gather(x, *, name):
    r, cdim = x.shape

    def body(x_ref, out_ref, send_sems, recv_sems, local_sem):
        mx, my, mc = lax.axis_index("x"), lax.axis_index("y"), lax.axis_index("c")
        me, sibling = (mx, my, mc), (mx, my, 1 - mc)
        chips = [(1 - mx, my), (mx, 1 - my), (1 - mx, 1 - my)]

        def rows(px, py, pc):
            return out_ref.at[pl.ds((4 * px + 2 * py + pc) * r, r), :]

        def copy(k, block, to, src=None):
            return pltpu.make_async_remote_copy(
                src_ref=rows(*block) if src is None else src, dst_ref=rows(*block),
                send_sem=send_sems.at[k], recv_sem=recv_sems.at[k], device_id=to, device_id_type=MESH)

        mine = pltpu.make_async_copy(x_ref, rows(*me), local_sem)
        mine.start()
        first = [copy(0, me, sibling, src=x_ref)]
        first += [copy(1 + j, me, (*chip, mc), src=x_ref) for j, chip in enumerate(chips)]
        for cp in first:
            cp.start()
        passed = [copy(4 + j, (*chip, mc), sibling) for j, chip in enumerate(chips)]
        for j, chip in enumerate(chips):
            copy(1 + j, (*chip, mc), me).wait_recv()
            passed[j].start()
        copy(0, sibling, me).wait_recv()
        for j, chip in enumerate(chips):
            copy(4 + j, (*chip, 1 - mc), me).wait_recv()
        for cp in first + passed:
            cp.wait_send()
        mine.wait()

    return pl.pallas_call(
        body, name=name,
        in_specs=[pl.BlockSpec(memory_space=pl.ANY)], out_specs=pl.BlockSpec(memory_space=pl.ANY),
        out_shape=jax.ShapeDtypeStruct((N_DEV * r, cdim), x.dtype),
        scratch_shapes=[pltpu.SemaphoreType.DMA((7,)), pltpu.SemaphoreType.DMA((7,)), pltpu.SemaphoreType.DMA],
    )(x)


def _exchange_partials(g, *, name):
    def body(g_ref, out_ref, send_sems, recv_sems, local_sem):
        mx, my, mc = lax.axis_index("x"), lax.axis_index("y"), lax.axis_index("c")
        me = 4 * mx + 2 * my + mc
        mine = pltpu.make_async_copy(g_ref.at[me], out_ref.at[me], local_sem)
        mine.start()
        copies = []
        for mask in range(1, N_DEV):
            px = 1 - mx if mask & 4 else mx
            py = 1 - my if mask & 2 else my
            pc = 1 - mc if mask & 1 else mc
            cp = pltpu.make_async_remote_copy(
                src_ref=g_ref.at[4 * px + 2 * py + pc], dst_ref=out_ref.at[me],
                send_sem=send_sems.at[mask - 1], recv_sem=recv_sems.at[mask - 1], device_id=(px, py, pc), device_id_type=MESH)
            cp.start()
            copies.append(cp)
        for cp in copies:
            cp.wait()
        mine.wait()

    return pl.pallas_call(
        body, name=name,
        in_specs=[pl.BlockSpec(memory_space=pl.ANY)], out_specs=pl.BlockSpec(memory_space=pl.ANY),
        out_shape=jax.ShapeDtypeStruct(g.shape, g.dtype),
        scratch_shapes=[pltpu.SemaphoreType.DMA((7,)), pltpu.SemaphoreType.DMA((7,)), pltpu.SemaphoreType.DMA],
    )(g)


def _adamw(parts, w, m, v, *, name):
    r, cdim = w.shape
    tr = _pick(r, (PACK_BLOCK_ROWS, 512, 256, 128, 64, 32, 16, 8))
    c1 = 1.0 / (1.0 - ADAM_B1 ** ADAM_STEP)
    c2 = 1.0 / (1.0 - ADAM_B2 ** ADAM_STEP)

    def body(p_ref, w_ref, m_ref, v_ref, g_ref, d_ref, nm_ref, nv_ref):
        g = p_ref[0].astype(F32)
        for dev in range(1, N_DEV):
            g = g + p_ref[dev].astype(F32)
        mn = ADAM_B1 * m_ref[...] + (1.0 - ADAM_B1) * g
        vn = ADAM_B2 * v_ref[...] + (1.0 - ADAM_B2) * (g * g)
        g_ref[...] = g
        nm_ref[...] = mn
        nv_ref[...] = vn
        d_ref[...] = -ADAM_LR * ((mn * c1) / (jnp.sqrt(vn * c2) + ADAM_EPS) + ADAM_WD * w_ref[...])

    blk = pl.BlockSpec((tr, cdim), lambda i: (i, 0))
    shape = jax.ShapeDtypeStruct((r, cdim), F32)
    return pl.pallas_call(
        body, name=name, grid=(r // tr,),
        in_specs=[pl.BlockSpec((N_DEV, tr, cdim), lambda i: (0, i, 0)), blk, blk, blk],
        out_specs=(blk, blk, blk, blk), out_shape=(shape, shape, shape, shape),
        compiler_params=_cparams(("parallel",)),
    )(parts, w, m, v)


def _rows_of(shape):
    n = math.prod(shape)
    assert n % LANES == 0, shape
    rows = n // LANES
    return -(-rows // PACK_ROW_ALIGN) * PACK_ROW_ALIGN


def _layout(shard_shapes):
    out, off = {}, 0
    for name, _ in SHARDED:
        rows = _rows_of(shard_shapes[name])
        out[name] = (off, rows, tuple(shard_shapes[name]))
        off += rows
    return out, -(-off // PACK_BLOCK_ROWS) * PACK_BLOCK_ROWS


def _pack_shards(layout, total, arrays, dtype):
    parts = []
    for name, _ in SHARDED:
        _, rows, _ = layout[name]
        flat = arrays[name].astype(dtype).reshape(-1, LANES)
        parts.append(jnp.pad(flat, ((0, rows - flat.shape[0]), (0, 0))))
    used = sum(p.shape[0] for p in parts)
    if total > used:
        parts.append(jnp.zeros((total - used, LANES), dtype))
    return jnp.concatenate(parts, axis=0)


def _unpack_shard(layout, flat, name):
    off, _, shape = layout[name]
    n = math.prod(shape) // LANES
    return flat[off:off + n].reshape(shape)


def _unpack_full(layout, gathered, name, axis):
    off, _, shape = layout[name]
    n = math.prod(shape) // LANES
    blocks = gathered[:, off:off + n].reshape((N_DEV,) + shape)
    blocks = jnp.moveaxis(blocks, 0, axis)
    return blocks.reshape(shape[:axis] + (N_DEV * shape[axis],) + shape[axis + 1:])


def _pack_full(layout, total, grads):
    parts = []
    for name, axis in SHARDED:
        _, rows, shape = layout[name]
        g = grads[name]
        blocks = g.reshape(shape[:axis] + (N_DEV, shape[axis]) + shape[axis + 1:])
        blocks = jnp.moveaxis(blocks, axis, 0).reshape(N_DEV, -1, LANES)
        parts.append(jnp.pad(blocks, ((0, 0), (0, rows - blocks.shape[1]), (0, 0))))
    used = sum(p.shape[1] for p in parts)
    if total > used:
        parts.append(jnp.zeros((N_DEV, total - used, LANES), F32))
    return jnp.concatenate(parts, axis=1)


def _pad_lanes(a):
    return jnp.pad(a, ((0, 0), (0, LANES - a.shape[1])))


def _pair_layouts(c):
    s = c.shape[0]
    by_pair = c.T.reshape(N_PAIRS, 2, s)
    c_col = jnp.repeat(by_pair.transpose(0, 2, 1), HEAD_DIM, axis=2)
    c_row = jnp.pad(by_pair, ((0, 0), (0, 6), (0, 0)))
    return c_col, c_row


def _key_norm_bound(k):
    norms = jnp.sqrt(jnp.max(jnp.sum(jnp.square(k.astype(F32)).reshape(k.shape[0], N_MIX_HEADS, HEAD_DIM), axis=2), axis=0))
    rows = jnp.pad(norms.reshape(N_PAIRS, 2), ((0, 0), (0, 6)))
    return jnp.broadcast_to(rows[:, :, None], (N_PAIRS, 8, LANES))


def _forward_backward(x, mem, target, wts, small):
    n_a = wts["w_in_a"].shape[0]
    n_b = wts["w_in_b"].shape[0]
    depth = n_a + n_b
    w_kv = wts["w_kv_shared"]
    w_kv_kv = w_kv[:, :2 * MIX_W]
    w_kv_f = _pad_lanes(w_kv[:, 2 * MIX_W:])
    b_f = _pad_lanes(small["b_f"].reshape(1, -1))

    saved = []
    shared = None
    h = x
    for l in range(depth):
        is_a = l < n_a
        if l == n_a:
            hs = _rmsnorm_fwd(h, small["kv_norm_g"], name="kv_norm")
            kv = _mm(hs, w_kv_kv, name="kv_proj", out_dtype=BF16)
            f = _mm(hs, w_kv_f, name="gate_proj")
            c = _gate_fwd(f, b_f, name="gate_cumsum")
            c_col, c_row = _pair_layouts(c[:, :N_MIX_HEADS])
            shared = dict(h=h, hs=hs, kv=kv, f=f, c_col=c_col, c_row=c_row, k_max=_key_norm_bound(kv[:, :MIX_W]))
        hn = _rmsnorm_fwd(h, small["norm1_g"][l], name=f"norm1_{l}")
        memn = _rmsnorm_fwd(mem, small["mem_norm_g"][l], name=f"mem_norm_{l}")
        mkv = _mm(memn, wts["w_mem_kv"][l], name=f"mem_kv_{l}", out_dtype=BF16)
        if is_a:
            proj = _mm(hn, wts["w_in_a"][l], name=f"in_proj_{l}", out_dtype=BF16)
            mix, stat, first = _sb_fwd(proj, name=f"sb_fwd_{l}")
            q_block = 3 * MIX_W // MEM_W
        else:
            proj = _mm(hn, wts["w_in_b"][l - n_a], name=f"in_proj_{l}", out_dtype=BF16)
            mix, stat, first = _fox_fwd(proj, shared["kv"], shared["c_col"], shared["c_row"], shared["k_max"], name=f"fox_fwd_{l}")
            q_block = MIX_W // MEM_W
        mem_out = _mem_fwd(proj, q_block, mkv, name=f"mem_fwd_{l}")
        merged = jnp.concatenate([mix, mem_out], axis=1)
        h_mid = _mm(merged, wts["w_o"][l], name=f"o_proj_{l}", res=h)
        h2n = _rmsnorm_fwd(h_mid, small["norm2_g"][l], name=f"norm2_{l}")
        u, act = _mm(h2n, wts["w_mlp1"][l], name=f"mlp1_{l}", epilogue="relu2")
        h_out = _mm(act, wts["w_mlp2"][l], name=f"mlp2_{l}", res=h_mid)
        saved.append(dict(h=h, hn=hn, memn=memn, mkv=mkv, proj=proj, stat=stat, first=first, merged=merged, h_mid=h_mid, h2n=h2n, u=u, act=act,
                          q_block=q_block))
        h = h_out

    loss, dh, dg_final = _final_loss(h, target, small["final_norm_g"], name="final_loss")

    g_w = {k: [None] * wts[k].shape[0] for k in ("w_in_a", "w_in_b", "w_mem_kv", "w_o", "w_mlp1", "w_mlp2")}
    g_n = {k: [None] * depth for k in ("norm1_g", "mem_norm_g", "norm2_g")}
    dk_sh = dv_sh = dc_sh = dcq_sh = None
    for l in reversed(range(depth)):
        sv = saved[l]
        is_a = l < n_a
        du = _mm(dh, wts["w_mlp2"][l], name=f"d_act_{l}", trans_b=True, epilogue="drelu2", u=sv["u"], out_dtype=BF16)
        g_w["w_mlp2"][l] = _mm_tn(sv["act"], dh, name=f"dw_mlp2_{l}")
        g_w["w_mlp1"][l] = _mm_tn(sv["h2n"], du, name=f"dw_mlp1_{l}")
        dh2n = _mm(du, wts["w_mlp1"][l], name=f"d_h2n_{l}", trans_b=True)
        dh_mid, g_n["norm2_g"][l] = _rmsnorm_bwd(dh2n, sv["h_mid"], small["norm2_g"][l], dh, name=f"norm2_bwd_{l}")
        dmerged = _mm(dh_mid, wts["w_o"][l], name=f"d_merged_{l}", trans_b=True, out_dtype=BF16)
        g_w["w_o"][l] = _mm_tn(sv["merged"], dh_mid, name=f"dw_o_{l}")
        dqm, dmkv = _mem_bwd(sv["proj"], sv["q_block"], sv["mkv"], dmerged, name=f"mem_bwd_{l}")
        if is_a:
            dq, dk, dv = _sb_bwd(sv["proj"], dmerged, sv["stat"], sv["first"], name=f"sb_bwd_{l}")
            dproj = jnp.concatenate([dq, dk.astype(BF16), dv.astype(BF16), dqm], axis=1)
            w_in, key, idx = wts["w_in_a"][l], "w_in_a", l
        else:
            dq, dk, dv, dc, dcq = _fox_bwd(sv["proj"], shared["kv"], shared["c_col"], shared["c_row"], sv["merged"], dmerged, sv["stat"],
                                           sv["first"], name=f"fox_bwd_{l}")
            dk_sh = dk if dk_sh is None else dk_sh + dk
            dv_sh = dv if dv_sh is None else dv_sh + dv
            dc_sh = dc if dc_sh is None else dc_sh + dc
            dcq_sh = dcq if dcq_sh is None else dcq_sh + dcq
            dproj = jnp.concatenate([dq, dqm], axis=1)
            w_in, key, idx = wts["w_in_b"][l - n_a], "w_in_b", l - n_a
        g_w[key][idx] = _mm_tn(sv["hn"], dproj, name=f"dw_in_{l}")
        dhn = _mm(dproj, w_in, name=f"d_hn_{l}", trans_b=True)
        dh, g_n["norm1_g"][l] = _rmsnorm_bwd(dhn, sv["h"], small["norm1_g"][l], dh_mid, name=f"norm1_bwd_{l}")
        g_w["w_mem_kv"][l] = _mm_tn(sv["memn"], dmkv, name=f"dw_mem_kv_{l}")
        dmemn = _mm(dmkv, wts["w_mem_kv"][l], name=f"d_memn_{l}", trans_b=True)
        _, g_n["mem_norm_g"][l] = _rmsnorm_bwd(dmemn, mem, small["mem_norm_g"][l], None, name=f"mem_norm_bwd_{l}")
        if l == n_a:
            s_len = x.shape[0]
            dc_query = dcq_sh[:, :, ::HEAD_DIM].transpose(1, 0, 2).reshape(s_len, N_MIX_HEADS)
            dc_tok = _pad_lanes(dc_sh[:, :2, :].reshape(N_MIX_HEADS, s_len).T + dc_query)
            df, db = _gate_bwd(shared["f"], b_f, dc_tok, name="gate_bwd")
            dkv = jnp.concatenate([dk_sh, dv_sh], axis=1)
            dw_kv_kv = _mm_tn(shared["hs"], dkv, name="dw_kv")
            dw_kv_f = _mm_tn(shared["hs"], df, name="dw_gate")
            dhs = _mm(dkv, w_kv_kv, name="d_hs_kv", trans_b=True)
            dhs = _mm(df, w_kv_f, name="d_hs_gate", trans_b=True, res=dhs)
            dh, dg_kv = _rmsnorm_bwd(dhs, shared["h"], small["kv_norm_g"], dh, name="kv_norm_bwd")
            g_kv = jnp.concatenate([dw_kv_kv, dw_kv_f[:, :N_KV_F]], axis=1)

    grads = {k: jnp.stack(v) for k, v in g_w.items()}
    grads["w_kv_shared"] = g_kv
    d = x.shape[1]
    small_g = dict(
        norm1_g=jnp.concatenate(g_n["norm1_g"], axis=0), mem_norm_g=jnp.concatenate(g_n["mem_norm_g"], axis=0),
        norm2_g=jnp.concatenate(g_n["norm2_g"], axis=0), kv_norm_g=dg_kv.reshape(d), b_f=db[0, :N_KV_F], final_norm_g=dg_final.reshape(d))
    return loss, dh, grads, small_g


def _pack_small(vals, d):
    rows = []
    for name in REPLICATED:
        a = vals[name].astype(F32)
        if name == "b_f":
            a = jnp.pad(a, (0, d - a.shape[0]))
        rows.append(a.reshape(-1, d))
    packed = jnp.concatenate(rows, axis=0)
    pad = -packed.shape[0] % 8
    return jnp.pad(packed, ((0, pad), (0, 0)))


def _unpack_small(packed, shapes):
    out, off = {}, 0
    for name in REPLICATED:
        shape = shapes[name]
        if name == "b_f":
            out[name] = packed[off, :shape[0]]
            off += 1
        else:
            n = math.prod(shape) // packed.shape[1]
            out[name] = packed[off:off + n].reshape(shape)
            off += n
    return out


def kernel(x, mem, norm1_g, w_in_a, w_in_b, w_mem_kv, mem_norm_g, w_o, norm2_g, w_mlp1, w_mlp2, kv_norm_g, w_kv_shared, b_f, final_norm_g, loss_target, m_norm1_g, m_w_in_a, m_w_in_b, m_w_mem_kv, m_mem_norm_g, m_w_o, m_norm2_g, m_w_mlp1, m_w_mlp2, m_kv_norm_g, m_w_kv_shared, m_b_f, m_final_norm_g, v_norm1_g, v_w_in_a, v_w_in_b, v_w_mem_kv, v_mem_norm_g, v_w_o, v_norm2_g, v_w_mlp1, v_w_mlp2, v_kv_norm_g, v_w_kv_shared, v_b_f, v_final_norm_g):
    w = dict(norm1_g=norm1_g, w_in_a=w_in_a, w_in_b=w_in_b, w_mem_kv=w_mem_kv, mem_norm_g=mem_norm_g, w_o=w_o, norm2_g=norm2_g,
             w_mlp1=w_mlp1, w_mlp2=w_mlp2, kv_norm_g=kv_norm_g, w_kv_shared=w_kv_shared, b_f=b_f, final_norm_g=final_norm_g)
    m = dict(norm1_g=m_norm1_g, w_in_a=m_w_in_a, w_in_b=m_w_in_b, w_mem_kv=m_w_mem_kv, mem_norm_g=m_mem_norm_g, w_o=m_w_o,
             norm2_g=m_norm2_g, w_mlp1=m_w_mlp1, w_mlp2=m_w_mlp2, kv_norm_g=m_kv_norm_g, w_kv_shared=m_w_kv_shared, b_f=m_b_f,
             final_norm_g=m_final_norm_g)
    v = dict(norm1_g=v_norm1_g, w_in_a=v_w_in_a, w_in_b=v_w_in_b, w_mem_kv=v_w_mem_kv, mem_norm_g=v_mem_norm_g, w_o=v_w_o,
             norm2_g=v_norm2_g, w_mlp1=v_w_mlp1, w_mlp2=v_w_mlp2, kv_norm_g=v_kv_norm_g, w_kv_shared=v_w_kv_shared, b_f=v_b_f,
             final_norm_g=v_final_norm_g)
    d = x.shape[-1]
    layout, total = _layout({name: w[name].shape for name, _ in SHARDED})

    gathered = _all_gather(_pack_shards(layout, total, w, BF16), name="gather_weights").reshape(N_DEV, total, LANES)
    wts = {name: _unpack_full(layout, gathered, name, axis) for name, axis in SHARDED}
    small = {name: w[name] for name in REPLICATED}

    loss, grad_x, grads, small_g = _forward_backward(x[0], mem[0], loss_target[0], wts, small)

    parts = _exchange_partials(_pack_full(layout, total, grads).astype(BF16), name="exchange_grads")
    g_flat, d_flat, m_flat, v_flat = _adamw(parts, _pack_shards(layout, total, w, F32), _pack_shards(layout, total, m, F32),
                                            _pack_shards(layout, total, v, F32), name="adamw_sharded")

    small_packed = _pack_small(small_g, d)
    n_small = small_packed.shape[0]
    small_parts = _all_gather(small_packed, name="gather_small_grads").reshape(N_DEV, n_small, d)
    gs, ds_, ms, vs = _adamw(small_parts, _pack_small(w, d), _pack_small(m, d), _pack_small(v, d), name="adamw_replicated")

    shapes = {name: w[name].shape for name in REPLICATED}
    out_g, out_d, out_m, out_v = {}, {}, {}, {}
    for flat, small_flat, out in ((g_flat, gs, out_g), (d_flat, ds_, out_d), (m_flat, ms, out_m), (v_flat, vs, out_v)):
        for name, _ in SHARDED:
            out[name] = _unpack_shard(layout, flat, name)
        out.update(_unpack_small(small_flat, shapes))

    loss_total = lax.psum(loss[0, 0], ("x", "y", "c"))
    return (loss_total, grad_x[None], *[out_g[n] for n in WEIGHT_ORDER], *[out_d[n] for n in WEIGHT_ORDER],
            *[out_m[n] for n in WEIGHT_ORDER], *[out_v[n] for n in WEIGHT_ORDER])
```

```python
import functools
import math

import jax
import jax.numpy as jnp
from jax import lax
from jax.experimental import pallas as pl
from jax.experimental.pallas import tpu as pltpu

F32 = jnp.float32
BF16 = jnp.bfloat16

N_DEV = 8
HEAD_DIM = 64
N_MIX_HEADS = 8
N_MEM_HEADS = 4
MIX_W = N_MIX_HEADS * HEAD_DIM
MEM_W = N_MEM_HEADS * HEAD_DIM
N_PAIRS = N_MIX_HEADS // 2
LANES = 128
SB_B = 256
FOX_B = 512
CUM_SUB = 256
SB_CUT = 64.0
FOX_CUT = 45.0
EPS = 1e-6
NEG_INF = -1e30
QK_SCALE = 1.0 / math.sqrt(HEAD_DIM)
LOG2E = 1.4426950408889634
N_KV_F = 8

ADAM_LR = 0.001
ADAM_B1 = 0.9
ADAM_B2 = 0.999
ADAM_EPS = 1e-08
ADAM_WD = 0.01
ADAM_STEP = 10

VMEM_LIMIT = 56 * 1024 * 1024
PACK_ROW_ALIGN = 16
PACK_BLOCK_ROWS = 1024

MESH = pl.DeviceIdType.MESH

SHARDED = (("w_in_a", 2), ("w_in_b", 1), ("w_mem_kv", 1), ("w_o", 2), ("w_mlp1", 2), ("w_mlp2", 1), ("w_kv_shared", 1))
REPLICATED = ("norm1_g", "mem_norm_g", "norm2_g", "kv_norm_g", "b_f", "final_norm_g")
WEIGHT_ORDER = ("norm1_g", "w_in_a", "w_in_b", "w_mem_kv", "mem_norm_g", "w_o", "norm2_g", "w_mlp1", "w_mlp2",
                "kv_norm_g", "w_kv_shared", "b_f", "final_norm_g")


def _cparams(sem=None):
    return pltpu.CompilerParams(dimension_semantics=sem, vmem_limit_bytes=VMEM_LIMIT)


def _resident(shape, index_map):
    return pl.BlockSpec(shape, index_map, pipeline_mode=pl.Buffered(1))


def _pick(n, cands):
    for c in cands:
        if c <= n and n % c == 0:
            return c
    return n


def _dot(a, b, dims):
    return lax.dot_general(a, b, (dims, ((), ())), preferred_element_type=F32)


NN = ((1,), (0,))
NT = ((1,), (1,))
TN = ((0,), (0,))


MM_CHUNK = 512


def _mm(a, b, *, name, trans_b=False, out_dtype=F32, res=None, epilogue=None, u=None, norm=None):
    m, k = a.shape
    n = b.shape[0] if trans_b else b.shape[1]
    tm = _pick(m, (512, 256, 128))
    tn = _pick(n, (MM_CHUNK, 384, 256, 128))
    has_dres = norm is not None and norm[2] is not None

    def body(*refs):
        a_ref, b_ref = refs[0], refs[1]
        pos = 2
        res_ref = u_ref = x_ref = g_ref = dres_ref = None
        if res is not None:
            res_ref = refs[pos]
            pos += 1
        if u is not None:
            u_ref = refs[pos]
            pos += 1
        if norm is not None:
            x_ref, g_ref = refs[pos], refs[pos + 1]
            pos += 2
            if has_dres:
                dres_ref = refs[pos]
                pos += 1
        outs = refs[pos:]
        av = a_ref[...].astype(BF16)
        prods = []
        for c in range(n // tn):
            cols = slice(c * tn, (c + 1) * tn)
            if trans_b:
                acc = _dot(av, b_ref[cols, :].astype(BF16), NT)
            else:
                acc = _dot(av, b_ref[:, cols].astype(BF16), NN)
            if res_ref is not None:
                acc = res_ref[:, cols] + acc
            if norm is not None:
                prods.append(acc)
            elif epilogue == "relu2":
                outs[0][:, cols] = acc.astype(BF16)
                r = jnp.maximum(acc, 0.0)
                outs[1][:, cols] = (r * r).astype(BF16)
            elif epilogue == "drelu2":
                outs[0][:, cols] = (acc * (2.0 * jnp.maximum(u_ref[:, cols], 0.0))).astype(out_dtype)
            else:
                outs[0][:, cols] = acc.astype(out_dtype)
        if norm is not None:
            dyv = prods[0] if len(prods) == 1 else jnp.concatenate(prods, axis=1)
            xf = x_ref[...]
            r = lax.rsqrt(jnp.mean(xf * xf, axis=-1, keepdims=True) + EPS)
            xh = xf * r
            dyg = dyv * g_ref[...]
            dx = r * (dyg - xh * jnp.mean(dyg * xh, axis=-1, keepdims=True))
            outs[0][...] = dx if dres_ref is None else dres_ref[...] + dx

            @pl.when(pl.program_id(0) == 0)
            def _():
                outs[1][...] = jnp.zeros_like(outs[1])

            outs[1][...] += jnp.sum(dyv * xh, axis=0, keepdims=True)

    row = pl.BlockSpec((tm, n), lambda i: (i, 0))
    vec = pl.BlockSpec((1, n), lambda i: (0, 0))
    in_specs = [pl.BlockSpec((tm, k), lambda i: (i, 0)), _resident(b.shape, lambda i: (0, 0))]
    args = [a, b]
    if res is not None:
        in_specs.append(row)
        args.append(res)
    if u is not None:
        in_specs.append(row)
        args.append(u)
    if norm is not None:
        in_specs += [row, vec] + ([row] if has_dres else [])
        args += [norm[0], norm[1].reshape(1, n)] + ([norm[2]] if has_dres else [])
        out_shape = (jax.ShapeDtypeStruct((m, n), F32), jax.ShapeDtypeStruct((1, n), F32))
        out_specs = (row, vec)
    elif epilogue == "relu2":
        out_shape = (jax.ShapeDtypeStruct((m, n), BF16), jax.ShapeDtypeStruct((m, n), BF16))
        out_specs = (row, row)
    else:
        out_shape = (jax.ShapeDtypeStruct((m, n), out_dtype),)
        out_specs = (row,)
    outs = pl.pallas_call(
        body, name=name, grid=(m // tm,), in_specs=in_specs, out_specs=out_specs, out_shape=out_shape,
        compiler_params=_cparams(("arbitrary",) if norm is not None else ("parallel",)),
    )(*args)
    return outs if (epilogue == "relu2" or norm is not None) else outs[0]


def _mm_tn(x, dy, *, name):
    m, k1 = x.shape
    n = dy.shape[1]
    t1 = _pick(k1, (1024, 896, 768, 512, 256, 128))
    tn = _pick(n, (1024, 896, 768, 512, 256, 128))
    tm = _pick(m, (2048, 1024, 512, 256, 128))
    nm = m // tm

    def body(x_ref, dy_ref, o_ref):
        mm = pl.program_id(2)

        @pl.when(mm == 0)
        def _():
            o_ref[...] = jnp.zeros_like(o_ref)

        o_ref[...] += _dot(x_ref[...].astype(BF16), dy_ref[...].astype(BF16), TN)

    return pl.pallas_call(
        body, name=name, grid=(k1 // t1, n // tn, nm),
        in_specs=[pl.BlockSpec((tm, t1), lambda i, j, mm: (mm, i)), pl.BlockSpec((tm, tn), lambda i, j, mm: (mm, j))],
        out_specs=pl.BlockSpec((t1, tn), lambda i, j, mm: (i, j)),
        out_shape=jax.ShapeDtypeStruct((k1, n), F32),
        compiler_params=_cparams(("parallel", "parallel", "arbitrary")),
    )(x, dy)


def _rmsnorm_fwd(x, g, *, name):
    s, d = x.shape
    tm = _pick(s, (512, 256, 128))

    def body(x_ref, g_ref, o_ref):
        xf = x_ref[...]
        r = lax.rsqrt(jnp.mean(xf * xf, axis=-1, keepdims=True) + EPS)
        o_ref[...] = (xf * r * g_ref[...]).astype(BF16)

    return pl.pallas_call(
        body, name=name, grid=(s // tm,),
        in_specs=[pl.BlockSpec((tm, d), lambda i: (i, 0)), pl.BlockSpec((1, d), lambda i: (0, 0))],
        out_specs=pl.BlockSpec((tm, d), lambda i: (i, 0)),
        out_shape=jax.ShapeDtypeStruct((s, d), BF16),
        compiler_params=_cparams(("parallel",)),
    )(x, g.reshape(1, d))


def _final_loss(h, target, g, *, name):
    s, d = h.shape
    tm = _pick(s, (256, 128))

    def body(h_ref, t_ref, g_ref, loss_ref, dh_ref, dg_ref):
        i = pl.program_id(0)
        xf = h_ref[...]
        gv = g_ref[...]
        r = lax.rsqrt(jnp.mean(xf * xf, axis=-1, keepdims=True) + EPS)
        xh = xf * r
        err = xh * gv - t_ref[...]
        part = 0.5 * jnp.sum(jnp.mean(err * err, axis=-1, keepdims=True), axis=0, keepdims=True)
        dyv = err * (1.0 / d)
        dyg = dyv * gv
        dh_ref[...] = r * (dyg - xh * jnp.mean(dyg * xh, axis=-1, keepdims=True))

        @pl.when(i == 0)
        def _():
            dg_ref[...] = jnp.zeros_like(dg_ref)
            loss_ref[...] = jnp.zeros_like(loss_ref)

        dg_ref[...] += jnp.sum(dyv * xh, axis=0, keepdims=True)
        loss_ref[...] += jnp.broadcast_to(part, loss_ref.shape)

    row = pl.BlockSpec((tm, d), lambda i: (i, 0))
    vec = pl.BlockSpec((1, d), lambda i: (0, 0))
    return pl.pallas_call(
        body, name=name, grid=(s // tm,), in_specs=[row, row, vec],
        out_specs=(pl.BlockSpec((1, LANES), lambda i: (0, 0)), row, vec),
        out_shape=(jax.ShapeDtypeStruct((1, LANES), F32), jax.ShapeDtypeStruct((s, d), F32), jax.ShapeDtypeStruct((1, d), F32)),
        compiler_params=_cparams(("arbitrary",)),
    )(h, target, g.reshape(1, d))


def _head_lanes(rows):
    lane = lax.broadcasted_iota(jnp.int32, (rows, LANES), 1)
    return [lane < HEAD_DIM, lane >= HEAD_DIM]


def _tri(b, cmp):
    row = lax.broadcasted_iota(jnp.int32, (b, b), 0)
    col = lax.broadcasted_iota(jnp.int32, (b, b), 1)
    return cmp(row, col)


def _twice(mask):
    return jnp.concatenate([mask, mask], axis=0)


def _stack_heads(x, heads):
    zero = jnp.zeros_like(x)
    return jnp.concatenate([jnp.where(heads[0], x, zero), jnp.where(heads[1], x, zero)], axis=0)


def _unstack_heads(x2, heads):
    b = x2.shape[0] // 2
    return jnp.where(heads[0], x2[:b], x2[b:])


def _stack_stat(stat):
    return jnp.concatenate([stat[:, 0:1], stat[:, HEAD_DIM:HEAD_DIM + 1]], axis=0)


def _unstack_stat(col, heads):
    b = col.shape[0] // 2
    return jnp.where(heads[0], jnp.broadcast_to(col[:b], (b, LANES)), jnp.broadcast_to(col[b:], (b, LANES)))


def _tri_dot(x, tri_bf16):
    return _dot(x.astype(BF16), tri_bf16, NN)


def _prefix_sums(x, tri_bf16, inclusive):
    sub = tri_bf16.shape[0]
    outs, carry = [], None
    for c in range(x.shape[1] // sub):
        xs = x[:, c * sub:(c + 1) * sub]
        loc = _tri_dot(xs, tri_bf16)
        outs.append(loc if carry is None else loc + carry)
        tot = loc[:, sub - 1:sub] if inclusive else loc[:, sub - 1:sub] + xs[:, sub - 1:sub]
        carry = tot if carry is None else carry + tot
    return (outs[0] if len(outs) == 1 else jnp.concatenate(outs, axis=1)), carry


def _suffix_sums(x, tri_bf16):
    sub = tri_bf16.shape[0]
    n = x.shape[1] // sub
    outs, carry = [None] * n, None
    for c in reversed(range(n)):
        xs = x[:, c * sub:(c + 1) * sub]
        loc = _tri_dot(xs, tri_bf16)
        outs[c] = loc if carry is None else loc + carry
        tot = loc[:, 0:1] + xs[:, 0:1].astype(BF16).astype(F32)
        carry = tot if carry is None else carry + tot
    return (outs[0] if n == 1 else jnp.concatenate(outs, axis=1)), carry


def _softplus2(z):
    z2 = z * LOG2E
    neg_abs = lax.bitcast_convert_type(lax.bitcast_convert_type(z2, jnp.uint32) | jnp.uint32(0x80000000), F32)
    return z2, jnp.maximum(z2, 0.0) + jnp.log2(1.0 + jnp.exp2(neg_abs))


def _block_rows(j, b):
    return pl.ds(pl.multiple_of(j * b, b), b)


def _sb_fwd(proj, *, name):
    s = proj.shape[0]
    b = _pick(s, (SB_B, 128))

    def body(q_ref, k_ref, v_ref, o_ref, tot_ref, first_ref, acc_ref):
        i = pl.program_id(1)
        heads = _head_lanes(b)
        suffix = _tri(min(b, CUM_SUB), lambda r, c: r > c).astype(BF16)
        strict = _twice(_tri(b, lambda r, c: c < r))
        q2 = _stack_heads(q_ref[...] * QK_SCALE, heads)

        def tile(j, a, masked):
            rows = _block_rows(j, b)
            z2, sp = _softplus2(_dot(q2, k_ref[rows, :], NT))
            if masked:
                sp = jnp.where(strict, sp, 0.0)
            rsum, total = _suffix_sums(sp, suffix)
            w = jnp.exp2((z2 - sp) - (a + rsum))
            if masked:
                w = jnp.where(strict, w, 0.0)
            acc_ref[...] += _dot(w.astype(BF16), v_ref[rows, :], NN)
            return a + total

        acc_ref[...] = jnp.zeros_like(acc_ref)
        a = tile(i, jnp.zeros((2 * b, 1), F32), True)

        def more(c):
            return jnp.logical_and(c[0] < i, c[2] < SB_CUT)

        def step(c):
            a = tile(i - 1 - c[0], c[1], False)
            return c[0] + 1, a, jnp.min(a)

        done, a, _ = lax.while_loop(more, step, (jnp.int32(0), a, jnp.min(a)))
        o_ref[...] = _unstack_heads(acc_ref[...], heads).astype(BF16)
        tot_ref[0] = _unstack_stat(a, heads)
        first_ref[pl.program_id(0), i] = i - done

    qblk = pl.BlockSpec((b, LANES), lambda p, i: (i, p))
    return pl.pallas_call(
        body, name=name, grid=(N_PAIRS, s // b),
        in_specs=[qblk, _resident((s, LANES), lambda p, i: (0, N_PAIRS + p)), _resident((s, LANES), lambda p, i: (0, 2 * N_PAIRS + p))],
        out_specs=(qblk, pl.BlockSpec((1, b, LANES), lambda p, i: (p, i, 0)), pl.BlockSpec(memory_space=pltpu.SMEM)),
        out_shape=(jax.ShapeDtypeStruct((s, MIX_W), BF16), jax.ShapeDtypeStruct((N_PAIRS, s, LANES), F32),
                   jax.ShapeDtypeStruct((N_PAIRS, s // b), jnp.int32)),
        scratch_shapes=[pltpu.VMEM((2 * b, LANES), F32)],
        compiler_params=_cparams(("arbitrary", "arbitrary")),
    )(proj, proj, proj)


def _sb_bwd(proj, dmerged, tot, first, *, name):
    s = proj.shape[0]
    b = _pick(s, (SB_B, 128))

    def body(q_ref, k_ref, v_ref, do_ref, tot_ref, first_ref, dq_ref, dk_ref, dv_ref, dq_acc):
        i = pl.program_id(1)

        @pl.when(i == 0)
        def _():
            dk_ref[...] = jnp.zeros_like(dk_ref)
            dv_ref[...] = jnp.zeros_like(dv_ref)

        heads = _head_lanes(b)
        incl = _tri(min(b, CUM_SUB), lambda r, c: r <= c).astype(BF16)
        excl = _tri(min(b, CUM_SUB), lambda r, c: r < c).astype(BF16)
        strict = _twice(_tri(b, lambda r, c: c < r))
        q2 = _stack_heads(q_ref[...] * QK_SCALE, heads)
        do2 = _stack_heads(do_ref[...], heads)
        tot2 = _stack_stat(tot_ref[0])

        def tile(j, pre, gpre, masked):
            rows = _block_rows(j, b)
            kb = k_ref[rows, :]
            z2, sp = _softplus2(_dot(q2, kb, NT))
            oms = jnp.exp2(-sp)
            if masked:
                sp = jnp.where(strict, sp, 0.0)
            pin, ptot = _prefix_sums(sp, incl, True)
            w = jnp.exp2((z2 - sp) + (pin + (pre - tot2)))
            if masked:
                w = jnp.where(strict, w, 0.0)
            gw = _dot(do2, v_ref[rows, :], NT) * w
            gex, gtot = _prefix_sums(gw, excl, False)
            dz = gw * oms - (1.0 - oms) * (gpre + gex)
            if masked:
                dz = jnp.where(strict, dz, 0.0)
            dzb = dz.astype(BF16)
            dq_acc[...] += _dot(dzb, kb, NN)
            dk_ref[rows, :] += _dot(dzb, q2, TN)
            dv_ref[rows, :] += _dot(w.astype(BF16), do2, TN)
            return pre + ptot, gpre + gtot

        dq_acc[...] = jnp.zeros_like(dq_acc)
        zero = jnp.zeros((2 * b, 1), F32)
        pre, gpre = lax.fori_loop(first_ref[pl.program_id(0), i], i, lambda j, c: tile(j, c[0], c[1], False), (zero, zero))
        tile(i, pre, gpre, True)
        dq_ref[...] = (_unstack_heads(dq_acc[...], heads) * QK_SCALE).astype(BF16)

    qblk = pl.BlockSpec((b, LANES), lambda p, i: (i, p))
    full = _resident((s, LANES), lambda p, i: (0, p))
    return pl.pallas_call(
        body, name=name, grid=(N_PAIRS, s // b),
        in_specs=[qblk, _resident((s, LANES), lambda p, i: (0, N_PAIRS + p)), _resident((s, LANES), lambda p, i: (0, 2 * N_PAIRS + p)),
                  qblk, pl.BlockSpec((1, b, LANES), lambda p, i: (p, i, 0)), pl.BlockSpec(memory_space=pltpu.SMEM)],
        out_specs=(qblk, full, full),
        out_shape=(jax.ShapeDtypeStruct((s, MIX_W), BF16), jax.ShapeDtypeStruct((s, MIX_W), F32), jax.ShapeDtypeStruct((s, MIX_W), F32)),
        scratch_shapes=[pltpu.VMEM((2 * b, LANES), F32)],
        compiler_params=_cparams(("parallel", "arbitrary")),
    )(proj, proj, proj, dmerged, tot, first)


def _fox_fwd(proj, kv, c_col, c_row, k_max, *, name):
    s = proj.shape[0]
    b = _pick(s, (FOX_B, 128))

    def body(q_ref, k_ref, v_ref, cc_ref, cr_ref, km_ref, o_ref, lse_ref, first_ref, acc_ref):
        i = pl.program_id(1)
        heads = _head_lanes(b)
        causal = _twice(_tri(b, lambda r, c: c <= r))
        top = lax.broadcasted_iota(jnp.int32, (2 * b, b), 0) < b
        q2 = _stack_heads(q_ref[...] * QK_SCALE, heads)
        c_t = _stack_stat(cc_ref[0])
        qf = q2.astype(F32)
        kmv = km_ref[0]
        z_max = jnp.sqrt(jnp.sum(qf * qf, axis=1, keepdims=True)) * jnp.where(top[:, 0:1], kmv[0:1, 0:1], kmv[1:2, 0:1]) * 1.001

        def tile(j, m, l, masked):
            rows = _block_rows(j, b)
            gate = c_t - jnp.where(top, cr_ref[0, 0:1, rows], cr_ref[0, 1:2, rows])
            sc = _dot(q2, k_ref[rows, :], NT) + gate
            if masked:
                sc = jnp.where(causal, sc, NEG_INF)
            m_new = jnp.maximum(m, jnp.max(sc, axis=1, keepdims=True))
            p = jnp.exp(sc - m_new)
            alpha = jnp.exp(m - m_new)
            acc_ref[...] = alpha * acc_ref[...] + _dot(p.astype(BF16), v_ref[rows, :], NN)
            return m_new, alpha * l + jnp.sum(p, axis=1, keepdims=True), jnp.max(z_max + gate[:, 0:1] - m_new)

        acc_ref[...] = jnp.zeros_like(acc_ref)
        m, l, slack = tile(i, jnp.full((2 * b, 1), NEG_INF, F32), jnp.zeros((2 * b, 1), F32), True)

        def more(c):
            return jnp.logical_and(c[0] < i, c[3] > -FOX_CUT)

        def step(c):
            m, l, slack = tile(i - 1 - c[0], c[1], c[2], False)
            return c[0] + 1, m, l, slack

        done, m, l, _ = lax.while_loop(more, step, (jnp.int32(0), m, l, slack))
        o_ref[...] = _unstack_heads(acc_ref[...] * (1.0 / l), heads).astype(BF16)
        lse_ref[0] = _unstack_stat(m + jnp.log(l), heads)
        first_ref[pl.program_id(0), i] = i - done

    qblk = pl.BlockSpec((b, LANES), lambda p, i: (i, p))
    stat = pl.BlockSpec((1, b, LANES), lambda p, i: (p, i, 0))
    return pl.pallas_call(
        body, name=name, grid=(N_PAIRS, s // b),
        in_specs=[qblk, _resident((s, LANES), lambda p, i: (0, p)), _resident((s, LANES), lambda p, i: (0, N_PAIRS + p)),
                  stat, _resident((1, 8, s), lambda p, i: (p, 0, 0)), pl.BlockSpec((1, 8, LANES), lambda p, i: (p, 0, 0))],
        out_specs=(qblk, stat, pl.BlockSpec(memory_space=pltpu.SMEM)),
        out_shape=(jax.ShapeDtypeStruct((s, MIX_W), BF16), jax.ShapeDtypeStruct((N_PAIRS, s, LANES), F32),
                   jax.ShapeDtypeStruct((N_PAIRS, s // b), jnp.int32)),
        scratch_shapes=[pltpu.VMEM((2 * b, LANES), F32)],
        compiler_params=_cparams(("arbitrary", "arbitrary")),
    )(proj, kv, kv, c_col, c_row, k_max)


def _fox_bwd(proj, kv, c_col, c_row, merged, dmerged, lse, first, *, name):
    s = proj.shape[0]
    b = _pick(s, (FOX_B, 128))

    def body(q_ref, k_ref, v_ref, cc_ref, cr_ref, o_ref, do_ref, lse_ref, first_ref, dq_ref, dk_ref, dv_ref, dc_ref, dcq_ref, dq_acc):
        i = pl.program_id(1)

        @pl.when(i == 0)
        def _():
            dk_ref[...] = jnp.zeros_like(dk_ref)
            dv_ref[...] = jnp.zeros_like(dv_ref)
            dc_ref[...] = jnp.zeros_like(dc_ref)

        heads = _head_lanes(b)
        causal = _twice(_tri(b, lambda r, c: c <= r))
        top = lax.broadcasted_iota(jnp.int32, (2 * b, b), 0) < b
        q2 = _stack_heads(q_ref[...] * QK_SCALE, heads)
        dov = do_ref[...]
        do2 = _stack_heads(dov, heads)
        prod = dov.astype(F32) * o_ref[...].astype(F32)
        delta = jnp.concatenate([jnp.sum(jnp.where(heads[hh], prod, 0.0), axis=1, keepdims=True) for hh in range(2)], axis=0)
        c_t = _stack_stat(cc_ref[0])
        lse_t = _stack_stat(lse_ref[0])

        def tile(j, rsum, masked):
            rows = _block_rows(j, b)
            kb = k_ref[rows, :]
            c_s = jnp.where(top, cr_ref[0, 0:1, rows], cr_ref[0, 1:2, rows])
            sc = _dot(q2, kb, NT) + (c_t - c_s)
            p = jnp.exp(sc - lse_t)
            if masked:
                p = jnp.where(causal, p, 0.0)
            ds = p * (_dot(do2, v_ref[rows, :], NT) - delta)
            dsb = ds.astype(BF16)
            dq_acc[...] += _dot(dsb, kb, NN)
            dk_ref[rows, :] += _dot(dsb, q2, TN)
            dv_ref[rows, :] += _dot(p.astype(BF16), do2, TN)
            dc_ref[0, 0:1, rows] -= jnp.sum(ds[:b], axis=0, keepdims=True)
            dc_ref[0, 1:2, rows] -= jnp.sum(ds[b:], axis=0, keepdims=True)
            return rsum + jnp.sum(ds, axis=1, keepdims=True)

        dq_acc[...] = jnp.zeros_like(dq_acc)
        rsum = lax.fori_loop(first_ref[pl.program_id(0), i], i, lambda j, r: tile(j, r, False), jnp.zeros((2 * b, 1), F32))
        rsum = tile(i, rsum, True)
        dq_ref[...] = (_unstack_heads(dq_acc[...], heads) * QK_SCALE).astype(BF16)
        dcq_ref[0] = _unstack_stat(rsum, heads)

    qblk = pl.BlockSpec((b, LANES), lambda p, i: (i, p))
    stat = pl.BlockSpec((1, b, LANES), lambda p, i: (p, i, 0))
    crow = _resident((1, 8, s), lambda p, i: (p, 0, 0))
    full = _resident((s, LANES), lambda p, i: (0, p))
    return pl.pallas_call(
        body, name=name, grid=(N_PAIRS, s // b),
        in_specs=[qblk, full, _resident((s, LANES), lambda p, i: (0, N_PAIRS + p)), stat, crow, qblk, qblk, stat,
                  pl.BlockSpec(memory_space=pltpu.SMEM)],
        out_specs=(qblk, full, full, crow, stat),
        out_shape=(jax.ShapeDtypeStruct((s, MIX_W), BF16), jax.ShapeDtypeStruct((s, MIX_W), F32), jax.ShapeDtypeStruct((s, MIX_W), F32),
                   jax.ShapeDtypeStruct((N_PAIRS, 8, s), F32), jax.ShapeDtypeStruct((N_PAIRS, s, LANES), F32)),
        scratch_shapes=[pltpu.VMEM((2 * b, LANES), F32)],
        compiler_params=_cparams(("parallel", "arbitrary")),
    )(proj, kv, kv, c_col, c_row, merged, dmerged, lse, first)


MEM_TQ = 1024


def _mem_fwd(proj, q_col_block, mkv, *, name):
    s = proj.shape[0]
    tq = _pick(s, (MEM_TQ, 128))
    n_mem = mkv.shape[0]

    def body(q_ref, mkv_ref, o_ref):
        heads = _head_lanes(tq)
        for pp in range(MEM_W // LANES):
            cols = slice(pp * LANES, (pp + 1) * LANES)
            qv = q_ref[:, cols] * QK_SCALE
            mk = mkv_ref[:, pp * LANES:(pp + 1) * LANES]
            mv = mkv_ref[:, MEM_W + pp * LANES:MEM_W + (pp + 1) * LANES]
            o_sel = None
            for hh in range(2):
                qm = jnp.where(heads[hh], qv, jnp.zeros_like(qv))
                sc = _dot(qm, mk, NT)
                p = jnp.exp(sc - jnp.max(sc, axis=1, keepdims=True))
                p = p / jnp.sum(p, axis=1, keepdims=True)
                out = _dot(p.astype(BF16), mv, NN)
                o_sel = out if hh == 0 else jnp.where(heads[0], o_sel, out)
            o_ref[:, cols] = o_sel.astype(BF16)

    return pl.pallas_call(
        body, name=name, grid=(s // tq,),
        in_specs=[pl.BlockSpec((tq, MEM_W), lambda i: (i, q_col_block)), pl.BlockSpec((n_mem, 2 * MEM_W), lambda i: (0, 0))],
        out_specs=pl.BlockSpec((tq, MEM_W), lambda i: (i, 0)),
        out_shape=jax.ShapeDtypeStruct((s, MEM_W), BF16),
        compiler_params=_cparams(("parallel",)),
    )(proj, mkv)


def _mem_bwd(proj, q_col_block, mkv, dmerged, *, name):
    s = proj.shape[0]
    tq = _pick(s, (MEM_TQ, 128))
    n_mem = mkv.shape[0]

    def body(q_ref, mkv_ref, do_ref, dq_ref, dmkv_ref):
        i = pl.program_id(0)

        @pl.when(i == 0)
        def _():
            dmkv_ref[...] = jnp.zeros_like(dmkv_ref)

        heads = _head_lanes(tq)
        for pp in range(MEM_W // LANES):
            cols = slice(pp * LANES, (pp + 1) * LANES)
            vcols = slice(MEM_W + pp * LANES, MEM_W + (pp + 1) * LANES)
            qv = q_ref[:, cols] * QK_SCALE
            dov = do_ref[:, cols]
            mk = mkv_ref[:, cols]
            mv = mkv_ref[:, vcols]
            dq_sel = None
            for hh in range(2):
                qm = jnp.where(heads[hh], qv, jnp.zeros_like(qv))
                dom = jnp.where(heads[hh], dov, jnp.zeros_like(dov))
                sc = _dot(qm, mk, NT)
                p = jnp.exp(sc - jnp.max(sc, axis=1, keepdims=True))
                p = p / jnp.sum(p, axis=1, keepdims=True)
                dp = _dot(dom, mv, NT)
                ds = p * (dp - jnp.sum(p * dp, axis=1, keepdims=True))
                dsb = ds.astype(BF16)
                dq = _dot(dsb, mk, NN)
                dmkv_ref[:, cols] += _dot(dsb, qm, TN)
                dmkv_ref[:, vcols] += _dot(p.astype(BF16), dom, TN)
                dq_sel = dq if hh == 0 else jnp.where(heads[0], dq_sel, dq)
            dq_ref[:, cols] = (dq_sel * QK_SCALE).astype(BF16)

    return pl.pallas_call(
        body, name=name, grid=(s // tq,),
        in_specs=[pl.BlockSpec((tq, MEM_W), lambda i: (i, q_col_block)), pl.BlockSpec((n_mem, 2 * MEM_W), lambda i: (0, 0)),
                  pl.BlockSpec((tq, MEM_W), lambda i: (i, MIX_W // MEM_W))],
        out_specs=(pl.BlockSpec((tq, MEM_W), lambda i: (i, 0)), pl.BlockSpec((n_mem, 2 * MEM_W), lambda i: (0, 0))),
        out_shape=(jax.ShapeDtypeStruct((s, MEM_W), BF16), jax.ShapeDtypeStruct((n_mem, 2 * MEM_W), F32)),
        compiler_params=_cparams(("arbitrary",)),
    )(proj, mkv, dmerged)


GATE_TB = 256


def _split3_dot(tri_bf16, x):
    x1 = x.astype(BF16)
    r1 = x - x1.astype(F32)
    x2 = r1.astype(BF16)
    x3 = (r1 - x2.astype(F32)).astype(BF16)
    return _dot(tri_bf16, x1, NN) + _dot(tri_bf16, x2, NN) + _dot(tri_bf16, x3, NN)


def _gate_fwd(f, b, *, name):
    s = f.shape[0]
    tb = _pick(s, (GATE_TB, 128))

    def body(f_ref, b_ref, c_ref, carry_ref):
        i = pl.program_id(0)

        @pl.when(i == 0)
        def _():
            carry_ref[...] = jnp.zeros_like(carry_ref)

        x = f_ref[...] + b_ref[...]
        lf = jnp.minimum(x, 0.0) - jnp.log1p(jnp.exp(-jnp.abs(x)))
        row = lax.broadcasted_iota(jnp.int32, (tb, tb), 0)
        col = lax.broadcasted_iota(jnp.int32, (tb, tb), 1)
        lower = (col <= row).astype(BF16)
        c = carry_ref[...] + _split3_dot(lower, lf)
        c_ref[...] = c
        carry_ref[...] = c[tb - 1:tb, :]

    return pl.pallas_call(
        body, name=name, grid=(s // tb,),
        in_specs=[pl.BlockSpec((tb, LANES), lambda i: (i, 0)), pl.BlockSpec((1, LANES), lambda i: (0, 0))],
        out_specs=pl.BlockSpec((tb, LANES), lambda i: (i, 0)),
        out_shape=jax.ShapeDtypeStruct((s, LANES), F32),
        scratch_shapes=[pltpu.VMEM((1, LANES), F32)],
        compiler_params=_cparams(("arbitrary",)),
    )(f, b)


def _gate_bwd(f, b, dc, *, name):
    s = f.shape[0]
    tb = _pick(s, (GATE_TB, 128))
    nb = s // tb

    def body(f_ref, b_ref, dc_ref, df_ref, db_ref, carry_ref):
        i = pl.program_id(0)

        @pl.when(i == 0)
        def _():
            carry_ref[...] = jnp.zeros_like(carry_ref)
            db_ref[...] = jnp.zeros_like(db_ref)

        row = lax.broadcasted_iota(jnp.int32, (tb, tb), 0)
        col = lax.broadcasted_iota(jnp.int32, (tb, tb), 1)
        upper = (col >= row).astype(BF16)
        dlf = carry_ref[...] + _split3_dot(upper, dc_ref[...])
        carry_ref[...] = dlf[0:1, :]
        x = f_ref[...] + b_ref[...]
        e = jnp.exp(-jnp.abs(x))
        one_minus_sig = jnp.where(x >= 0.0, e, 1.0) / (1.0 + e)
        df = dlf * one_minus_sig
        df_ref[...] = df
        db_ref[...] += jnp.sum(df, axis=0, keepdims=True)

    rev = pl.BlockSpec((tb, LANES), lambda i: (nb - 1 - i, 0))
    vec = pl.BlockSpec((1, LANES), lambda i: (0, 0))
    return pl.pallas_call(
        body, name=name, grid=(nb,), in_specs=[rev, vec, rev], out_specs=(rev, vec),
        out_shape=(jax.ShapeDtypeStruct((s, LANES), F32), jax.ShapeDtypeStruct((1, LANES), F32)),
        scratch_shapes=[pltpu.VMEM((1, LANES), F32)],
        compiler_params=_cparams(("arbitrary",)),
    )(f, b, dc)


def _all_gather(x, *, name):
    r, cdim = x.shape

    def body(x_ref, out_ref, send_sems, recv_sems, local_sem):
        mx, my, mc = lax.axis_index("x"), lax.axis_index("y"), lax.axis_index("c")
        me, sibling = (mx, my, mc), (mx, my, 1 - mc)
        chips = [(1 - mx, my), (mx, 1 - my), (1 - mx, 1 - my)]

        def rows(px, py, pc):
            return out_ref.at[pl.ds((4 * px + 2 * py + pc) * r, r), :]

        def copy(k, block, to, src=None):
            return pltpu.make_async_remote_copy(
                src_ref=rows(*block) if src is None else src, dst_ref=rows(*block),
                send_sem=send_sems.at[k], recv_sem=recv_sems.at[k], device_id=to, device_id_type=MESH)

        mine = pltpu.make_async_copy(x_ref, rows(*me), local_sem)
        mine.start()
        first = [copy(0, me, sibling, src=x_ref)]
        first += [copy(1 + j, me, (*chip, mc), src=x_ref) for j, chip in enumerate(chips)]
        for cp in first:
            cp.start()
        passed = [copy(4 + j, (*chip, mc), sibling) for j, chip in enumerate(chips)]
        for j, chip in enumerate(chips):
            copy(1 + j, (*chip, mc), me).wait_recv()
            passed[j].start()
        copy(0, sibling, me).wait_recv()
        for j, chip in enumerate(chips):
            copy(4 + j, (*chip, 1 - mc), me).wait_recv()
        for cp in first + passed:
            cp.wait_send()
        mine.wait()

    return pl.pallas_call(
        body, name=name,
        in_specs=[pl.BlockSpec(memory_space=pl.ANY)], out_specs=pl.BlockSpec(memory_space=pl.ANY),
        out_shape=jax.ShapeDtypeStruct((N_DEV * r, cdim), x.dtype),
        scratch_shapes=[pltpu.SemaphoreType.DMA((7,)), pltpu.SemaphoreType.DMA((7,)), pltpu.SemaphoreType.DMA],
    )(x)


def _exchange_partials(g, small, *, name):
    def body(g_ref, s_ref, out_ref, sout_ref, send_sems, recv_sems, local_sems):
        mx, my, mc = lax.axis_index("x"), lax.axis_index("y"), lax.axis_index("c")
        me = 4 * mx + 2 * my + mc
        mine = [pltpu.make_async_copy(g_ref.at[me], out_ref.at[me], local_sems.at[0]),
                pltpu.make_async_copy(s_ref, sout_ref.at[me], local_sems.at[1])]
        for cp in mine:
            cp.start()
        copies = []
        for mask in range(1, N_DEV):
            px = 1 - mx if mask & 4 else mx
            py = 1 - my if mask & 2 else my
            pc = 1 - mc if mask & 1 else mc
            copies.append(pltpu.make_async_remote_copy(
                src_ref=g_ref.at[4 * px + 2 * py + pc], dst_ref=out_ref.at[me],
                send_sem=send_sems.at[mask - 1], recv_sem=recv_sems.at[mask - 1], device_id=(px, py, pc), device_id_type=MESH))
            copies.append(pltpu.make_async_remote_copy(
                src_ref=s_ref, dst_ref=sout_ref.at[me],
                send_sem=send_sems.at[N_DEV - 2 + mask], recv_sem=recv_sems.at[N_DEV - 2 + mask], device_id=(px, py, pc), device_id_type=MESH))
        for cp in copies:
            cp.start()
        for cp in copies:
            cp.wait()
        for cp in mine:
            cp.wait()

    n_sem = 2 * (N_DEV - 1)
    return pl.pallas_call(
        body, name=name,
        in_specs=[pl.BlockSpec(memory_space=pl.ANY), pl.BlockSpec(memory_space=pl.ANY)],
        out_specs=(pl.BlockSpec(memory_space=pl.ANY), pl.BlockSpec(memory_space=pl.ANY)),
        out_shape=(jax.ShapeDtypeStruct(g.shape, g.dtype), jax.ShapeDtypeStruct((N_DEV,) + small.shape, small.dtype)),
        scratch_shapes=[pltpu.SemaphoreType.DMA((n_sem,)), pltpu.SemaphoreType.DMA((n_sem,)), pltpu.SemaphoreType.DMA((2,))],
    )(g, small)


def _adamw(parts, w, m, v, *, name):
    r, cdim = w.shape
    tr = _pick(r, (PACK_BLOCK_ROWS, 512, 256, 128, 64, 32, 16, 8))
    c1 = 1.0 / (1.0 - ADAM_B1 ** ADAM_STEP)
    c2 = 1.0 / (1.0 - ADAM_B2 ** ADAM_STEP)

    def body(p_ref, w_ref, m_ref, v_ref, g_ref, d_ref, nm_ref, nv_ref):
        g = p_ref[0].astype(F32)
        for dev in range(1, N_DEV):
            g = g + p_ref[dev].astype(F32)
        mn = ADAM_B1 * m_ref[...] + (1.0 - ADAM_B1) * g
        vn = ADAM_B2 * v_ref[...] + (1.0 - ADAM_B2) * (g * g)
        g_ref[...] = g
        nm_ref[...] = mn
        nv_ref[...] = vn
        d_ref[...] = -ADAM_LR * ((mn * c1) / (jnp.sqrt(vn * c2) + ADAM_EPS) + ADAM_WD * w_ref[...])

    blk = pl.BlockSpec((tr, cdim), lambda i: (i, 0))
    shape = jax.ShapeDtypeStruct((r, cdim), F32)
    return pl.pallas_call(
        body, name=name, grid=(r // tr,),
        in_specs=[pl.BlockSpec((N_DEV, tr, cdim), lambda i: (0, i, 0)), blk, blk, blk],
        out_specs=(blk, blk, blk, blk), out_shape=(shape, shape, shape, shape),
        compiler_params=_cparams(("parallel",)),
    )(parts, w, m, v)


def _rows_of(shape):
    n = math.prod(shape)
    assert n % LANES == 0, shape
    rows = n // LANES
    return -(-rows // PACK_ROW_ALIGN) * PACK_ROW_ALIGN


def _layout(shard_shapes):
    out, off = {}, 0
    for name, _ in SHARDED:
        rows = _rows_of(shard_shapes[name])
        out[name] = (off, rows, tuple(shard_shapes[name]))
        off += rows
    return out, -(-off // PACK_BLOCK_ROWS) * PACK_BLOCK_ROWS


def _pack_shards(layout, total, arrays, dtype):
    parts = []
    for name, _ in SHARDED:
        _, rows, _ = layout[name]
        flat = arrays[name].astype(dtype).reshape(-1, LANES)
        parts.append(jnp.pad(flat, ((0, rows - flat.shape[0]), (0, 0))))
    used = sum(p.shape[0] for p in parts)
    if total > used:
        parts.append(jnp.zeros((total - used, LANES), dtype))
    return jnp.concatenate(parts, axis=0)


def _unpack_shard(layout, flat, name):
    off, _, shape = layout[name]
    n = math.prod(shape) // LANES
    return flat[off:off + n].reshape(shape)


def _unpack_full(layout, gathered, name, axis):
    off, _, shape = layout[name]
    n = math.prod(shape) // LANES
    blocks = gathered[:, off:off + n].reshape((N_DEV,) + shape)
    blocks = jnp.moveaxis(blocks, 0, axis)
    return blocks.reshape(shape[:axis] + (N_DEV * shape[axis],) + shape[axis + 1:])


def _pack_full(layout, total, grads):
    parts = []
    for name, axis in SHARDED:
        _, rows, shape = layout[name]
        g = grads[name]
        blocks = g.reshape(shape[:axis] + (N_DEV, shape[axis]) + shape[axis + 1:])
        blocks = jnp.moveaxis(blocks, axis, 0).reshape(N_DEV, -1, LANES)
        parts.append(jnp.pad(blocks, ((0, 0), (0, rows - blocks.shape[1]), (0, 0))))
    used = sum(p.shape[1] for p in parts)
    if total > used:
        parts.append(jnp.zeros((N_DEV, total - used, LANES), F32))
    return jnp.concatenate(parts, axis=1)


def _pad_lanes(a):
    return jnp.pad(a, ((0, 0), (0, LANES - a.shape[1])))


def _pair_layouts(c):
    s = c.shape[0]
    by_pair = c.T.reshape(N_PAIRS, 2, s)
    c_col = jnp.repeat(by_pair.transpose(0, 2, 1), HEAD_DIM, axis=2)
    c_row = jnp.pad(by_pair, ((0, 0), (0, 6), (0, 0)))
    return c_col, c_row


def _key_norm_bound(k):
    norms = jnp.sqrt(jnp.max(jnp.sum(jnp.square(k.astype(F32)).reshape(k.shape[0], N_MIX_HEADS, HEAD_DIM), axis=2), axis=0))
    rows = jnp.pad(norms.reshape(N_PAIRS, 2), ((0, 0), (0, 6)))
    return jnp.broadcast_to(rows[:, :, None], (N_PAIRS, 8, LANES))


def _forward_backward(x, mem, target, wts, small):
    n_a = wts["w_in_a"].shape[0]
    n_b = wts["w_in_b"].shape[0]
    depth = n_a + n_b
    w_kv = wts["w_kv_shared"]
    w_kv_kv = w_kv[:, :2 * MIX_W]
    w_kv_f = _pad_lanes(w_kv[:, 2 * MIX_W:])
    b_f = _pad_lanes(small["b_f"].reshape(1, -1))

    saved = []
    shared = None
    h = x
    for l in range(depth):
        is_a = l < n_a
        if l == n_a:
            hs = _rmsnorm_fwd(h, small["kv_norm_g"], name="kv_norm")
            kv = _mm(hs, w_kv_kv, name="kv_proj", out_dtype=BF16)
            f = _mm(hs, w_kv_f, name="gate_proj")
            c = _gate_fwd(f, b_f, name="gate_cumsum")
            c_col, c_row = _pair_layouts(c[:, :N_MIX_HEADS])
            shared = dict(h=h, hs=hs, kv=kv, f=f, c_col=c_col, c_row=c_row, k_max=_key_norm_bound(kv[:, :MIX_W]))
        hn = _rmsnorm_fwd(h, small["norm1_g"][l], name=f"norm1_{l}")
        memn = _rmsnorm_fwd(mem, small["mem_norm_g"][l], name=f"mem_norm_{l}")
        mkv = _mm(memn, wts["w_mem_kv"][l], name=f"mem_kv_{l}", out_dtype=BF16)
        if is_a:
            proj = _mm(hn, wts["w_in_a"][l], name=f"in_proj_{l}", out_dtype=BF16)
            mix, stat, first = _sb_fwd(proj, name=f"sb_fwd_{l}")
            q_block = 3 * MIX_W // MEM_W
        else:
            proj = _mm(hn, wts["w_in_b"][l - n_a], name=f"in_proj_{l}", out_dtype=BF16)
            mix, stat, first = _fox_fwd(proj, shared["kv"], shared["c_col"], shared["c_row"], shared["k_max"], name=f"fox_fwd_{l}")
            q_block = MIX_W // MEM_W
        mem_out = _mem_fwd(proj, q_block, mkv, name=f"mem_fwd_{l}")
        merged = jnp.concatenate([mix, mem_out], axis=1)
        h_mid = _mm(merged, wts["w_o"][l], name=f"o_proj_{l}", res=h)
        h2n = _rmsnorm_fwd(h_mid, small["norm2_g"][l], name=f"norm2_{l}")
        u, act = _mm(h2n, wts["w_mlp1"][l], name=f"mlp1_{l}", epilogue="relu2")
        h_out = _mm(act, wts["w_mlp2"][l], name=f"mlp2_{l}", res=h_mid)
        saved.append(dict(h=h, hn=hn, memn=memn, mkv=mkv, proj=proj, stat=stat, first=first, merged=merged, h_mid=h_mid, h2n=h2n, u=u, act=act,
                          q_block=q_block))
        h = h_out

    loss, dh, dg_final = _final_loss(h, target, small["final_norm_g"], name="final_loss")

    g_w = {k: [None] * wts[k].shape[0] for k in ("w_in_a", "w_in_b", "w_mem_kv", "w_o", "w_mlp1", "w_mlp2")}
    g_n = {k: [None] * depth for k in ("norm1_g", "mem_norm_g", "norm2_g")}
    dk_sh = dv_sh = dc_sh = dcq_sh = None
    for l in reversed(range(depth)):
        sv = saved[l]
        is_a = l < n_a
        du = _mm(dh, wts["w_mlp2"][l], name=f"d_act_{l}", trans_b=True, epilogue="drelu2", u=sv["u"], out_dtype=BF16)
        g_w["w_mlp2"][l] = _mm_tn(sv["act"], dh, name=f"dw_mlp2_{l}")
        g_w["w_mlp1"][l] = _mm_tn(sv["h2n"], du, name=f"dw_mlp1_{l}")
        dh_mid, g_n["norm2_g"][l] = _mm(du, wts["w_mlp1"][l], name=f"d_h2n_{l}", trans_b=True, norm=(sv["h_mid"], small["norm2_g"][l], dh))
        dmerged = _mm(dh_mid, wts["w_o"][l], name=f"d_merged_{l}", trans_b=True, out_dtype=BF16)
        g_w["w_o"][l] = _mm_tn(sv["merged"], dh_mid, name=f"dw_o_{l}")
        dqm, dmkv = _mem_bwd(sv["proj"], sv["q_block"], sv["mkv"], dmerged, name=f"mem_bwd_{l}")
        if is_a:
            dq, dk, dv = _sb_bwd(sv["proj"], dmerged, sv["stat"], sv["first"], name=f"sb_bwd_{l}")
            dproj = jnp.concatenate([dq, dk.astype(BF16), dv.astype(BF16), dqm], axis=1)
            w_in, key, idx = wts["w_in_a"][l], "w_in_a", l
        else:
            dq, dk, dv, dc, dcq = _fox_bwd(sv["proj"], shared["kv"], shared["c_col"], shared["c_row"], sv["merged"], dmerged, sv["stat"],
                                           sv["first"], name=f"fox_bwd_{l}")
            dk_sh = dk if dk_sh is None else dk_sh + dk
            dv_sh = dv if dv_sh is None else dv_sh + dv
            dc_sh = dc if dc_sh is None else dc_sh + dc
            dcq_sh = dcq if dcq_sh is None else dcq_sh + dcq
            dproj = jnp.concatenate([dq, dqm], axis=1)
            w_in, key, idx = wts["w_in_b"][l - n_a], "w_in_b", l - n_a
        g_w[key][idx] = _mm_tn(sv["hn"], dproj, name=f"dw_in_{l}")
        dh, g_n["norm1_g"][l] = _mm(dproj, w_in, name=f"d_hn_{l}", trans_b=True, norm=(sv["h"], small["norm1_g"][l], dh_mid))
        g_w["w_mem_kv"][l] = _mm_tn(sv["memn"], dmkv, name=f"dw_mem_kv_{l}")
        _, g_n["mem_norm_g"][l] = _mm(dmkv, wts["w_mem_kv"][l], name=f"d_memn_{l}", trans_b=True, norm=(mem, small["mem_norm_g"][l], None))
        if l == n_a:
            s_len = x.shape[0]
            dc_query = jnp.stack([dcq_sh[:, :, 0], dcq_sh[:, :, HEAD_DIM]], axis=-1).transpose(1, 0, 2).reshape(s_len, N_MIX_HEADS)
            dc_tok = _pad_lanes(dc_sh[:, :2, :].reshape(N_MIX_HEADS, s_len).T + dc_query)
            df, db = _gate_bwd(shared["f"], b_f, dc_tok, name="gate_bwd")
            dkv = jnp.concatenate([dk_sh, dv_sh], axis=1)
            dw_kv_kv = _mm_tn(shared["hs"], dkv, name="dw_kv")
            dw_kv_f = _mm_tn(shared["hs"], df, name="dw_gate")
            dhs = _mm(dkv, w_kv_kv, name="d_hs_kv", trans_b=True)
            dh, dg_kv = _mm(df, w_kv_f, name="d_hs_gate", trans_b=True, res=dhs, norm=(shared["h"], small["kv_norm_g"], dh))
            g_kv = jnp.concatenate([dw_kv_kv, dw_kv_f[:, :N_KV_F]], axis=1)

    grads = {k: jnp.stack(v) for k, v in g_w.items()}
    grads["w_kv_shared"] = g_kv
    d = x.shape[1]
    small_g = dict(
        norm1_g=jnp.concatenate(g_n["norm1_g"], axis=0), mem_norm_g=jnp.concatenate(g_n["mem_norm_g"], axis=0),
        norm2_g=jnp.concatenate(g_n["norm2_g"], axis=0), kv_norm_g=dg_kv.reshape(d), b_f=db[0, :N_KV_F], final_norm_g=dg_final.reshape(d))
    return loss, dh, grads, small_g


def _pack_small(vals, d):
    rows = []
    for name in REPLICATED:
        a = vals[name].astype(F32)
        if name == "b_f":
            a = jnp.pad(a, (0, d - a.shape[0]))
        rows.append(a.reshape(-1, d))
    packed = jnp.concatenate(rows, axis=0)
    pad = -packed.shape[0] % 8
    return jnp.pad(packed, ((0, pad), (0, 0)))


def _unpack_small(packed, shapes):
    out, off = {}, 0
    for name in REPLICATED:
        shape = shapes[name]
        if name == "b_f":
            out[name] = packed[off, :shape[0]]
            off += 1
        else:
            n = math.prod(shape) // packed.shape[1]
            out[name] = packed[off:off + n].reshape(shape)
            off += n
    return out


def kernel(x, mem, norm1_g, w_in_a, w_in_b, w_mem_kv, mem_norm_g, w_o, norm2_g, w_mlp1, w_mlp2, kv_norm_g, w_kv_shared, b_f, final_norm_g, loss_target, m_norm1_g, m_w_in_a, m_w_in_b, m_w_mem_kv, m_mem_norm_g, m_w_o, m_norm2_g, m_w_mlp1, m_w_mlp2, m_kv_norm_g, m_w_kv_shared, m_b_f, m_final_norm_g, v_norm1_g, v_w_in_a, v_w_in_b, v_w_mem_kv, v_mem_norm_g, v_w_o, v_norm2_g, v_w_mlp1, v_w_mlp2, v_kv_norm_g, v_w_kv_shared, v_b_f, v_final_norm_g):
    w = dict(norm1_g=norm1_g, w_in_a=w_in_a, w_in_b=w_in_b, w_mem_kv=w_mem_kv, mem_norm_g=mem_norm_g, w_o=w_o, norm2_g=norm2_g,
             w_mlp1=w_mlp1, w_mlp2=w_mlp2, kv_norm_g=kv_norm_g, w_kv_shared=w_kv_shared, b_f=b_f, final_norm_g=final_norm_g)
    m = dict(norm1_g=m_norm1_g, w_in_a=m_w_in_a, w_in_b=m_w_in_b, w_mem_kv=m_w_mem_kv, mem_norm_g=m_mem_norm_g, w_o=m_w_o,
             norm2_g=m_norm2_g, w_mlp1=m_w_mlp1, w_mlp2=m_w_mlp2, kv_norm_g=m_kv_norm_g, w_kv_shared=m_w_kv_shared, b_f=m_b_f,
             final_norm_g=m_final_norm_g)
    v = dict(norm1_g=v_norm1_g, w_in_a=v_w_in_a, w_in_b=v_w_in_b, w_mem_kv=v_w_mem_kv, mem_norm_g=v_mem_norm_g, w_o=v_w_o,
             norm2_g=v_norm2_g, w_mlp1=v_w_mlp1, w_mlp2=v_w_mlp2, kv_norm_g=v_kv_norm_g, w_kv_shared=v_w_kv_shared, b_f=v_b_f,
             final_norm_g=v_final_norm_g)
    d = x.shape[-1]
    layout, total = _layout({name: w[name].shape for name, _ in SHARDED})

    gathered = _all_gather(_pack_shards(layout, total, w, BF16), name="gather_weights").reshape(N_DEV, total, LANES)
    wts = {name: _unpack_full(layout, gathered, name, axis) for name, axis in SHARDED}
    small = {name: w[name] for name in REPLICATED}

    loss, grad_x, grads, small_g = _forward_backward(x[0], mem[0], loss_target[0], wts, small)

    parts, small_parts = _exchange_partials(_pack_full(layout, total, grads).astype(BF16), _pack_small(small_g, d), name="exchange_grads")
    g_flat, d_flat, m_flat, v_flat = _adamw(parts, _pack_shards(layout, total, w, F32), _pack_shards(layout, total, m, F32),
                                            _pack_shards(layout, total, v, F32), name="adamw_sharded")
    gs, ds_, ms, vs = _adamw(small_parts, _pack_small(w, d), _pack_small(m, d), _pack_small(v, d), name="adamw_replicated")

    shapes = {name: w[name].shape for name in REPLICATED}
    out_g, out_d, out_m, out_v = {}, {}, {}, {}
    for flat, small_flat, out in ((g_flat, gs, out_g), (d_flat, ds_, out_d), (m_flat, ms, out_m), (v_flat, vs, out_v)):
        for name, _ in SHARDED:
            out[name] = _unpack_shard(layout, flat, name)
        out.update(_unpack_small(small_flat, shapes))

    loss_total = lax.psum(loss[0, 0], ("x", "y", "c"))
    return (loss_total, grad_x[None], *[out_g[n] for n in WEIGHT_ORDER], *[out_d[n] for n in WEIGHT_ORDER],
            *[out_m[n] for n in WEIGHT_ORDER], *[out_v[n] for n in WEIGHT_ORDER])
```

```python
import functools
import math

import jax
import jax.numpy as jnp
from jax import lax
from jax.experimental import pallas as pl
from jax.experimental.pallas import tpu as pltpu

F32 = jnp.float32
BF16 = jnp.bfloat16

N_DEV = 8
HEAD_DIM = 64
N_MIX_HEADS = 8
N_MEM_HEADS = 4
MIX_W = N_MIX_HEADS * HEAD_DIM
MEM_W = N_MEM_HEADS * HEAD_DIM
N_PAIRS = N_MIX_HEADS // 2
LANES = 128
SB_B = 256
FOX_B = 512
CUM_SUB = 256
SB_CUT = 64.0
FOX_CUT = 45.0
EPS = 1e-6
NEG_INF = -1e30
QK_SCALE = 1.0 / math.sqrt(HEAD_DIM)
LOG2E = 1.4426950408889634
N_KV_F = 8

ADAM_LR = 0.001
ADAM_B1 = 0.9
ADAM_B2 = 0.999
ADAM_EPS = 1e-08
ADAM_WD = 0.01
ADAM_STEP = 10

VMEM_LIMIT = 56 * 1024 * 1024
PACK_ROW_ALIGN = 16
PACK_BLOCK_ROWS = 1024

MESH = pl.DeviceIdType.MESH

SHARDED = (("w_in_a", 2), ("w_in_b", 1), ("w_mem_kv", 1), ("w_o", 2), ("w_mlp1", 2), ("w_mlp2", 1), ("w_kv_shared", 1))
REPLICATED = ("norm1_g", "mem_norm_g", "norm2_g", "kv_norm_g", "b_f", "final_norm_g")
WEIGHT_ORDER = ("norm1_g", "w_in_a", "w_in_b", "w_mem_kv", "mem_norm_g", "w_o", "norm2_g", "w_mlp1", "w_mlp2",
                "kv_norm_g", "w_kv_shared", "b_f", "final_norm_g")


def _cparams(sem=None):
    return pltpu.CompilerParams(dimension_semantics=sem, vmem_limit_bytes=VMEM_LIMIT)


def _resident(shape, index_map):
    return pl.BlockSpec(shape, index_map, pipeline_mode=pl.Buffered(1))


def _pick(n, cands):
    for c in cands:
        if c <= n and n % c == 0:
            return c
    return n


def _dot(a, b, dims):
    return lax.dot_general(a, b, (dims, ((), ())), preferred_element_type=F32)


NN = ((1,), (0,))
NT = ((1,), (1,))
TN = ((0,), (0,))


MM_CHUNK = 512


def _mm(a, b, *, name, trans_b=False, out_dtype=F32, res=None, epilogue=None, u=None, norm=None):
    m, k = a.shape
    n = b.shape[0] if trans_b else b.shape[1]
    tm = _pick(m, (512, 256, 128))
    tn = _pick(n, (MM_CHUNK, 384, 256, 128))
    has_dres = norm is not None and norm[2] is not None

    def body(*refs):
        a_ref, b_ref = refs[0], refs[1]
        pos = 2
        res_ref = u_ref = x_ref = g_ref = dres_ref = None
        if res is not None:
            res_ref = refs[pos]
            pos += 1
        if u is not None:
            u_ref = refs[pos]
            pos += 1
        if norm is not None:
            x_ref, g_ref = refs[pos], refs[pos + 1]
            pos += 2
            if has_dres:
                dres_ref = refs[pos]
                pos += 1
        outs = refs[pos:]
        av = a_ref[...].astype(BF16)
        prods = []
        for c in range(n // tn):
            cols = slice(c * tn, (c + 1) * tn)
            if trans_b:
                acc = _dot(av, b_ref[cols, :].astype(BF16), NT)
            else:
                acc = _dot(av, b_ref[:, cols].astype(BF16), NN)
            if res_ref is not None:
                acc = res_ref[:, cols] + acc
            if norm is not None:
                prods.append(acc)
            elif epilogue == "relu2":
                outs[0][:, cols] = acc.astype(BF16)
                r = jnp.maximum(acc, 0.0)
                outs[1][:, cols] = (r * r).astype(BF16)
            elif epilogue == "drelu2":
                outs[0][:, cols] = (acc * (2.0 * jnp.maximum(u_ref[:, cols], 0.0))).astype(out_dtype)
            else:
                outs[0][:, cols] = acc.astype(out_dtype)
        if norm is not None:
            dyv = prods[0] if len(prods) == 1 else jnp.concatenate(prods, axis=1)
            xf = x_ref[...]
            r = lax.rsqrt(jnp.mean(xf * xf, axis=-1, keepdims=True) + EPS)
            xh = xf * r
            dyg = dyv * g_ref[...]
            dx = r * (dyg - xh * jnp.mean(dyg * xh, axis=-1, keepdims=True))
            outs[0][...] = dx if dres_ref is None else dres_ref[...] + dx

            @pl.when(pl.program_id(0) == 0)
            def _():
                outs[1][...] = jnp.zeros_like(outs[1])

            outs[1][...] += jnp.sum(dyv * xh, axis=0, keepdims=True)

    row = pl.BlockSpec((tm, n), lambda i: (i, 0))
    vec = pl.BlockSpec((1, n), lambda i: (0, 0))
    in_specs = [pl.BlockSpec((tm, k), lambda i: (i, 0)), _resident(b.shape, lambda i: (0, 0))]
    args = [a, b]
    if res is not None:
        in_specs.append(row)
        args.append(res)
    if u is not None:
        in_specs.append(row)
        args.append(u)
    if norm is not None:
        in_specs += [row, vec] + ([row] if has_dres else [])
        args += [norm[0], norm[1].reshape(1, n)] + ([norm[2]] if has_dres else [])
        out_shape = (jax.ShapeDtypeStruct((m, n), F32), jax.ShapeDtypeStruct((1, n), F32))
        out_specs = (row, vec)
    elif epilogue == "relu2":
        out_shape = (jax.ShapeDtypeStruct((m, n), BF16), jax.ShapeDtypeStruct((m, n), BF16))
        out_specs = (row, row)
    else:
        out_shape = (jax.ShapeDtypeStruct((m, n), out_dtype),)
        out_specs = (row,)
    outs = pl.pallas_call(
        body, name=name, grid=(m // tm,), in_specs=in_specs, out_specs=out_specs, out_shape=out_shape,
        compiler_params=_cparams(("arbitrary",) if norm is not None else ("parallel",)),
    )(*args)
    return outs if (epilogue == "relu2" or norm is not None) else outs[0]


def _mm_tn(x, dy, *, name):
    m, k1 = x.shape
    n = dy.shape[1]
    t1 = _pick(k1, (1024, 896, 768, 512, 256, 128))
    tn = _pick(n, (1024, 896, 768, 512, 256, 128))
    tm = _pick(m, (2048, 1024, 512, 256, 128))
    nm = m // tm

    def body(x_ref, dy_ref, o_ref):
        mm = pl.program_id(2)

        @pl.when(mm == 0)
        def _():
            o_ref[...] = jnp.zeros_like(o_ref)

        o_ref[...] += _dot(x_ref[...].astype(BF16), dy_ref[...].astype(BF16), TN)

    return pl.pallas_call(
        body, name=name, grid=(k1 // t1, n // tn, nm),
        in_specs=[pl.BlockSpec((tm, t1), lambda i, j, mm: (mm, i)), pl.BlockSpec((tm, tn), lambda i, j, mm: (mm, j))],
        out_specs=pl.BlockSpec((t1, tn), lambda i, j, mm: (i, j)),
        out_shape=jax.ShapeDtypeStruct((k1, n), F32),
        compiler_params=_cparams(("parallel", "parallel", "arbitrary")),
    )(x, dy)


def _rmsnorm_fwd(x, g, *, name):
    s, d = x.shape
    tm = _pick(s, (512, 256, 128))

    def body(x_ref, g_ref, o_ref):
        xf = x_ref[...]
        r = lax.rsqrt(jnp.mean(xf * xf, axis=-1, keepdims=True) + EPS)
        o_ref[...] = (xf * r * g_ref[...]).astype(BF16)

    return pl.pallas_call(
        body, name=name, grid=(s // tm,),
        in_specs=[pl.BlockSpec((tm, d), lambda i: (i, 0)), pl.BlockSpec((1, d), lambda i: (0, 0))],
        out_specs=pl.BlockSpec((tm, d), lambda i: (i, 0)),
        out_shape=jax.ShapeDtypeStruct((s, d), BF16),
        compiler_params=_cparams(("parallel",)),
    )(x, g.reshape(1, d))


def _final_loss(h, target, g, *, name):
    s, d = h.shape
    tm = _pick(s, (256, 128))

    def body(h_ref, t_ref, g_ref, loss_ref, dh_ref, dg_ref):
        i = pl.program_id(0)
        xf = h_ref[...]
        gv = g_ref[...]
        r = lax.rsqrt(jnp.mean(xf * xf, axis=-1, keepdims=True) + EPS)
        xh = xf * r
        err = xh * gv - t_ref[...]
        part = 0.5 * jnp.sum(jnp.mean(err * err, axis=-1, keepdims=True), axis=0, keepdims=True)
        dyv = err * (1.0 / d)
        dyg = dyv * gv
        dh_ref[...] = r * (dyg - xh * jnp.mean(dyg * xh, axis=-1, keepdims=True))

        @pl.when(i == 0)
        def _():
            dg_ref[...] = jnp.zeros_like(dg_ref)
            loss_ref[...] = jnp.zeros_like(loss_ref)

        dg_ref[...] += jnp.sum(dyv * xh, axis=0, keepdims=True)
        loss_ref[...] += jnp.broadcast_to(part, loss_ref.shape)

    row = pl.BlockSpec((tm, d), lambda i: (i, 0))
    vec = pl.BlockSpec((1, d), lambda i: (0, 0))
    return pl.pallas_call(
        body, name=name, grid=(s // tm,), in_specs=[row, row, vec],
        out_specs=(pl.BlockSpec((1, LANES), lambda i: (0, 0)), row, vec),
        out_shape=(jax.ShapeDtypeStruct((1, LANES), F32), jax.ShapeDtypeStruct((s, d), F32), jax.ShapeDtypeStruct((1, d), F32)),
        compiler_params=_cparams(("arbitrary",)),
    )(h, target, g.reshape(1, d))


def _head_lanes(rows):
    lane = lax.broadcasted_iota(jnp.int32, (rows, LANES), 1)
    return [lane < HEAD_DIM, lane >= HEAD_DIM]


def _tri(b, cmp):
    row = lax.broadcasted_iota(jnp.int32, (b, b), 0)
    col = lax.broadcasted_iota(jnp.int32, (b, b), 1)
    return cmp(row, col)


def _twice(mask):
    return jnp.concatenate([mask, mask], axis=0)


def _stack_heads(x, heads):
    zero = jnp.zeros_like(x)
    return jnp.concatenate([jnp.where(heads[0], x, zero), jnp.where(heads[1], x, zero)], axis=0)


def _unstack_heads(x2, heads):
    b = x2.shape[0] // 2
    return jnp.where(heads[0], x2[:b], x2[b:])


def _stack_stat(stat):
    return jnp.concatenate([stat[:, 0:1], stat[:, HEAD_DIM:HEAD_DIM + 1]], axis=0)


def _unstack_stat(col, heads):
    b = col.shape[0] // 2
    return jnp.where(heads[0], jnp.broadcast_to(col[:b], (b, LANES)), jnp.broadcast_to(col[b:], (b, LANES)))


def _tri_dot(x, tri_bf16):
    return _dot(x.astype(BF16), tri_bf16, NN)


def _prefix_sums(x, tri_bf16, inclusive):
    sub = tri_bf16.shape[0]
    outs, carry = [], None
    for c in range(x.shape[1] // sub):
        xs = x[:, c * sub:(c + 1) * sub]
        loc = _tri_dot(xs, tri_bf16)
        outs.append(loc if carry is None else loc + carry)
        tot = loc[:, sub - 1:sub] if inclusive else loc[:, sub - 1:sub] + xs[:, sub - 1:sub]
        carry = tot if carry is None else carry + tot
    return (outs[0] if len(outs) == 1 else jnp.concatenate(outs, axis=1)), carry


def _suffix_sums(x, tri_bf16):
    sub = tri_bf16.shape[0]
    n = x.shape[1] // sub
    outs, carry = [None] * n, None
    for c in reversed(range(n)):
        xs = x[:, c * sub:(c + 1) * sub]
        loc = _tri_dot(xs, tri_bf16)
        outs[c] = loc if carry is None else loc + carry
        tot = loc[:, 0:1] + xs[:, 0:1].astype(BF16).astype(F32)
        carry = tot if carry is None else carry + tot
    return (outs[0] if n == 1 else jnp.concatenate(outs, axis=1)), carry


def _softplus2(z):
    z2 = z * LOG2E
    neg_abs = lax.bitcast_convert_type(lax.bitcast_convert_type(z2, jnp.uint32) | jnp.uint32(0x80000000), F32)
    return z2, jnp.maximum(z2, 0.0) + jnp.log2(1.0 + jnp.exp2(neg_abs))


def _block_rows(j, b):
    return pl.ds(pl.multiple_of(j * b, b), b)


def _sb_fwd(proj, *, name):
    s = proj.shape[0]
    b = _pick(s, (SB_B, 128))

    def body(q_ref, k_ref, v_ref, o_ref, tot_ref, first_ref, acc_ref):
        i = pl.program_id(1)
        heads = _head_lanes(b)
        suffix = _tri(min(b, CUM_SUB), lambda r, c: r > c).astype(BF16)
        strict = _twice(_tri(b, lambda r, c: c < r))
        q2 = _stack_heads(q_ref[...] * QK_SCALE, heads)

        def tile(j, a, masked):
            rows = _block_rows(j, b)
            z2, sp = _softplus2(_dot(q2, k_ref[rows, :], NT))
            if masked:
                sp = jnp.where(strict, sp, 0.0)
            rsum, total = _suffix_sums(sp, suffix)
            w = jnp.exp2((z2 - sp) - (a + rsum))
            if masked:
                w = jnp.where(strict, w, 0.0)
            acc_ref[...] += _dot(w.astype(BF16), v_ref[rows, :], NN)
            return a + total

        acc_ref[...] = jnp.zeros_like(acc_ref)
        a = tile(i, jnp.zeros((2 * b, 1), F32), True)

        def more(c):
            return jnp.logical_and(c[0] < i, c[2] < SB_CUT)

        def step(c):
            a = tile(i - 1 - c[0], c[1], False)
            return c[0] + 1, a, jnp.min(a)

        done, a, _ = lax.while_loop(more, step, (jnp.int32(0), a, jnp.min(a)))
        o_ref[...] = _unstack_heads(acc_ref[...], heads).astype(BF16)
        tot_ref[0] = _unstack_stat(a, heads)
        first_ref[pl.program_id(0), i] = i - done

    qblk = pl.BlockSpec((b, LANES), lambda p, i: (i, p))
    return pl.pallas_call(
        body, name=name, grid=(N_PAIRS, s // b),
        in_specs=[qblk, _resident((s, LANES), lambda p, i: (0, N_PAIRS + p)), _resident((s, LANES), lambda p, i: (0, 2 * N_PAIRS + p))],
        out_specs=(qblk, pl.BlockSpec((1, b, LANES), lambda p, i: (p, i, 0)), pl.BlockSpec(memory_space=pltpu.SMEM)),
        out_shape=(jax.ShapeDtypeStruct((s, MIX_W), BF16), jax.ShapeDtypeStruct((N_PAIRS, s, LANES), F32),
                   jax.ShapeDtypeStruct((N_PAIRS, s // b), jnp.int32)),
        scratch_shapes=[pltpu.VMEM((2 * b, LANES), F32)],
        compiler_params=_cparams(("arbitrary", "arbitrary")),
    )(proj, proj, proj)


def _sb_bwd(proj, dmerged, tot, first, *, name):
    s = proj.shape[0]
    b = _pick(s, (SB_B, 128))

    def body(q_ref, k_ref, v_ref, do_ref, tot_ref, first_ref, dq_ref, dk_ref, dv_ref, dq_acc):
        i = pl.program_id(1)

        @pl.when(i == 0)
        def _():
            dk_ref[...] = jnp.zeros_like(dk_ref)
            dv_ref[...] = jnp.zeros_like(dv_ref)

        heads = _head_lanes(b)
        incl = _tri(min(b, CUM_SUB), lambda r, c: r <= c).astype(BF16)
        excl = _tri(min(b, CUM_SUB), lambda r, c: r < c).astype(BF16)
        strict = _twice(_tri(b, lambda r, c: c < r))
        q2 = _stack_heads(q_ref[...] * QK_SCALE, heads)
        do2 = _stack_heads(do_ref[...], heads)
        tot2 = _stack_stat(tot_ref[0])

        def tile(j, pre, gpre, masked):
            rows = _block_rows(j, b)
            kb = k_ref[rows, :]
            z2, sp = _softplus2(_dot(q2, kb, NT))
            oms = jnp.exp2(-sp)
            if masked:
                sp = jnp.where(strict, sp, 0.0)
            pin, ptot = _prefix_sums(sp, incl, True)
            w = jnp.exp2((z2 - sp) + (pin + (pre - tot2)))
            if masked:
                w = jnp.where(strict, w, 0.0)
            gw = _dot(do2, v_ref[rows, :], NT) * w
            gex, gtot = _prefix_sums(gw, excl, False)
            dz = gw * oms - (1.0 - oms) * (gpre + gex)
            if masked:
                dz = jnp.where(strict, dz, 0.0)
            dzb = dz.astype(BF16)
            dq_acc[...] += _dot(dzb, kb, NN)
            dk_ref[rows, :] += _dot(dzb, q2, TN)
            dv_ref[rows, :] += _dot(w.astype(BF16), do2, TN)
            return pre + ptot, gpre + gtot

        dq_acc[...] = jnp.zeros_like(dq_acc)
        zero = jnp.zeros((2 * b, 1), F32)
        pre, gpre = lax.fori_loop(first_ref[pl.program_id(0), i], i, lambda j, c: tile(j, c[0], c[1], False), (zero, zero))
        tile(i, pre, gpre, True)
        dq_ref[...] = (_unstack_heads(dq_acc[...], heads) * QK_SCALE).astype(BF16)

    qblk = pl.BlockSpec((b, LANES), lambda p, i: (i, p))
    full = _resident((s, LANES), lambda p, i: (0, p))
    return pl.pallas_call(
        body, name=name, grid=(N_PAIRS, s // b),
        in_specs=[qblk, _resident((s, LANES), lambda p, i: (0, N_PAIRS + p)), _resident((s, LANES), lambda p, i: (0, 2 * N_PAIRS + p)),
                  qblk, pl.BlockSpec((1, b, LANES), lambda p, i: (p, i, 0)), pl.BlockSpec(memory_space=pltpu.SMEM)],
        out_specs=(qblk, full, full),
        out_shape=(jax.ShapeDtypeStruct((s, MIX_W), BF16), jax.ShapeDtypeStruct((s, MIX_W), F32), jax.ShapeDtypeStruct((s, MIX_W), F32)),
        scratch_shapes=[pltpu.VMEM((2 * b, LANES), F32)],
        compiler_params=_cparams(("parallel", "arbitrary")),
    )(proj, proj, proj, dmerged, tot, first)


def _fox_fwd(proj, kv, c_col, c_row, k_max, *, name):
    s = proj.shape[0]
    b = _pick(s, (FOX_B, 128))

    def body(q_ref, k_ref, v_ref, cc_ref, cr_ref, km_ref, o_ref, o32_ref, lse_ref, first_ref, acc_ref):
        i = pl.program_id(1)
        heads = _head_lanes(b)
        causal = _twice(_tri(b, lambda r, c: c <= r))
        top = lax.broadcasted_iota(jnp.int32, (2 * b, b), 0) < b
        q2 = _stack_heads(q_ref[...] * QK_SCALE, heads)
        c_t = _stack_stat(cc_ref[0])
        qf = q2.astype(F32)
        kmv = km_ref[0]
        z_max = jnp.sqrt(jnp.sum(qf * qf, axis=1, keepdims=True)) * jnp.where(top[:, 0:1], kmv[0:1, 0:1], kmv[1:2, 0:1]) * 1.001

        def tile(j, m, l, masked):
            rows = _block_rows(j, b)
            gate = c_t - jnp.where(top, cr_ref[0, 0:1, rows], cr_ref[0, 1:2, rows])
            sc = _dot(q2, k_ref[rows, :], NT) + gate
            if masked:
                sc = jnp.where(causal, sc, NEG_INF)
            m_new = jnp.maximum(m, jnp.max(sc, axis=1, keepdims=True))
            p = jnp.exp(sc - m_new)
            alpha = jnp.exp(m - m_new)
            acc_ref[...] = alpha * acc_ref[...] + _dot(p.astype(BF16), v_ref[rows, :], NN)
            return m_new, alpha * l + jnp.sum(p, axis=1, keepdims=True), jnp.max(z_max + gate[:, 0:1] - m_new)

        acc_ref[...] = jnp.zeros_like(acc_ref)
        m, l, slack = tile(i, jnp.full((2 * b, 1), NEG_INF, F32), jnp.zeros((2 * b, 1), F32), True)

        def more(c):
            return jnp.logical_and(c[0] < i, c[3] > -FOX_CUT)

        def step(c):
            m, l, slack = tile(i - 1 - c[0], c[1], c[2], False)
            return c[0] + 1, m, l, slack

        done, m, l, _ = lax.while_loop(more, step, (jnp.int32(0), m, l, slack))
        out = _unstack_heads(acc_ref[...] * (1.0 / l), heads)
        o_ref[...] = out.astype(BF16)
        o32_ref[...] = out
        lse_ref[0] = _unstack_stat(m + jnp.log(l), heads)
        first_ref[pl.program_id(0), i] = i - done

    qblk = pl.BlockSpec((b, LANES), lambda p, i: (i, p))
    stat = pl.BlockSpec((1, b, LANES), lambda p, i: (p, i, 0))
    return pl.pallas_call(
        body, name=name, grid=(N_PAIRS, s // b),
        in_specs=[qblk, _resident((s, LANES), lambda p, i: (0, p)), _resident((s, LANES), lambda p, i: (0, N_PAIRS + p)),
                  stat, _resident((1, 8, s), lambda p, i: (p, 0, 0)), pl.BlockSpec((1, 8, LANES), lambda p, i: (p, 0, 0))],
        out_specs=(qblk, qblk, stat, pl.BlockSpec(memory_space=pltpu.SMEM)),
        out_shape=(jax.ShapeDtypeStruct((s, MIX_W), BF16), jax.ShapeDtypeStruct((s, MIX_W), F32), jax.ShapeDtypeStruct((N_PAIRS, s, LANES), F32),
                   jax.ShapeDtypeStruct((N_PAIRS, s // b), jnp.int32)),
        scratch_shapes=[pltpu.VMEM((2 * b, LANES), F32)],
        compiler_params=_cparams(("arbitrary", "arbitrary")),
    )(proj, kv, kv, c_col, c_row, k_max)


def _fox_bwd(proj, kv, c_col, c_row, out32, dmerged, lse, first, *, name):
    s = proj.shape[0]
    b = _pick(s, (FOX_B, 128))

    def body(q_ref, k_ref, v_ref, cc_ref, cr_ref, o_ref, do_ref, lse_ref, first_ref, dq_ref, dk_ref, dv_ref, dc_ref, dcq_ref, dq_acc):
        i = pl.program_id(1)

        @pl.when(i == 0)
        def _():
            dk_ref[...] = jnp.zeros_like(dk_ref)
            dv_ref[...] = jnp.zeros_like(dv_ref)
            dc_ref[...] = jnp.zeros_like(dc_ref)

        heads = _head_lanes(b)
        causal = _twice(_tri(b, lambda r, c: c <= r))
        top = lax.broadcasted_iota(jnp.int32, (2 * b, b), 0) < b
        q2 = _stack_heads(q_ref[...] * QK_SCALE, heads)
        dov = do_ref[...]
        do2 = _stack_heads(dov, heads)
        prod = dov.astype(F32) * o_ref[...]
        delta = jnp.concatenate([jnp.sum(jnp.where(heads[hh], prod, 0.0), axis=1, keepdims=True) for hh in range(2)], axis=0)
        c_t = _stack_stat(cc_ref[0])
        lse_t = _stack_stat(lse_ref[0])

        def tile(j, rsum, masked):
            rows = _block_rows(j, b)
            kb = k_ref[rows, :]
            c_s = jnp.where(top, cr_ref[0, 0:1, rows], cr_ref[0, 1:2, rows])
            sc = _dot(q2, kb, NT) + (c_t - c_s)
            p = jnp.exp(sc - lse_t)
            if masked:
                p = jnp.where(causal, p, 0.0)
            ds = p * (_dot(do2, v_ref[rows, :], NT) - delta)
            dsb = ds.astype(BF16)
            dq_acc[...] += _dot(dsb, kb, NN)
            dk_ref[rows, :] += _dot(dsb, q2, TN)
            dv_ref[rows, :] += _dot(p.astype(BF16), do2, TN)
            dc_ref[0, 0:1, rows] -= jnp.sum(ds[:b], axis=0, keepdims=True)
            dc_ref[0, 1:2, rows] -= jnp.sum(ds[b:], axis=0, keepdims=True)
            return rsum + jnp.sum(ds, axis=1, keepdims=True)

        dq_acc[...] = jnp.zeros_like(dq_acc)
        rsum = lax.fori_loop(first_ref[pl.program_id(0), i], i, lambda j, r: tile(j, r, False), jnp.zeros((2 * b, 1), F32))
        rsum = tile(i, rsum, True)
        dq_ref[...] = (_unstack_heads(dq_acc[...], heads) * QK_SCALE).astype(BF16)
        dcq_ref[0] = _unstack_stat(rsum, heads)

    qblk = pl.BlockSpec((b, LANES), lambda p, i: (i, p))
    stat = pl.BlockSpec((1, b, LANES), lambda p, i: (p, i, 0))
    crow = _resident((1, 8, s), lambda p, i: (p, 0, 0))
    full = _resident((s, LANES), lambda p, i: (0, p))
    return pl.pallas_call(
        body, name=name, grid=(N_PAIRS, s // b),
        in_specs=[qblk, full, _resident((s, LANES), lambda p, i: (0, N_PAIRS + p)), stat, crow, qblk, qblk, stat,
                  pl.BlockSpec(memory_space=pltpu.SMEM)],
        out_specs=(qblk, full, full, crow, stat),
        out_shape=(jax.ShapeDtypeStruct((s, MIX_W), BF16), jax.ShapeDtypeStruct((s, MIX_W), F32), jax.ShapeDtypeStruct((s, MIX_W), F32),
                   jax.ShapeDtypeStruct((N_PAIRS, 8, s), F32), jax.ShapeDtypeStruct((N_PAIRS, s, LANES), F32)),
        scratch_shapes=[pltpu.VMEM((2 * b, LANES), F32)],
        compiler_params=_cparams(("parallel", "arbitrary")),
    )(proj, kv, kv, c_col, c_row, out32, dmerged, lse, first)


MEM_TQ = 1024


def _mem_fwd(proj, q_col_block, mkv, *, name):
    s = proj.shape[0]
    tq = _pick(s, (MEM_TQ, 128))
    n_mem = mkv.shape[0]

    def body(q_ref, mkv_ref, o_ref):
        heads = _head_lanes(tq)
        for pp in range(MEM_W // LANES):
            cols = slice(pp * LANES, (pp + 1) * LANES)
            qv = q_ref[:, cols] * QK_SCALE
            mk = mkv_ref[:, pp * LANES:(pp + 1) * LANES]
            mv = mkv_ref[:, MEM_W + pp * LANES:MEM_W + (pp + 1) * LANES]
            o_sel = None
            for hh in range(2):
                qm = jnp.where(heads[hh], qv, jnp.zeros_like(qv))
                sc = _dot(qm, mk, NT)
                p = jnp.exp(sc - jnp.max(sc, axis=1, keepdims=True))
                p = p / jnp.sum(p, axis=1, keepdims=True)
                out = _dot(p.astype(BF16), mv, NN)
                o_sel = out if hh == 0 else jnp.where(heads[0], o_sel, out)
            o_ref[:, cols] = o_sel.astype(BF16)

    return pl.pallas_call(
        body, name=name, grid=(s // tq,),
        in_specs=[pl.BlockSpec((tq, MEM_W), lambda i: (i, q_col_block)), pl.BlockSpec((n_mem, 2 * MEM_W), lambda i: (0, 0))],
        out_specs=pl.BlockSpec((tq, MEM_W), lambda i: (i, 0)),
        out_shape=jax.ShapeDtypeStruct((s, MEM_W), BF16),
        compiler_params=_cparams(("parallel",)),
    )(proj, mkv)


def _mem_bwd(proj, q_col_block, mkv, dmerged, *, name):
    s = proj.shape[0]
    tq = _pick(s, (MEM_TQ, 128))
    n_mem = mkv.shape[0]

    def body(q_ref, mkv_ref, do_ref, dq_ref, dmkv_ref):
        i = pl.program_id(0)

        @pl.when(i == 0)
        def _():
            dmkv_ref[...] = jnp.zeros_like(dmkv_ref)

        heads = _head_lanes(tq)
        for pp in range(MEM_W // LANES):
            cols = slice(pp * LANES, (pp + 1) * LANES)
            vcols = slice(MEM_W + pp * LANES, MEM_W + (pp + 1) * LANES)
            qv = q_ref[:, cols] * QK_SCALE
            dov = do_ref[:, cols]
            mk = mkv_ref[:, cols]
            mv = mkv_ref[:, vcols]
            dq_sel = None
            for hh in range(2):
                qm = jnp.where(heads[hh], qv, jnp.zeros_like(qv))
                dom = jnp.where(heads[hh], dov, jnp.zeros_like(dov))
                sc = _dot(qm, mk, NT)
                p = jnp.exp(sc - jnp.max(sc, axis=1, keepdims=True))
                p = p / jnp.sum(p, axis=1, keepdims=True)
                dp = _dot(dom, mv, NT)
                ds = p * (dp - jnp.sum(p * dp, axis=1, keepdims=True))
                dsb = ds.astype(BF16)
                dq = _dot(dsb, mk, NN)
                dmkv_ref[:, cols] += _dot(dsb, qm, TN)
                dmkv_ref[:, vcols] += _dot(p.astype(BF16), dom, TN)
                dq_sel = dq if hh == 0 else jnp.where(heads[0], dq_sel, dq)
            dq_ref[:, cols] = (dq_sel * QK_SCALE).astype(BF16)

    return pl.pallas_call(
        body, name=name, grid=(s // tq,),
        in_specs=[pl.BlockSpec((tq, MEM_W), lambda i: (i, q_col_block)), pl.BlockSpec((n_mem, 2 * MEM_W), lambda i: (0, 0)),
                  pl.BlockSpec((tq, MEM_W), lambda i: (i, MIX_W // MEM_W))],
        out_specs=(pl.BlockSpec((tq, MEM_W), lambda i: (i, 0)), pl.BlockSpec((n_mem, 2 * MEM_W), lambda i: (0, 0))),
        out_shape=(jax.ShapeDtypeStruct((s, MEM_W), BF16), jax.ShapeDtypeStruct((n_mem, 2 * MEM_W), F32)),
        compiler_params=_cparams(("arbitrary",)),
    )(proj, mkv, dmerged)


GATE_TB = 256


def _split3_dot(tri_bf16, x):
    x1 = x.astype(BF16)
    r1 = x - x1.astype(F32)
    x2 = r1.astype(BF16)
    x3 = (r1 - x2.astype(F32)).astype(BF16)
    return _dot(tri_bf16, x1, NN) + _dot(tri_bf16, x2, NN) + _dot(tri_bf16, x3, NN)


def _gate_fwd(f, b, *, name):
    s = f.shape[0]
    tb = _pick(s, (GATE_TB, 128))

    def body(f_ref, b_ref, c_ref, carry_ref):
        i = pl.program_id(0)

        @pl.when(i == 0)
        def _():
            carry_ref[...] = jnp.zeros_like(carry_ref)

        x = f_ref[...] + b_ref[...]
        lf = jnp.minimum(x, 0.0) - jnp.log1p(jnp.exp(-jnp.abs(x)))
        row = lax.broadcasted_iota(jnp.int32, (tb, tb), 0)
        col = lax.broadcasted_iota(jnp.int32, (tb, tb), 1)
        lower = (col <= row).astype(BF16)
        c = carry_ref[...] + _split3_dot(lower, lf)
        c_ref[...] = c
        carry_ref[...] = c[tb - 1:tb, :]

    return pl.pallas_call(
        body, name=name, grid=(s // tb,),
        in_specs=[pl.BlockSpec((tb, LANES), lambda i: (i, 0)), pl.BlockSpec((1, LANES), lambda i: (0, 0))],
        out_specs=pl.BlockSpec((tb, LANES), lambda i: (i, 0)),
        out_shape=jax.ShapeDtypeStruct((s, LANES), F32),
        scratch_shapes=[pltpu.VMEM((1, LANES), F32)],
        compiler_params=_cparams(("arbitrary",)),
    )(f, b)


def _gate_bwd(f, b, dc, *, name):
    s = f.shape[0]
    tb = _pick(s, (GATE_TB, 128))
    nb = s // tb

    def body(f_ref, b_ref, dc_ref, df_ref, db_ref, carry_ref):
        i = pl.program_id(0)

        @pl.when(i == 0)
        def _():
            carry_ref[...] = jnp.zeros_like(carry_ref)
            db_ref[...] = jnp.zeros_like(db_ref)

        row = lax.broadcasted_iota(jnp.int32, (tb, tb), 0)
        col = lax.broadcasted_iota(jnp.int32, (tb, tb), 1)
        upper = (col >= row).astype(BF16)
        dlf = carry_ref[...] + _split3_dot(upper, dc_ref[...])
        carry_ref[...] = dlf[0:1, :]
        x = f_ref[...] + b_ref[...]
        e = jnp.exp(-jnp.abs(x))
        one_minus_sig = jnp.where(x >= 0.0, e, 1.0) / (1.0 + e)
        df = dlf * one_minus_sig
        df_ref[...] = df
        db_ref[...] += jnp.sum(df, axis=0, keepdims=True)

    rev = pl.BlockSpec((tb, LANES), lambda i: (nb - 1 - i, 0))
    vec = pl.BlockSpec((1, LANES), lambda i: (0, 0))
    return pl.pallas_call(
        body, name=name, grid=(nb,), in_specs=[rev, vec, rev], out_specs=(rev, vec),
        out_shape=(jax.ShapeDtypeStruct((s, LANES), F32), jax.ShapeDtypeStruct((1, LANES), F32)),
        scratch_shapes=[pltpu.VMEM((1, LANES), F32)],
        compiler_params=_cparams(("arbitrary",)),
    )(f, b, dc)


def _all_gather(x, *, name):
    r, cdim = x.shape

    def body(x_ref, out_ref, send_sems, recv_sems, local_sem):
        mx, my, mc = lax.axis_index("x"), lax.axis_index("y"), lax.axis_index("c")
        me, sibling = (mx, my, mc), (mx, my, 1 - mc)
        chips = [(1 - mx, my), (mx, 1 - my), (1 - mx, 1 - my)]

        def rows(px, py, pc):
            return out_ref.at[pl.ds((4 * px + 2 * py + pc) * r, r), :]

        def copy(k, block, to, src=None):
            return pltpu.make_async_remote_copy(
                src_ref=rows(*block) if src is None else src, dst_ref=rows(*block),
                send_sem=send_sems.at[k], recv_sem=recv_sems.at[k], device_id=to, device_id_type=MESH)

        mine = pltpu.make_async_copy(x_ref, rows(*me), local_sem)
        mine.start()
        first = [copy(0, me, sibling, src=x_ref)]
        first += [copy(1 + j, me, (*chip, mc), src=x_ref) for j, chip in enumerate(chips)]
        for cp in first:
            cp.start()
        passed = [copy(4 + j, (*chip, mc), sibling) for j, chip in enumerate(chips)]
        for j, chip in enumerate(chips):
            copy(1 + j, (*chip, mc), me).wait_recv()
            passed[j].start()
        copy(0, sibling, me).wait_recv()
        for j, chip in enumerate(chips):
            copy(4 + j, (*chip, 1 - mc), me).wait_recv()
        for cp in first + passed:
            cp.wait_send()
        mine.wait()

    return pl.pallas_call(
        body, name=name,
        in_specs=[pl.BlockSpec(memory_space=pl.ANY)], out_specs=pl.BlockSpec(memory_space=pl.ANY),
        out_shape=jax.ShapeDtypeStruct((N_DEV * r, cdim), x.dtype),
        scratch_shapes=[pltpu.SemaphoreType.DMA((7,)), pltpu.SemaphoreType.DMA((7,)), pltpu.SemaphoreType.DMA],
    )(x)


N_CHIPS = N_DEV // 2


def _sibling_exchange(g, small, *, name):
    def body(g_ref, s_ref, out_ref, sout_ref, send_sems, recv_sems, local_sem):
        mx, my, mc = lax.axis_index("x"), lax.axis_index("y"), lax.axis_index("c")
        me = 4 * mx + 2 * my + mc
        sibling = (mx, my, 1 - mc)
        mine = pltpu.make_async_copy(s_ref, sout_ref.at[me], local_sem)
        mine.start()
        copies = []
        for k in range(N_CHIPS):
            copies.append(pltpu.make_async_remote_copy(
                src_ref=g_ref.at[2 * k + (1 - mc)], dst_ref=out_ref.at[k],
                send_sem=send_sems.at[k], recv_sem=recv_sems.at[k], device_id=sibling, device_id_type=MESH))
        for mask in range(1, N_DEV):
            px = 1 - mx if mask & 4 else mx
            py = 1 - my if mask & 2 else my
            pc = 1 - mc if mask & 1 else mc
            copies.append(pltpu.make_async_remote_copy(
                src_ref=s_ref, dst_ref=sout_ref.at[me],
                send_sem=send_sems.at[N_CHIPS - 1 + mask], recv_sem=recv_sems.at[N_CHIPS - 1 + mask], device_id=(px, py, pc), device_id_type=MESH))
        for cp in copies:
            cp.start()
        for cp in copies:
            cp.wait()
        mine.wait()

    n_sem = N_CHIPS + N_DEV - 1
    return pl.pallas_call(
        body, name=name,
        in_specs=[pl.BlockSpec(memory_space=pl.ANY), pl.BlockSpec(memory_space=pl.ANY)],
        out_specs=(pl.BlockSpec(memory_space=pl.ANY), pl.BlockSpec(memory_space=pl.ANY)),
        out_shape=(jax.ShapeDtypeStruct((N_CHIPS,) + g.shape[1:], g.dtype), jax.ShapeDtypeStruct((N_DEV,) + small.shape, small.dtype)),
        scratch_shapes=[pltpu.SemaphoreType.DMA((n_sem,)), pltpu.SemaphoreType.DMA((n_sem,)), pltpu.SemaphoreType.DMA],
    )(g, small)


def _pair_sum(g, theirs, *, name):
    _, r, cdim = g.shape
    tr = _pick(r, (PACK_BLOCK_ROWS, 512, 256, 128, 64, 32, 16))

    def body(g0_ref, g1_ref, t_ref, o_ref):
        south = lax.axis_index("c") == 0
        mine = jnp.where(south, g0_ref[...].astype(F32), g1_ref[...].astype(F32))
        o_ref[...] = (mine + t_ref[...].astype(F32)).astype(o_ref.dtype)

    return pl.pallas_call(
        body, name=name, grid=(N_CHIPS, r // tr),
        in_specs=[pl.BlockSpec((1, tr, cdim), lambda k, i: (2 * k, i, 0)), pl.BlockSpec((1, tr, cdim), lambda k, i: (2 * k + 1, i, 0)),
                  pl.BlockSpec((1, tr, cdim), lambda k, i: (k, i, 0))],
        out_specs=pl.BlockSpec((1, tr, cdim), lambda k, i: (k, i, 0)),
        out_shape=jax.ShapeDtypeStruct((N_CHIPS, r, cdim), g.dtype),
        compiler_params=_cparams(("parallel", "parallel")),
    )(g, g, theirs)


def _chip_exchange(sums, *, name):
    def body(s_ref, out_ref, send_sems, recv_sems, local_sem):
        mx, my, mc = lax.axis_index("x"), lax.axis_index("y"), lax.axis_index("c")
        chip = 2 * mx + my
        mine = pltpu.make_async_copy(s_ref.at[chip], out_ref.at[chip], local_sem)
        mine.start()
        copies = []
        for mask in range(1, N_CHIPS):
            px = 1 - mx if mask & 2 else mx
            py = 1 - my if mask & 1 else my
            copies.append(pltpu.make_async_remote_copy(
                src_ref=s_ref.at[2 * px + py], dst_ref=out_ref.at[chip],
                send_sem=send_sems.at[mask - 1], recv_sem=recv_sems.at[mask - 1], device_id=(px, py, mc), device_id_type=MESH))
        for cp in copies:
            cp.start()
        for cp in copies:
            cp.wait()
        mine.wait()

    return pl.pallas_call(
        body, name=name,
        in_specs=[pl.BlockSpec(memory_space=pl.ANY)], out_specs=pl.BlockSpec(memory_space=pl.ANY),
        out_shape=jax.ShapeDtypeStruct(sums.shape, sums.dtype),
        scratch_shapes=[pltpu.SemaphoreType.DMA((N_CHIPS - 1,)), pltpu.SemaphoreType.DMA((N_CHIPS - 1,)), pltpu.SemaphoreType.DMA],
    )(sums)


def _adamw(parts, w, m, v, *, name):
    r, cdim = w.shape
    n_parts = parts.shape[0]
    tr = _pick(r, (PACK_BLOCK_ROWS, 512, 256, 128, 64, 32, 16, 8))
    c1 = 1.0 / (1.0 - ADAM_B1 ** ADAM_STEP)
    c2 = 1.0 / (1.0 - ADAM_B2 ** ADAM_STEP)

    def body(p_ref, w_ref, m_ref, v_ref, g_ref, d_ref, nm_ref, nv_ref):
        g = p_ref[0].astype(F32)
        for part in range(1, n_parts):
            g = g + p_ref[part].astype(F32)
        mn = ADAM_B1 * m_ref[...] + (1.0 - ADAM_B1) * g
        vn = ADAM_B2 * v_ref[...] + (1.0 - ADAM_B2) * (g * g)
        g_ref[...] = g
        nm_ref[...] = mn
        nv_ref[...] = vn
        d_ref[...] = -ADAM_LR * ((mn * c1) / (jnp.sqrt(vn * c2) + ADAM_EPS) + ADAM_WD * w_ref[...])

    blk = pl.BlockSpec((tr, cdim), lambda i: (i, 0))
    shape = jax.ShapeDtypeStruct((r, cdim), F32)
    return pl.pallas_call(
        body, name=name, grid=(r // tr,),
        in_specs=[pl.BlockSpec((n_parts, tr, cdim), lambda i: (0, i, 0)), blk, blk, blk],
        out_specs=(blk, blk, blk, blk), out_shape=(shape, shape, shape, shape),
        compiler_params=_cparams(("parallel",)),
    )(parts, w, m, v)


def _rows_of(shape):
    n = math.prod(shape)
    assert n % LANES == 0, shape
    rows = n // LANES
    return -(-rows // PACK_ROW_ALIGN) * PACK_ROW_ALIGN


def _layout(shard_shapes):
    out, off = {}, 0
    for name, _ in SHARDED:
        rows = _rows_of(shard_shapes[name])
        out[name] = (off, rows, tuple(shard_shapes[name]))
        off += rows
    return out, -(-off // PACK_BLOCK_ROWS) * PACK_BLOCK_ROWS


def _pack_shards(layout, total, arrays, dtype):
    parts = []
    for name, _ in SHARDED:
        _, rows, _ = layout[name]
        flat = arrays[name].astype(dtype).reshape(-1, LANES)
        parts.append(jnp.pad(flat, ((0, rows - flat.shape[0]), (0, 0))))
    used = sum(p.shape[0] for p in parts)
    if total > used:
        parts.append(jnp.zeros((total - used, LANES), dtype))
    return jnp.concatenate(parts, axis=0)


def _unpack_shard(layout, flat, name):
    off, _, shape = layout[name]
    n = math.prod(shape) // LANES
    return flat[off:off + n].reshape(shape)


def _unpack_full(layout, gathered, name, axis):
    off, _, shape = layout[name]
    n = math.prod(shape) // LANES
    blocks = gathered[:, off:off + n].reshape((N_DEV,) + shape)
    blocks = jnp.moveaxis(blocks, 0, axis)
    return blocks.reshape(shape[:axis] + (N_DEV * shape[axis],) + shape[axis + 1:])


def _pack_full(layout, total, grads):
    parts = []
    for name, axis in SHARDED:
        _, rows, shape = layout[name]
        g = grads[name]
        blocks = g.reshape(shape[:axis] + (N_DEV, shape[axis]) + shape[axis + 1:])
        blocks = jnp.moveaxis(blocks, axis, 0).reshape(N_DEV, -1, LANES)
        parts.append(jnp.pad(blocks, ((0, 0), (0, rows - blocks.shape[1]), (0, 0))))
    used = sum(p.shape[1] for p in parts)
    if total > used:
        parts.append(jnp.zeros((N_DEV, total - used, LANES), F32))
    return jnp.concatenate(parts, axis=1)


def _pad_lanes(a):
    return jnp.pad(a, ((0, 0), (0, LANES - a.shape[1])))


def _pair_layouts(c):
    s = c.shape[0]
    by_pair = c.T.reshape(N_PAIRS, 2, s)
    c_col = jnp.repeat(by_pair.transpose(0, 2, 1), HEAD_DIM, axis=2)
    c_row = jnp.pad(by_pair, ((0, 0), (0, 6), (0, 0)))
    return c_col, c_row


def _key_norm_bound(k):
    norms = jnp.sqrt(jnp.max(jnp.sum(jnp.square(k.astype(F32)).reshape(k.shape[0], N_MIX_HEADS, HEAD_DIM), axis=2), axis=0))
    rows = jnp.pad(norms.reshape(N_PAIRS, 2), ((0, 0), (0, 6)))
    return jnp.broadcast_to(rows[:, :, None], (N_PAIRS, 8, LANES))


def _forward_backward(x, mem, target, wts, small):
    n_a = wts["w_in_a"].shape[0]
    n_b = wts["w_in_b"].shape[0]
    depth = n_a + n_b
    w_kv = wts["w_kv_shared"]
    w_kv_kv = w_kv[:, :2 * MIX_W]
    w_kv_f = _pad_lanes(w_kv[:, 2 * MIX_W:])
    b_f = _pad_lanes(small["b_f"].reshape(1, -1))

    saved = []
    shared = None
    h = x
    for l in range(depth):
        is_a = l < n_a
        if l == n_a:
            hs = _rmsnorm_fwd(h, small["kv_norm_g"], name="kv_norm")
            kv = _mm(hs, w_kv_kv, name="kv_proj", out_dtype=BF16)
            f = _mm(hs, w_kv_f, name="gate_proj")
            c = _gate_fwd(f, b_f, name="gate_cumsum")
            c_col, c_row = _pair_layouts(c[:, :N_MIX_HEADS])
            shared = dict(h=h, hs=hs, kv=kv, f=f, c_col=c_col, c_row=c_row, k_max=_key_norm_bound(kv[:, :MIX_W]))
        hn = _rmsnorm_fwd(h, small["norm1_g"][l], name=f"norm1_{l}")
        memn = _rmsnorm_fwd(mem, small["mem_norm_g"][l], name=f"mem_norm_{l}")
        mkv = _mm(memn, wts["w_mem_kv"][l], name=f"mem_kv_{l}", out_dtype=BF16)
        if is_a:
            proj = _mm(hn, wts["w_in_a"][l], name=f"in_proj_{l}", out_dtype=BF16)
            mix, stat, first = _sb_fwd(proj, name=f"sb_fwd_{l}")
            mix32 = None
            q_block = 3 * MIX_W // MEM_W
        else:
            proj = _mm(hn, wts["w_in_b"][l - n_a], name=f"in_proj_{l}", out_dtype=BF16)
            mix, mix32, stat, first = _fox_fwd(proj, shared["kv"], shared["c_col"], shared["c_row"], shared["k_max"], name=f"fox_fwd_{l}")
            q_block = MIX_W // MEM_W
        mem_out = _mem_fwd(proj, q_block, mkv, name=f"mem_fwd_{l}")
        merged = jnp.concatenate([mix, mem_out], axis=1)
        h_mid = _mm(merged, wts["w_o"][l], name=f"o_proj_{l}", res=h)
        h2n = _rmsnorm_fwd(h_mid, small["norm2_g"][l], name=f"norm2_{l}")
        u, act = _mm(h2n, wts["w_mlp1"][l], name=f"mlp1_{l}", epilogue="relu2")
        h_out = _mm(act, wts["w_mlp2"][l], name=f"mlp2_{l}", res=h_mid)
        saved.append(dict(h=h, hn=hn, memn=memn, mkv=mkv, proj=proj, stat=stat, first=first, mix32=mix32, merged=merged, h_mid=h_mid, h2n=h2n, u=u, act=act,
                          q_block=q_block))
        h = h_out

    loss, dh, dg_final = _final_loss(h, target, small["final_norm_g"], name="final_loss")

    g_w = {k: [None] * wts[k].shape[0] for k in ("w_in_a", "w_in_b", "w_mem_kv", "w_o", "w_mlp1", "w_mlp2")}
    g_n = {k: [None] * depth for k in ("norm1_g", "mem_norm_g", "norm2_g")}
    dk_sh = dv_sh = dc_sh = dcq_sh = None
    for l in reversed(range(depth)):
        sv = saved[l]
        is_a = l < n_a
        du = _mm(dh, wts["w_mlp2"][l], name=f"d_act_{l}", trans_b=True, epilogue="drelu2", u=sv["u"], out_dtype=BF16)
        g_w["w_mlp2"][l] = _mm_tn(sv["act"], dh, name=f"dw_mlp2_{l}")
        g_w["w_mlp1"][l] = _mm_tn(sv["h2n"], du, name=f"dw_mlp1_{l}")
        dh_mid, g_n["norm2_g"][l] = _mm(du, wts["w_mlp1"][l], name=f"d_h2n_{l}", trans_b=True, norm=(sv["h_mid"], small["norm2_g"][l], dh))
        dmerged = _mm(dh_mid, wts["w_o"][l], name=f"d_merged_{l}", trans_b=True, out_dtype=BF16)
        g_w["w_o"][l] = _mm_tn(sv["merged"], dh_mid, name=f"dw_o_{l}")
        dqm, dmkv = _mem_bwd(sv["proj"], sv["q_block"], sv["mkv"], dmerged, name=f"mem_bwd_{l}")
        if is_a:
            dq, dk, dv = _sb_bwd(sv["proj"], dmerged, sv["stat"], sv["first"], name=f"sb_bwd_{l}")
            dproj = jnp.concatenate([dq, dk.astype(BF16), dv.astype(BF16), dqm], axis=1)
            w_in, key, idx = wts["w_in_a"][l], "w_in_a", l
        else:
            dq, dk, dv, dc, dcq = _fox_bwd(sv["proj"], shared["kv"], shared["c_col"], shared["c_row"], sv["mix32"], dmerged, sv["stat"],
                                           sv["first"], name=f"fox_bwd_{l}")
            dk_sh = dk if dk_sh is None else dk_sh + dk
            dv_sh = dv if dv_sh is None else dv_sh + dv
            dc_sh = dc if dc_sh is None else dc_sh + dc
            dcq_sh = dcq if dcq_sh is None else dcq_sh + dcq
            dproj = jnp.concatenate([dq, dqm], axis=1)
            w_in, key, idx = wts["w_in_b"][l - n_a], "w_in_b", l - n_a
        g_w[key][idx] = _mm_tn(sv["hn"], dproj, name=f"dw_in_{l}")
        dh, g_n["norm1_g"][l] = _mm(dproj, w_in, name=f"d_hn_{l}", trans_b=True, norm=(sv["h"], small["norm1_g"][l], dh_mid))
        g_w["w_mem_kv"][l] = _mm_tn(sv["memn"], dmkv, name=f"dw_mem_kv_{l}")
        _, g_n["mem_norm_g"][l] = _mm(dmkv, wts["w_mem_kv"][l], name=f"d_memn_{l}", trans_b=True, norm=(mem, small["mem_norm_g"][l], None))
        if l == n_a:
            s_len = x.shape[0]
            dc_query = jnp.stack([dcq_sh[:, :, 0], dcq_sh[:, :, HEAD_DIM]], axis=-1).transpose(1, 0, 2).reshape(s_len, N_MIX_HEADS)
            dc_tok = _pad_lanes(dc_sh[:, :2, :].reshape(N_MIX_HEADS, s_len).T + dc_query)
            df, db = _gate_bwd(shared["f"], b_f, dc_tok, name="gate_bwd")
            dkv = jnp.concatenate([dk_sh, dv_sh], axis=1)
            dw_kv_kv = _mm_tn(shared["hs"], dkv, name="dw_kv")
            dw_kv_f = _mm_tn(shared["hs"], df, name="dw_gate")
            dhs = _mm(dkv, w_kv_kv, name="d_hs_kv", trans_b=True)
            dh, dg_kv = _mm(df, w_kv_f, name="d_hs_gate", trans_b=True, res=dhs, norm=(shared["h"], small["kv_norm_g"], dh))
            g_kv = jnp.concatenate([dw_kv_kv, dw_kv_f[:, :N_KV_F]], axis=1)

    grads = {k: jnp.stack(v) for k, v in g_w.items()}
    grads["w_kv_shared"] = g_kv
    d = x.shape[1]
    small_g = dict(
        norm1_g=jnp.concatenate(g_n["norm1_g"], axis=0), mem_norm_g=jnp.concatenate(g_n["mem_norm_g"], axis=0),
        norm2_g=jnp.concatenate(g_n["norm2_g"], axis=0), kv_norm_g=dg_kv.reshape(d), b_f=db[0, :N_KV_F], final_norm_g=dg_final.reshape(d))
    return loss, dh, grads, small_g


def _pack_small(vals, d):
    rows = []
    for name in REPLICATED:
        a = vals[name].astype(F32)
        if name == "b_f":
            a = jnp.pad(a, (0, d - a.shape[0]))
        rows.append(a.reshape(-1, d))
    packed = jnp.concatenate(rows, axis=0)
    pad = -packed.shape[0] % 8
    return jnp.pad(packed, ((0, pad), (0, 0)))


def _unpack_small(packed, shapes):
    out, off = {}, 0
    for name in REPLICATED:
        shape = shapes[name]
        if name == "b_f":
            out[name] = packed[off, :shape[0]]
            off += 1
        else:
            n = math.prod(shape) // packed.shape[1]
            out[name] = packed[off:off + n].reshape(shape)
            off += n
    return out


def kernel(x, mem, norm1_g, w_in_a, w_in_b, w_mem_kv, mem_norm_g, w_o, norm2_g, w_mlp1, w_mlp2, kv_norm_g, w_kv_shared, b_f, final_norm_g, loss_target, m_norm1_g, m_w_in_a, m_w_in_b, m_w_mem_kv, m_mem_norm_g, m_w_o, m_norm2_g, m_w_mlp1, m_w_mlp2, m_kv_norm_g, m_w_kv_shared, m_b_f, m_final_norm_g, v_norm1_g, v_w_in_a, v_w_in_b, v_w_mem_kv, v_mem_norm_g, v_w_o, v_norm2_g, v_w_mlp1, v_w_mlp2, v_kv_norm_g, v_w_kv_shared, v_b_f, v_final_norm_g):
    w = dict(norm1_g=norm1_g, w_in_a=w_in_a, w_in_b=w_in_b, w_mem_kv=w_mem_kv, mem_norm_g=mem_norm_g, w_o=w_o, norm2_g=norm2_g,
             w_mlp1=w_mlp1, w_mlp2=w_mlp2, kv_norm_g=kv_norm_g, w_kv_shared=w_kv_shared, b_f=b_f, final_norm_g=final_norm_g)
    m = dict(norm1_g=m_norm1_g, w_in_a=m_w_in_a, w_in_b=m_w_in_b, w_mem_kv=m_w_mem_kv, mem_norm_g=m_mem_norm_g, w_o=m_w_o,
             norm2_g=m_norm2_g, w_mlp1=m_w_mlp1, w_mlp2=m_w_mlp2, kv_norm_g=m_kv_norm_g, w_kv_shared=m_w_kv_shared, b_f=m_b_f,
             final_norm_g=m_final_norm_g)
    v = dict(norm1_g=v_norm1_g, w_in_a=v_w_in_a, w_in_b=v_w_in_b, w_mem_kv=v_w_mem_kv, mem_norm_g=v_mem_norm_g, w_o=v_w_o,
             norm2_g=v_norm2_g, w_mlp1=v_w_mlp1, w_mlp2=v_w_mlp2, kv_norm_g=v_kv_norm_g, w_kv_shared=v_w_kv_shared, b_f=v_b_f,
             final_norm_g=v_final_norm_g)
    d = x.shape[-1]
    layout, total = _layout({name: w[name].shape for name, _ in SHARDED})

    gathered = _all_gather(_pack_shards(layout, total, w, BF16), name="gather_weights").reshape(N_DEV, total, LANES)
    wts = {name: _unpack_full(layout, gathered, name, axis) for name, axis in SHARDED}
    small = {name: w[name] for name in REPLICATED}

    loss, grad_x, grads, small_g = _forward_backward(x[0], mem[0], loss_target[0], wts, small)

    g_packed = _pack_full(layout, total, grads).astype(BF16)
    theirs, small_parts = _sibling_exchange(g_packed, _pack_small(small_g, d), name="exchange_siblings")
    parts = _chip_exchange(_pair_sum(g_packed, theirs, name="pair_sum"), name="exchange_chips")
    g_flat, d_flat, m_flat, v_flat = _adamw(parts, _pack_shards(layout, total, w, F32), _pack_shards(layout, total, m, F32),
                                            _pack_shards(layout, total, v, F32), name="adamw_sharded")
    gs, ds_, ms, vs = _adamw(small_parts, _pack_small(w, d), _pack_small(m, d), _pack_small(v, d), name="adamw_replicated")

    shapes = {name: w[name].shape for name in REPLICATED}
    out_g, out_d, out_m, out_v = {}, {}, {}, {}
    for flat, small_flat, out in ((g_flat, gs, out_g), (d_flat, ds_, out_d), (m_flat, ms, out_m), (v_flat, vs, out_v)):
        for name, _ in SHARDED:
            out[name] = _unpack_shard(layout, flat, name)
        out.update(_unpack_small(small_flat, shapes))

    loss_total = lax.psum(loss[0, 0], ("x", "y", "c"))
    return (loss_total, grad_x[None], *[out_g[n] for n in WEIGHT_ORDER], *[out_d[n] for n in WEIGHT_ORDER],
            *[out_m[n] for n in WEIGHT_ORDER], *[out_v[n] for n in WEIGHT_ORDER])
```

```python
import functools
import math

import jax
import jax.numpy as jnp
from jax import lax
from jax.experimental import pallas as pl
from jax.experimental.pallas import tpu as pltpu

F32 = jnp.float32
BF16 = jnp.bfloat16

N_DEV = 8
HEAD_DIM = 64
N_MIX_HEADS = 8
N_MEM_HEADS = 4
MIX_W = N_MIX_HEADS * HEAD_DIM
MEM_W = N_MEM_HEADS * HEAD_DIM
N_PAIRS = N_MIX_HEADS // 2
LANES = 128
SB_B = 256
FOX_B = 512
CUM_SUB = 256
SB_CUT = 64.0
FOX_CUT = 45.0
EPS = 1e-6
NEG_INF = -1e30
QK_SCALE = 1.0 / math.sqrt(HEAD_DIM)
LOG2E = 1.4426950408889634
N_KV_F = 8

ADAM_LR = 0.001
ADAM_B1 = 0.9
ADAM_B2 = 0.999
ADAM_EPS = 1e-08
ADAM_WD = 0.01
ADAM_STEP = 10

VMEM_LIMIT = 56 * 1024 * 1024
PACK_ROW_ALIGN = 16
PACK_BLOCK_ROWS = 1024

MESH = pl.DeviceIdType.MESH

SHARDED = (("w_in_a", 2), ("w_in_b", 1), ("w_mem_kv", 1), ("w_o", 2), ("w_mlp1", 2), ("w_mlp2", 1), ("w_kv_shared", 1))
REPLICATED = ("norm1_g", "mem_norm_g", "norm2_g", "kv_norm_g", "b_f", "final_norm_g")
WEIGHT_ORDER = ("norm1_g", "w_in_a", "w_in_b", "w_mem_kv", "mem_norm_g", "w_o", "norm2_g", "w_mlp1", "w_mlp2",
                "kv_norm_g", "w_kv_shared", "b_f", "final_norm_g")


def _cparams(sem=None):
    return pltpu.CompilerParams(dimension_semantics=sem, vmem_limit_bytes=VMEM_LIMIT)


def _resident(shape, index_map):
    return pl.BlockSpec(shape, index_map, pipeline_mode=pl.Buffered(1))


def _pick(n, cands):
    for c in cands:
        if c <= n and n % c == 0:
            return c
    return n


def _dot(a, b, dims):
    return lax.dot_general(a, b, (dims, ((), ())), preferred_element_type=F32)


NN = ((1,), (0,))
NT = ((1,), (1,))
TN = ((0,), (0,))


MM_CHUNK = 512


def _mm(a, b, *, name, trans_b=False, out_dtype=F32, res=None, epilogue=None, u=None, norm=None):
    m, k = a.shape
    n = b.shape[0] if trans_b else b.shape[1]
    tm = _pick(m, (512, 256, 128))
    tn = _pick(n, (MM_CHUNK, 384, 256, 128))
    has_dres = norm is not None and norm[2] is not None

    def body(*refs):
        a_ref, b_ref = refs[0], refs[1]
        pos = 2
        res_ref = u_ref = x_ref = g_ref = dres_ref = None
        if res is not None:
            res_ref = refs[pos]
            pos += 1
        if u is not None:
            u_ref = refs[pos]
            pos += 1
        if norm is not None:
            x_ref, g_ref = refs[pos], refs[pos + 1]
            pos += 2
            if has_dres:
                dres_ref = refs[pos]
                pos += 1
        outs = refs[pos:]
        av = a_ref[...].astype(BF16)
        prods = []
        for c in range(n // tn):
            cols = slice(c * tn, (c + 1) * tn)
            if trans_b:
                acc = _dot(av, b_ref[cols, :].astype(BF16), NT)
            else:
                acc = _dot(av, b_ref[:, cols].astype(BF16), NN)
            if res_ref is not None:
                acc = res_ref[:, cols] + acc
            if norm is not None:
                prods.append(acc)
            elif epilogue == "relu2":
                outs[0][:, cols] = acc.astype(BF16)
                r = jnp.maximum(acc, 0.0)
                outs[1][:, cols] = (r * r).astype(BF16)
            elif epilogue == "drelu2":
                outs[0][:, cols] = (acc * (2.0 * jnp.maximum(u_ref[:, cols], 0.0))).astype(out_dtype)
            else:
                outs[0][:, cols] = acc.astype(out_dtype)
        if norm is not None:
            dyv = prods[0] if len(prods) == 1 else jnp.concatenate(prods, axis=1)
            xf = x_ref[...]
            r = lax.rsqrt(jnp.mean(xf * xf, axis=-1, keepdims=True) + EPS)
            xh = xf * r
            dyg = dyv * g_ref[...]
            dx = r * (dyg - xh * jnp.mean(dyg * xh, axis=-1, keepdims=True))
            outs[0][...] = dx if dres_ref is None else dres_ref[...] + dx

            @pl.when(pl.program_id(0) == 0)
            def _():
                outs[1][...] = jnp.zeros_like(outs[1])

            outs[1][...] += jnp.sum(dyv * xh, axis=0, keepdims=True)

    row = pl.BlockSpec((tm, n), lambda i: (i, 0))
    vec = pl.BlockSpec((1, n), lambda i: (0, 0))
    in_specs = [pl.BlockSpec((tm, k), lambda i: (i, 0)), _resident(b.shape, lambda i: (0, 0))]
    args = [a, b]
    if res is not None:
        in_specs.append(row)
        args.append(res)
    if u is not None:
        in_specs.append(row)
        args.append(u)
    if norm is not None:
        in_specs += [row, vec] + ([row] if has_dres else [])
        args += [norm[0], norm[1].reshape(1, n)] + ([norm[2]] if has_dres else [])
        out_shape = (jax.ShapeDtypeStruct((m, n), F32), jax.ShapeDtypeStruct((1, n), F32))
        out_specs = (row, vec)
    elif epilogue == "relu2":
        out_shape = (jax.ShapeDtypeStruct((m, n), BF16), jax.ShapeDtypeStruct((m, n), BF16))
        out_specs = (row, row)
    else:
        out_shape = (jax.ShapeDtypeStruct((m, n), out_dtype),)
        out_specs = (row,)
    outs = pl.pallas_call(
        body, name=name, grid=(m // tm,), in_specs=in_specs, out_specs=out_specs, out_shape=out_shape,
        compiler_params=_cparams(("arbitrary",) if norm is not None else ("parallel",)),
    )(*args)
    return outs if (epilogue == "relu2" or norm is not None) else outs[0]


def _mm_tn(x, dy, *, name):
    m, k1 = x.shape
    n = dy.shape[1]
    t1 = _pick(k1, (1024, 896, 768, 512, 256, 128))
    tn = _pick(n, (1024, 896, 768, 512, 256, 128))
    tm = _pick(m, (2048, 1024, 512, 256, 128))
    nm = m // tm

    def body(x_ref, dy_ref, o_ref):
        mm = pl.program_id(2)

        @pl.when(mm == 0)
        def _():
            o_ref[...] = jnp.zeros_like(o_ref)

        o_ref[...] += _dot(x_ref[...].astype(BF16), dy_ref[...].astype(BF16), TN)

    return pl.pallas_call(
        body, name=name, grid=(k1 // t1, n // tn, nm),
        in_specs=[pl.BlockSpec((tm, t1), lambda i, j, mm: (mm, i)), pl.BlockSpec((tm, tn), lambda i, j, mm: (mm, j))],
        out_specs=pl.BlockSpec((t1, tn), lambda i, j, mm: (i, j)),
        out_shape=jax.ShapeDtypeStruct((k1, n), F32),
        compiler_params=_cparams(("parallel", "parallel", "arbitrary")),
    )(x, dy)


def _rmsnorm_fwd(x, g, *, name):
    s, d = x.shape
    tm = _pick(s, (512, 256, 128))

    def body(x_ref, g_ref, o_ref):
        xf = x_ref[...]
        r = lax.rsqrt(jnp.mean(xf * xf, axis=-1, keepdims=True) + EPS)
        o_ref[...] = (xf * r * g_ref[...]).astype(BF16)

    return pl.pallas_call(
        body, name=name, grid=(s // tm,),
        in_specs=[pl.BlockSpec((tm, d), lambda i: (i, 0)), pl.BlockSpec((1, d), lambda i: (0, 0))],
        out_specs=pl.BlockSpec((tm, d), lambda i: (i, 0)),
        out_shape=jax.ShapeDtypeStruct((s, d), BF16),
        compiler_params=_cparams(("parallel",)),
    )(x, g.reshape(1, d))


def _final_loss(h, target, g, *, name):
    s, d = h.shape
    tm = _pick(s, (256, 128))

    def body(h_ref, t_ref, g_ref, loss_ref, dh_ref, dg_ref):
        i = pl.program_id(0)
        xf = h_ref[...]
        gv = g_ref[...]
        r = lax.rsqrt(jnp.mean(xf * xf, axis=-1, keepdims=True) + EPS)
        xh = xf * r
        err = xh * gv - t_ref[...]
        part = 0.5 * jnp.sum(jnp.mean(err * err, axis=-1, keepdims=True), axis=0, keepdims=True)
        dyv = err * (1.0 / d)
        dyg = dyv * gv
        dh_ref[...] = r * (dyg - xh * jnp.mean(dyg * xh, axis=-1, keepdims=True))

        @pl.when(i == 0)
        def _():
            dg_ref[...] = jnp.zeros_like(dg_ref)
            loss_ref[...] = jnp.zeros_like(loss_ref)

        dg_ref[...] += jnp.sum(dyv * xh, axis=0, keepdims=True)
        loss_ref[...] += jnp.broadcast_to(part, loss_ref.shape)

    row = pl.BlockSpec((tm, d), lambda i: (i, 0))
    vec = pl.BlockSpec((1, d), lambda i: (0, 0))
    return pl.pallas_call(
        body, name=name, grid=(s // tm,), in_specs=[row, row, vec],
        out_specs=(pl.BlockSpec((1, LANES), lambda i: (0, 0)), row, vec),
        out_shape=(jax.ShapeDtypeStruct((1, LANES), F32), jax.ShapeDtypeStruct((s, d), F32), jax.ShapeDtypeStruct((1, d), F32)),
        compiler_params=_cparams(("arbitrary",)),
    )(h, target, g.reshape(1, d))


def _head_lanes(rows):
    lane = lax.broadcasted_iota(jnp.int32, (rows, LANES), 1)
    return [lane < HEAD_DIM, lane >= HEAD_DIM]


def _tri(b, cmp):
    row = lax.broadcasted_iota(jnp.int32, (b, b), 0)
    col = lax.broadcasted_iota(jnp.int32, (b, b), 1)
    return cmp(row, col)


def _twice(mask):
    return jnp.concatenate([mask, mask], axis=0)


def _stack_heads(x, heads):
    zero = jnp.zeros_like(x)
    return jnp.concatenate([jnp.where(heads[0], x, zero), jnp.where(heads[1], x, zero)], axis=0)


def _unstack_heads(x2, heads):
    b = x2.shape[0] // 2
    return jnp.where(heads[0], x2[:b], x2[b:])


def _stack_stat(stat):
    return jnp.concatenate([stat[:, 0:1], stat[:, HEAD_DIM:HEAD_DIM + 1]], axis=0)


def _unstack_stat(col, heads):
    b = col.shape[0] // 2
    return jnp.where(heads[0], jnp.broadcast_to(col[:b], (b, LANES)), jnp.broadcast_to(col[b:], (b, LANES)))


def _tri_dot(x, tri_bf16):
    return _dot(x.astype(BF16), tri_bf16, NN)


def _prefix_sums(x, tri_bf16, inclusive):
    sub = tri_bf16.shape[0]
    outs, carry = [], None
    for c in range(x.shape[1] // sub):
        xs = x[:, c * sub:(c + 1) * sub]
        loc = _tri_dot(xs, tri_bf16)
        outs.append(loc if carry is None else loc + carry)
        tot = loc[:, sub - 1:sub] if inclusive else loc[:, sub - 1:sub] + xs[:, sub - 1:sub]
        carry = tot if carry is None else carry + tot
    return (outs[0] if len(outs) == 1 else jnp.concatenate(outs, axis=1)), carry


def _suffix_sums(x, tri_bf16):
    sub = tri_bf16.shape[0]
    n = x.shape[1] // sub
    outs, carry = [None] * n, None
    for c in reversed(range(n)):
        xs = x[:, c * sub:(c + 1) * sub]
        loc = _tri_dot(xs, tri_bf16)
        outs[c] = loc if carry is None else loc + carry
        tot = loc[:, 0:1] + xs[:, 0:1].astype(BF16).astype(F32)
        carry = tot if carry is None else carry + tot
    return (outs[0] if n == 1 else jnp.concatenate(outs, axis=1)), carry


def _softplus2(z):
    z2 = z * LOG2E
    neg_abs = lax.bitcast_convert_type(lax.bitcast_convert_type(z2, jnp.uint32) | jnp.uint32(0x80000000), F32)
    return z2, jnp.maximum(z2, 0.0) + jnp.log2(1.0 + jnp.exp2(neg_abs))


def _block_rows(j, b):
    return pl.ds(pl.multiple_of(j * b, b), b)


def _sb_fwd(proj, *, name):
    s = proj.shape[0]
    b = _pick(s, (SB_B, 128))

    def body(q_ref, k_ref, v_ref, o_ref, tot_ref, first_ref, acc_ref):
        i = pl.program_id(1)
        heads = _head_lanes(b)
        suffix = _tri(min(b, CUM_SUB), lambda r, c: r > c).astype(BF16)
        strict = _twice(_tri(b, lambda r, c: c < r))
        q2 = _stack_heads(q_ref[...] * QK_SCALE, heads)

        def tile(j, a, masked):
            rows = _block_rows(j, b)
            z2, sp = _softplus2(_dot(q2, k_ref[rows, :], NT))
            if masked:
                sp = jnp.where(strict, sp, 0.0)
            rsum, total = _suffix_sums(sp, suffix)
            w = jnp.exp2((z2 - sp) - (a + rsum))
            if masked:
                w = jnp.where(strict, w, 0.0)
            acc_ref[...] += _dot(w.astype(BF16), v_ref[rows, :], NN)
            return a + total

        acc_ref[...] = jnp.zeros_like(acc_ref)
        a = tile(i, jnp.zeros((2 * b, 1), F32), True)

        def more(c):
            return jnp.logical_and(c[0] < i, c[2] < SB_CUT)

        def step(c):
            a = tile(i - 1 - c[0], c[1], False)
            return c[0] + 1, a, jnp.min(a)

        done, a, _ = lax.while_loop(more, step, (jnp.int32(0), a, jnp.min(a)))
        o_ref[...] = _unstack_heads(acc_ref[...], heads).astype(BF16)
        tot_ref[0] = _unstack_stat(a, heads)
        first_ref[pl.program_id(0), i] = i - done

    qblk = pl.BlockSpec((b, LANES), lambda p, i: (i, p))
    return pl.pallas_call(
        body, name=name, grid=(N_PAIRS, s // b),
        in_specs=[qblk, _resident((s, LANES), lambda p, i: (0, N_PAIRS + p)), _resident((s, LANES), lambda p, i: (0, 2 * N_PAIRS + p))],
        out_specs=(qblk, pl.BlockSpec((1, b, LANES), lambda p, i: (p, i, 0)), pl.BlockSpec(memory_space=pltpu.SMEM)),
        out_shape=(jax.ShapeDtypeStruct((s, MIX_W), BF16), jax.ShapeDtypeStruct((N_PAIRS, s, LANES), F32),
                   jax.ShapeDtypeStruct((N_PAIRS, s // b), jnp.int32)),
        scratch_shapes=[pltpu.VMEM((2 * b, LANES), F32)],
        compiler_params=_cparams(("arbitrary", "arbitrary")),
    )(proj, proj, proj)


def _sb_bwd(proj, dmerged, tot, first, *, name):
    s = proj.shape[0]
    b = _pick(s, (SB_B, 128))

    def body(q_ref, k_ref, v_ref, do_ref, tot_ref, first_ref, dq_ref, dk_ref, dv_ref, dq_acc):
        i = pl.program_id(1)

        @pl.when(i == 0)
        def _():
            dk_ref[...] = jnp.zeros_like(dk_ref)
            dv_ref[...] = jnp.zeros_like(dv_ref)

        heads = _head_lanes(b)
        incl = _tri(min(b, CUM_SUB), lambda r, c: r <= c).astype(BF16)
        excl = _tri(min(b, CUM_SUB), lambda r, c: r < c).astype(BF16)
        strict = _twice(_tri(b, lambda r, c: c < r))
        q2 = _stack_heads(q_ref[...] * QK_SCALE, heads)
        do2 = _stack_heads(do_ref[...], heads)
        tot2 = _stack_stat(tot_ref[0])

        def tile(j, pre, gpre, masked):
            rows = _block_rows(j, b)
            kb = k_ref[rows, :]
            z2, sp = _softplus2(_dot(q2, kb, NT))
            oms = jnp.exp2(-sp)
            if masked:
                sp = jnp.where(strict, sp, 0.0)
            pin, ptot = _prefix_sums(sp, incl, True)
            w = jnp.exp2((z2 - sp) + (pin + (pre - tot2)))
            if masked:
                w = jnp.where(strict, w, 0.0)
            gw = _dot(do2, v_ref[rows, :], NT) * w
            gex, gtot = _prefix_sums(gw, excl, False)
            dz = gw * oms - (1.0 - oms) * (gpre + gex)
            if masked:
                dz = jnp.where(strict, dz, 0.0)
            dzb = dz.astype(BF16)
            dq_acc[...] += _dot(dzb, kb, NN)
            dk_ref[rows, :] += _dot(dzb, q2, TN)
            dv_ref[rows, :] += _dot(w.astype(BF16), do2, TN)
            return pre + ptot, gpre + gtot

        dq_acc[...] = jnp.zeros_like(dq_acc)
        zero = jnp.zeros((2 * b, 1), F32)
        pre, gpre = lax.fori_loop(first_ref[pl.program_id(0), i], i, lambda j, c: tile(j, c[0], c[1], False), (zero, zero))
        tile(i, pre, gpre, True)
        dq_ref[...] = (_unstack_heads(dq_acc[...], heads) * QK_SCALE).astype(BF16)

    qblk = pl.BlockSpec((b, LANES), lambda p, i: (i, p))
    full = _resident((s, LANES), lambda p, i: (0, p))
    return pl.pallas_call(
        body, name=name, grid=(N_PAIRS, s // b),
        in_specs=[qblk, _resident((s, LANES), lambda p, i: (0, N_PAIRS + p)), _resident((s, LANES), lambda p, i: (0, 2 * N_PAIRS + p)),
                  qblk, pl.BlockSpec((1, b, LANES), lambda p, i: (p, i, 0)), pl.BlockSpec(memory_space=pltpu.SMEM)],
        out_specs=(qblk, full, full),
        out_shape=(jax.ShapeDtypeStruct((s, MIX_W), BF16), jax.ShapeDtypeStruct((s, MIX_W), F32), jax.ShapeDtypeStruct((s, MIX_W), F32)),
        scratch_shapes=[pltpu.VMEM((2 * b, LANES), F32)],
        compiler_params=_cparams(("parallel", "arbitrary")),
    )(proj, proj, proj, dmerged, tot, first)


def _fox_fwd(proj, kv, c_col, c_row, k_max, *, name):
    s = proj.shape[0]
    b = _pick(s, (FOX_B, 128))

    def body(q_ref, k_ref, v_ref, cc_ref, cr_ref, km_ref, o_ref, o32_ref, lse_ref, first_ref, acc_ref):
        i = pl.program_id(1)
        heads = _head_lanes(b)
        causal = _twice(_tri(b, lambda r, c: c <= r))
        top = lax.broadcasted_iota(jnp.int32, (2 * b, b), 0) < b
        q2 = _stack_heads(q_ref[...] * QK_SCALE, heads)
        c_t = _stack_stat(cc_ref[0])
        qf = q2.astype(F32)
        kmv = km_ref[0]
        z_max = jnp.sqrt(jnp.sum(qf * qf, axis=1, keepdims=True)) * jnp.where(top[:, 0:1], kmv[0:1, 0:1], kmv[1:2, 0:1]) * 1.001

        def tile(j, m, l, masked):
            rows = _block_rows(j, b)
            gate = c_t - jnp.where(top, cr_ref[0, 0:1, rows], cr_ref[0, 1:2, rows])
            sc = _dot(q2, k_ref[rows, :], NT) + gate
            if masked:
                sc = jnp.where(causal, sc, NEG_INF)
            m_new = jnp.maximum(m, jnp.max(sc, axis=1, keepdims=True))
            p = jnp.exp(sc - m_new)
            alpha = jnp.exp(m - m_new)
            acc_ref[...] = alpha * acc_ref[...] + _dot(p.astype(BF16), v_ref[rows, :], NN)
            return m_new, alpha * l + jnp.sum(p, axis=1, keepdims=True), jnp.max(z_max + gate[:, 0:1] - m_new)

        acc_ref[...] = jnp.zeros_like(acc_ref)
        m, l, slack = tile(i, jnp.full((2 * b, 1), NEG_INF, F32), jnp.zeros((2 * b, 1), F32), True)

        def more(c):
            return jnp.logical_and(c[0] < i, c[3] > -FOX_CUT)

        def step(c):
            m, l, slack = tile(i - 1 - c[0], c[1], c[2], False)
            return c[0] + 1, m, l, slack

        done, m, l, _ = lax.while_loop(more, step, (jnp.int32(0), m, l, slack))
        out = _unstack_heads(acc_ref[...] * (1.0 / l), heads)
        o_ref[...] = out.astype(BF16)
        o32_ref[...] = out
        lse_ref[0] = _unstack_stat(m + jnp.log(l), heads)
        first_ref[pl.program_id(0), i] = i - done

    qblk = pl.BlockSpec((b, LANES), lambda p, i: (i, p))
    stat = pl.BlockSpec((1, b, LANES), lambda p, i: (p, i, 0))
    return pl.pallas_call(
        body, name=name, grid=(N_PAIRS, s // b),
        in_specs=[qblk, _resident((s, LANES), lambda p, i: (0, p)), _resident((s, LANES), lambda p, i: (0, N_PAIRS + p)),
                  stat, _resident((1, 8, s), lambda p, i: (p, 0, 0)), pl.BlockSpec((1, 8, LANES), lambda p, i: (p, 0, 0))],
        out_specs=(qblk, qblk, stat, pl.BlockSpec(memory_space=pltpu.SMEM)),
        out_shape=(jax.ShapeDtypeStruct((s, MIX_W), BF16), jax.ShapeDtypeStruct((s, MIX_W), F32), jax.ShapeDtypeStruct((N_PAIRS, s, LANES), F32),
                   jax.ShapeDtypeStruct((N_PAIRS, s // b), jnp.int32)),
        scratch_shapes=[pltpu.VMEM((2 * b, LANES), F32)],
        compiler_params=_cparams(("arbitrary", "arbitrary")),
    )(proj, kv, kv, c_col, c_row, k_max)


def _fox_bwd(proj, kv, c_col, c_row, out32, dmerged, lse, first, *, name):
    s = proj.shape[0]
    b = _pick(s, (FOX_B, 128))

    def body(q_ref, k_ref, v_ref, cc_ref, cr_ref, o_ref, do_ref, lse_ref, first_ref, dq_ref, dk_ref, dv_ref, dc_ref, dcq_ref, dq_acc):
        i = pl.program_id(1)

        @pl.when(i == 0)
        def _():
            dk_ref[...] = jnp.zeros_like(dk_ref)
            dv_ref[...] = jnp.zeros_like(dv_ref)
            dc_ref[...] = jnp.zeros_like(dc_ref)

        heads = _head_lanes(b)
        causal = _twice(_tri(b, lambda r, c: c <= r))
        top = lax.broadcasted_iota(jnp.int32, (2 * b, b), 0) < b
        q2 = _stack_heads(q_ref[...] * QK_SCALE, heads)
        dov = do_ref[...]
        do2 = _stack_heads(dov, heads)
        prod = dov.astype(F32) * o_ref[...]
        delta = jnp.concatenate([jnp.sum(jnp.where(heads[hh], prod, 0.0), axis=1, keepdims=True) for hh in range(2)], axis=0)
        c_t = _stack_stat(cc_ref[0])
        lse_t = _stack_stat(lse_ref[0])

        def tile(j, rsum, masked):
            rows = _block_rows(j, b)
            kb = k_ref[rows, :]
            c_s = jnp.where(top, cr_ref[0, 0:1, rows], cr_ref[0, 1:2, rows])
            sc = _dot(q2, kb, NT) + (c_t - c_s)
            p = jnp.exp(sc - lse_t)
            if masked:
                p = jnp.where(causal, p, 0.0)
            ds = p * (_dot(do2, v_ref[rows, :], NT) - delta)
            dsb = ds.astype(BF16)
            dq_acc[...] += _dot(dsb, kb, NN)
            dk_ref[rows, :] += _dot(dsb, q2, TN)
            dv_ref[rows, :] += _dot(p.astype(BF16), do2, TN)
            dc_ref[0, 0:1, rows] -= jnp.sum(ds[:b], axis=0, keepdims=True)
            dc_ref[0, 1:2, rows] -= jnp.sum(ds[b:], axis=0, keepdims=True)
            return rsum + jnp.sum(ds, axis=1, keepdims=True)

        dq_acc[...] = jnp.zeros_like(dq_acc)
        rsum = lax.fori_loop(first_ref[pl.program_id(0), i], i, lambda j, r: tile(j, r, False), jnp.zeros((2 * b, 1), F32))
        rsum = tile(i, rsum, True)
        dq_ref[...] = (_unstack_heads(dq_acc[...], heads) * QK_SCALE).astype(BF16)
        dcq_ref[0] = _unstack_stat(rsum, heads)

    qblk = pl.BlockSpec((b, LANES), lambda p, i: (i, p))
    stat = pl.BlockSpec((1, b, LANES), lambda p, i: (p, i, 0))
    crow = _resident((1, 8, s), lambda p, i: (p, 0, 0))
    full = _resident((s, LANES), lambda p, i: (0, p))
    return pl.pallas_call(
        body, name=name, grid=(N_PAIRS, s // b),
        in_specs=[qblk, full, _resident((s, LANES), lambda p, i: (0, N_PAIRS + p)), stat, crow, qblk, qblk, stat,
                  pl.BlockSpec(memory_space=pltpu.SMEM)],
        out_specs=(qblk, full, full, crow, stat),
        out_shape=(jax.ShapeDtypeStruct((s, MIX_W), BF16), jax.ShapeDtypeStruct((s, MIX_W), F32), jax.ShapeDtypeStruct((s, MIX_W), F32),
                   jax.ShapeDtypeStruct((N_PAIRS, 8, s), F32), jax.ShapeDtypeStruct((N_PAIRS, s, LANES), F32)),
        scratch_shapes=[pltpu.VMEM((2 * b, LANES), F32)],
        compiler_params=_cparams(("parallel", "arbitrary")),
    )(proj, kv, kv, c_col, c_row, out32, dmerged, lse, first)


MEM_TQ = 1024


def _mem_fwd(proj, q_col_block, mkv, *, name):
    s = proj.shape[0]
    tq = _pick(s, (MEM_TQ, 128))
    n_mem = mkv.shape[0]

    def body(q_ref, mkv_ref, o_ref):
        heads = _head_lanes(tq)
        for pp in range(MEM_W // LANES):
            cols = slice(pp * LANES, (pp + 1) * LANES)
            qv = q_ref[:, cols] * QK_SCALE
            mk = mkv_ref[:, pp * LANES:(pp + 1) * LANES]
            mv = mkv_ref[:, MEM_W + pp * LANES:MEM_W + (pp + 1) * LANES]
            o_sel = None
            for hh in range(2):
                qm = jnp.where(heads[hh], qv, jnp.zeros_like(qv))
                sc = _dot(qm, mk, NT)
                p = jnp.exp(sc - jnp.max(sc, axis=1, keepdims=True))
                p = p / jnp.sum(p, axis=1, keepdims=True)
                out = _dot(p.astype(BF16), mv, NN)
                o_sel = out if hh == 0 else jnp.where(heads[0], o_sel, out)
            o_ref[:, cols] = o_sel.astype(BF16)

    return pl.pallas_call(
        body, name=name, grid=(s // tq,),
        in_specs=[pl.BlockSpec((tq, MEM_W), lambda i: (i, q_col_block)), pl.BlockSpec((n_mem, 2 * MEM_W), lambda i: (0, 0))],
        out_specs=pl.BlockSpec((tq, MEM_W), lambda i: (i, 0)),
        out_shape=jax.ShapeDtypeStruct((s, MEM_W), BF16),
        compiler_params=_cparams(("parallel",)),
    )(proj, mkv)


def _mem_bwd(proj, q_col_block, mkv, dmerged, *, name):
    s = proj.shape[0]
    tq = _pick(s, (MEM_TQ, 128))
    n_mem = mkv.shape[0]

    def body(q_ref, mkv_ref, do_ref, dq_ref, dmkv_ref):
        i = pl.program_id(0)

        @pl.when(i == 0)
        def _():
            dmkv_ref[...] = jnp.zeros_like(dmkv_ref)

        heads = _head_lanes(tq)
        for pp in range(MEM_W // LANES):
            cols = slice(pp * LANES, (pp + 1) * LANES)
            vcols = slice(MEM_W + pp * LANES, MEM_W + (pp + 1) * LANES)
            qv = q_ref[:, cols] * QK_SCALE
            dov = do_ref[:, cols]
            mk = mkv_ref[:, cols]
            mv = mkv_ref[:, vcols]
            dq_sel = None
            for hh in range(2):
                qm = jnp.where(heads[hh], qv, jnp.zeros_like(qv))
                dom = jnp.where(heads[hh], dov, jnp.zeros_like(dov))
                sc = _dot(qm, mk, NT)
                p = jnp.exp(sc - jnp.max(sc, axis=1, keepdims=True))
                p = p / jnp.sum(p, axis=1, keepdims=True)
                dp = _dot(dom, mv, NT)
                ds = p * (dp - jnp.sum(p * dp, axis=1, keepdims=True))
                dsb = ds.astype(BF16)
                dq = _dot(dsb, mk, NN)
                dmkv_ref[:, cols] += _dot(dsb, qm, TN)
                dmkv_ref[:, vcols] += _dot(p.astype(BF16), dom, TN)
                dq_sel = dq if hh == 0 else jnp.where(heads[0], dq_sel, dq)
            dq_ref[:, cols] = (dq_sel * QK_SCALE).astype(BF16)

    return pl.pallas_call(
        body, name=name, grid=(s // tq,),
        in_specs=[pl.BlockSpec((tq, MEM_W), lambda i: (i, q_col_block)), pl.BlockSpec((n_mem, 2 * MEM_W), lambda i: (0, 0)),
                  pl.BlockSpec((tq, MEM_W), lambda i: (i, MIX_W // MEM_W))],
        out_specs=(pl.BlockSpec((tq, MEM_W), lambda i: (i, 0)), pl.BlockSpec((n_mem, 2 * MEM_W), lambda i: (0, 0))),
        out_shape=(jax.ShapeDtypeStruct((s, MEM_W), BF16), jax.ShapeDtypeStruct((n_mem, 2 * MEM_W), F32)),
        compiler_params=_cparams(("arbitrary",)),
    )(proj, mkv, dmerged)


GATE_TB = 256


def _split3_dot(tri_bf16, x):
    x1 = x.astype(BF16)
    r1 = x - x1.astype(F32)
    x2 = r1.astype(BF16)
    x3 = (r1 - x2.astype(F32)).astype(BF16)
    return _dot(tri_bf16, x1, NN) + _dot(tri_bf16, x2, NN) + _dot(tri_bf16, x3, NN)


def _gate_fwd(f, b, *, name):
    s = f.shape[0]
    tb = _pick(s, (GATE_TB, 128))

    def body(f_ref, b_ref, c_ref, carry_ref):
        i = pl.program_id(0)

        @pl.when(i == 0)
        def _():
            carry_ref[...] = jnp.zeros_like(carry_ref)

        x = f_ref[...] + b_ref[...]
        lf = jnp.minimum(x, 0.0) - jnp.log1p(jnp.exp(-jnp.abs(x)))
        row = lax.broadcasted_iota(jnp.int32, (tb, tb), 0)
        col = lax.broadcasted_iota(jnp.int32, (tb, tb), 1)
        lower = (col <= row).astype(BF16)
        c = carry_ref[...] + _split3_dot(lower, lf)
        c_ref[...] = c
        carry_ref[...] = c[tb - 1:tb, :]

    return pl.pallas_call(
        body, name=name, grid=(s // tb,),
        in_specs=[pl.BlockSpec((tb, LANES), lambda i: (i, 0)), pl.BlockSpec((1, LANES), lambda i: (0, 0))],
        out_specs=pl.BlockSpec((tb, LANES), lambda i: (i, 0)),
        out_shape=jax.ShapeDtypeStruct((s, LANES), F32),
        scratch_shapes=[pltpu.VMEM((1, LANES), F32)],
        compiler_params=_cparams(("arbitrary",)),
    )(f, b)


def _gate_bwd(f, b, dc, *, name):
    s = f.shape[0]
    tb = _pick(s, (GATE_TB, 128))
    nb = s // tb

    def body(f_ref, b_ref, dc_ref, df_ref, db_ref, carry_ref):
        i = pl.program_id(0)

        @pl.when(i == 0)
        def _():
            carry_ref[...] = jnp.zeros_like(carry_ref)
            db_ref[...] = jnp.zeros_like(db_ref)

        row = lax.broadcasted_iota(jnp.int32, (tb, tb), 0)
        col = lax.broadcasted_iota(jnp.int32, (tb, tb), 1)
        upper = (col >= row).astype(BF16)
        dlf = carry_ref[...] + _split3_dot(upper, dc_ref[...])
        carry_ref[...] = dlf[0:1, :]
        x = f_ref[...] + b_ref[...]
        e = jnp.exp(-jnp.abs(x))
        one_minus_sig = jnp.where(x >= 0.0, e, 1.0) / (1.0 + e)
        df = dlf * one_minus_sig
        df_ref[...] = df
        db_ref[...] += jnp.sum(df, axis=0, keepdims=True)

    rev = pl.BlockSpec((tb, LANES), lambda i: (nb - 1 - i, 0))
    vec = pl.BlockSpec((1, LANES), lambda i: (0, 0))
    return pl.pallas_call(
        body, name=name, grid=(nb,), in_specs=[rev, vec, rev], out_specs=(rev, vec),
        out_shape=(jax.ShapeDtypeStruct((s, LANES), F32), jax.ShapeDtypeStruct((1, LANES), F32)),
        scratch_shapes=[pltpu.VMEM((1, LANES), F32)],
        compiler_params=_cparams(("arbitrary",)),
    )(f, b, dc)


def _all_gather(x, *, name):
    r, cdim = x.shape

    def body(x_ref, out_ref, send_sems, recv_sems, local_sem):
        mx, my, mc = lax.axis_index("x"), lax.axis_index("y"), lax.axis_index("c")
        me, sibling = (mx, my, mc), (mx, my, 1 - mc)
        chips = [(1 - mx, my), (mx, 1 - my), (1 - mx, 1 - my)]

        def rows(px, py, pc):
            return out_ref.at[pl.ds((4 * px + 2 * py + pc) * r, r), :]

        def copy(k, block, to, src=None):
            return pltpu.make_async_remote_copy(
                src_ref=rows(*block) if src is None else src, dst_ref=rows(*block),
                send_sem=send_sems.at[k], recv_sem=recv_sems.at[k], device_id=to, device_id_type=MESH)

        mine = pltpu.make_async_copy(x_ref, rows(*me), local_sem)
        mine.start()
        first = [copy(0, me, sibling, src=x_ref)]
        first += [copy(1 + j, me, (*chip, mc), src=x_ref) for j, chip in enumerate(chips)]
        for cp in first:
            cp.start()
        passed = [copy(4 + j, (*chip, mc), sibling) for j, chip in enumerate(chips)]
        for j, chip in enumerate(chips):
            copy(1 + j, (*chip, mc), me).wait_recv()
            passed[j].start()
        copy(0, sibling, me).wait_recv()
        for j, chip in enumerate(chips):
            copy(4 + j, (*chip, 1 - mc), me).wait_recv()
        for cp in first + passed:
            cp.wait_send()
        mine.wait()

    return pl.pallas_call(
        body, name=name,
        in_specs=[pl.BlockSpec(memory_space=pl.ANY)], out_specs=pl.BlockSpec(memory_space=pl.ANY),
        out_shape=jax.ShapeDtypeStruct((N_DEV * r, cdim), x.dtype),
        scratch_shapes=[pltpu.SemaphoreType.DMA((7,)), pltpu.SemaphoreType.DMA((7,)), pltpu.SemaphoreType.DMA],
    )(x)


N_CHIPS = N_DEV // 2


def _sibling_exchange(g, small, *, name):
    def body(g_ref, s_ref, out_ref, sout_ref, send_sems, recv_sems, local_sem):
        mx, my, mc = lax.axis_index("x"), lax.axis_index("y"), lax.axis_index("c")
        me = 4 * mx + 2 * my + mc
        sibling = (mx, my, 1 - mc)
        mine = pltpu.make_async_copy(s_ref, sout_ref.at[me], local_sem)
        mine.start()
        copies = []
        for k in range(N_CHIPS):
            copies.append(pltpu.make_async_remote_copy(
                src_ref=g_ref.at[2 * k + (1 - mc)], dst_ref=out_ref.at[k],
                send_sem=send_sems.at[k], recv_sem=recv_sems.at[k], device_id=sibling, device_id_type=MESH))
        for mask in range(1, N_DEV):
            px = 1 - mx if mask & 4 else mx
            py = 1 - my if mask & 2 else my
            pc = 1 - mc if mask & 1 else mc
            copies.append(pltpu.make_async_remote_copy(
                src_ref=s_ref, dst_ref=sout_ref.at[me],
                send_sem=send_sems.at[N_CHIPS - 1 + mask], recv_sem=recv_sems.at[N_CHIPS - 1 + mask], device_id=(px, py, pc), device_id_type=MESH))
        for cp in copies:
            cp.start()
        for cp in copies:
            cp.wait()
        mine.wait()

    n_sem = N_CHIPS + N_DEV - 1
    return pl.pallas_call(
        body, name=name,
        in_specs=[pl.BlockSpec(memory_space=pl.ANY), pl.BlockSpec(memory_space=pl.ANY)],
        out_specs=(pl.BlockSpec(memory_space=pl.ANY), pl.BlockSpec(memory_space=pl.ANY)),
        out_shape=(jax.ShapeDtypeStruct((N_CHIPS,) + g.shape[1:], g.dtype), jax.ShapeDtypeStruct((N_DEV,) + small.shape, small.dtype)),
        scratch_shapes=[pltpu.SemaphoreType.DMA((n_sem,)), pltpu.SemaphoreType.DMA((n_sem,)), pltpu.SemaphoreType.DMA],
    )(g, small)


def _pair_sum(g, theirs, *, name):
    _, r, cdim = g.shape
    tr = _pick(r, (4 * PACK_BLOCK_ROWS, 2 * PACK_BLOCK_ROWS, PACK_BLOCK_ROWS, 512, 256, 128, 64, 32, 16))

    def body(g0_ref, g1_ref, t_ref, o_ref):
        south = lax.axis_index("c") == 0
        mine = jnp.where(south, g0_ref[...].astype(F32), g1_ref[...].astype(F32))
        o_ref[...] = (mine + t_ref[...].astype(F32)).astype(o_ref.dtype)

    return pl.pallas_call(
        body, name=name, grid=(N_CHIPS, r // tr),
        in_specs=[pl.BlockSpec((1, tr, cdim), lambda k, i: (2 * k, i, 0)), pl.BlockSpec((1, tr, cdim), lambda k, i: (2 * k + 1, i, 0)),
                  pl.BlockSpec((1, tr, cdim), lambda k, i: (k, i, 0))],
        out_specs=pl.BlockSpec((1, tr, cdim), lambda k, i: (k, i, 0)),
        out_shape=jax.ShapeDtypeStruct((N_CHIPS, r, cdim), g.dtype),
        compiler_params=_cparams(("parallel", "parallel")),
    )(g, g, theirs)


def _chip_exchange(sums, *, name):
    def body(s_ref, out_ref, send_sems, recv_sems, local_sem):
        mx, my, mc = lax.axis_index("x"), lax.axis_index("y"), lax.axis_index("c")
        chip = 2 * mx + my
        mine = pltpu.make_async_copy(s_ref.at[chip], out_ref.at[chip], local_sem)
        mine.start()
        copies = []
        for mask in range(1, N_CHIPS):
            px = 1 - mx if mask & 2 else mx
            py = 1 - my if mask & 1 else my
            copies.append(pltpu.make_async_remote_copy(
                src_ref=s_ref.at[2 * px + py], dst_ref=out_ref.at[chip],
                send_sem=send_sems.at[mask - 1], recv_sem=recv_sems.at[mask - 1], device_id=(px, py, mc), device_id_type=MESH))
        for cp in copies:
            cp.start()
        for cp in copies:
            cp.wait()
        mine.wait()

    return pl.pallas_call(
        body, name=name,
        in_specs=[pl.BlockSpec(memory_space=pl.ANY)], out_specs=pl.BlockSpec(memory_space=pl.ANY),
        out_shape=jax.ShapeDtypeStruct(sums.shape, sums.dtype),
        scratch_shapes=[pltpu.SemaphoreType.DMA((N_CHIPS - 1,)), pltpu.SemaphoreType.DMA((N_CHIPS - 1,)), pltpu.SemaphoreType.DMA],
    )(sums)


def _adamw(parts, w, m, v, *, name):
    r, cdim = w.shape
    n_parts = parts.shape[0]
    tr = _pick(r, (4 * PACK_BLOCK_ROWS, 2 * PACK_BLOCK_ROWS, PACK_BLOCK_ROWS, 512, 256, 128, 64, 32, 16, 8))
    c1 = 1.0 / (1.0 - ADAM_B1 ** ADAM_STEP)
    c2 = 1.0 / (1.0 - ADAM_B2 ** ADAM_STEP)

    def body(p_ref, w_ref, m_ref, v_ref, g_ref, d_ref, nm_ref, nv_ref):
        g = p_ref[0].astype(F32)
        for part in range(1, n_parts):
            g = g + p_ref[part].astype(F32)
        mn = ADAM_B1 * m_ref[...] + (1.0 - ADAM_B1) * g
        vn = ADAM_B2 * v_ref[...] + (1.0 - ADAM_B2) * (g * g)
        g_ref[...] = g
        nm_ref[...] = mn
        nv_ref[...] = vn
        d_ref[...] = -ADAM_LR * ((mn * c1) / (jnp.sqrt(vn * c2) + ADAM_EPS) + ADAM_WD * w_ref[...])

    blk = pl.BlockSpec((tr, cdim), lambda i: (i, 0))
    shape = jax.ShapeDtypeStruct((r, cdim), F32)
    return pl.pallas_call(
        body, name=name, grid=(r // tr,),
        in_specs=[pl.BlockSpec((n_parts, tr, cdim), lambda i: (0, i, 0)), blk, blk, blk],
        out_specs=(blk, blk, blk, blk), out_shape=(shape, shape, shape, shape),
        compiler_params=_cparams(("parallel",)),
    )(parts, w, m, v)


def _rows_of(shape):
    n = math.prod(shape)
    assert n % LANES == 0, shape
    rows = n // LANES
    return -(-rows // PACK_ROW_ALIGN) * PACK_ROW_ALIGN


def _layout(shard_shapes):
    out, off = {}, 0
    for name, _ in SHARDED:
        rows = _rows_of(shard_shapes[name])
        out[name] = (off, rows, tuple(shard_shapes[name]))
        off += rows
    return out, -(-off // PACK_BLOCK_ROWS) * PACK_BLOCK_ROWS


def _pack_shards(layout, total, arrays, dtype):
    parts = []
    for name, _ in SHARDED:
        _, rows, _ = layout[name]
        flat = arrays[name].astype(dtype).reshape(-1, LANES)
        parts.append(jnp.pad(flat, ((0, rows - flat.shape[0]), (0, 0))))
    used = sum(p.shape[0] for p in parts)
    if total > used:
        parts.append(jnp.zeros((total - used, LANES), dtype))
    return jnp.concatenate(parts, axis=0)


def _unpack_shard(layout, flat, name):
    off, _, shape = layout[name]
    n = math.prod(shape) // LANES
    return flat[off:off + n].reshape(shape)


def _unpack_full(layout, gathered, name, axis):
    off, _, shape = layout[name]
    n = math.prod(shape) // LANES
    blocks = gathered[:, off:off + n].reshape((N_DEV,) + shape)
    blocks = jnp.moveaxis(blocks, 0, axis)
    return blocks.reshape(shape[:axis] + (N_DEV * shape[axis],) + shape[axis + 1:])


def _pack_full(layout, total, grads):
    parts = []
    for name, axis in SHARDED:
        _, rows, shape = layout[name]
        g = grads[name]
        blocks = g.reshape(shape[:axis] + (N_DEV, shape[axis]) + shape[axis + 1:])
        blocks = jnp.moveaxis(blocks, axis, 0).reshape(N_DEV, -1, LANES)
        parts.append(jnp.pad(blocks, ((0, 0), (0, rows - blocks.shape[1]), (0, 0))))
    used = sum(p.shape[1] for p in parts)
    if total > used:
        parts.append(jnp.zeros((N_DEV, total - used, LANES), F32))
    return jnp.concatenate(parts, axis=1)


def _pad_lanes(a):
    return jnp.pad(a, ((0, 0), (0, LANES - a.shape[1])))


def _pair_layouts(c):
    s = c.shape[0]
    by_pair = c.T.reshape(N_PAIRS, 2, s)
    c_col = jnp.repeat(by_pair.transpose(0, 2, 1), HEAD_DIM, axis=2)
    c_row = jnp.pad(by_pair, ((0, 0), (0, 6), (0, 0)))
    return c_col, c_row


def _key_norm_bound(k):
    norms = jnp.sqrt(jnp.max(jnp.sum(jnp.square(k.astype(F32)).reshape(k.shape[0], N_MIX_HEADS, HEAD_DIM), axis=2), axis=0))
    rows = jnp.pad(norms.reshape(N_PAIRS, 2), ((0, 0), (0, 6)))
    return jnp.broadcast_to(rows[:, :, None], (N_PAIRS, 8, LANES))


def _forward_backward(x, mem, target, wts, small):
    n_a = wts["w_in_a"].shape[0]
    n_b = wts["w_in_b"].shape[0]
    depth = n_a + n_b
    w_kv = wts["w_kv_shared"]
    w_kv_kv = w_kv[:, :2 * MIX_W]
    w_kv_f = _pad_lanes(w_kv[:, 2 * MIX_W:])
    b_f = _pad_lanes(small["b_f"].reshape(1, -1))

    saved = []
    shared = None
    h = x
    for l in range(depth):
        is_a = l < n_a
        if l == n_a:
            hs = _rmsnorm_fwd(h, small["kv_norm_g"], name="kv_norm")
            kv = _mm(hs, w_kv_kv, name="kv_proj", out_dtype=BF16)
            f = _mm(hs, w_kv_f, name="gate_proj")
            c = _gate_fwd(f, b_f, name="gate_cumsum")
            c_col, c_row = _pair_layouts(c[:, :N_MIX_HEADS])
            shared = dict(h=h, hs=hs, kv=kv, f=f, c_col=c_col, c_row=c_row, k_max=_key_norm_bound(kv[:, :MIX_W]))
        hn = _rmsnorm_fwd(h, small["norm1_g"][l], name=f"norm1_{l}")
        memn = _rmsnorm_fwd(mem, small["mem_norm_g"][l], name=f"mem_norm_{l}")
        mkv = _mm(memn, wts["w_mem_kv"][l], name=f"mem_kv_{l}", out_dtype=BF16)
        if is_a:
            proj = _mm(hn, wts["w_in_a"][l], name=f"in_proj_{l}", out_dtype=BF16)
            mix, stat, first = _sb_fwd(proj, name=f"sb_fwd_{l}")
            mix32 = None
            q_block = 3 * MIX_W // MEM_W
        else:
            proj = _mm(hn, wts["w_in_b"][l - n_a], name=f"in_proj_{l}", out_dtype=BF16)
            mix, mix32, stat, first = _fox_fwd(proj, shared["kv"], shared["c_col"], shared["c_row"], shared["k_max"], name=f"fox_fwd_{l}")
            q_block = MIX_W // MEM_W
        mem_out = _mem_fwd(proj, q_block, mkv, name=f"mem_fwd_{l}")
        merged = jnp.concatenate([mix, mem_out], axis=1)
        h_mid = _mm(merged, wts["w_o"][l], name=f"o_proj_{l}", res=h)
        h2n = _rmsnorm_fwd(h_mid, small["norm2_g"][l], name=f"norm2_{l}")
        u, act = _mm(h2n, wts["w_mlp1"][l], name=f"mlp1_{l}", epilogue="relu2")
        h_out = _mm(act, wts["w_mlp2"][l], name=f"mlp2_{l}", res=h_mid)
        saved.append(dict(h=h, hn=hn, memn=memn, mkv=mkv, proj=proj, stat=stat, first=first, mix32=mix32, merged=merged, h_mid=h_mid, h2n=h2n, u=u, act=act,
                          q_block=q_block))
        h = h_out

    loss, dh, dg_final = _final_loss(h, target, small["final_norm_g"], name="final_loss")

    g_w = {k: [None] * wts[k].shape[0] for k in ("w_in_a", "w_in_b", "w_mem_kv", "w_o", "w_mlp1", "w_mlp2")}
    g_n = {k: [None] * depth for k in ("norm1_g", "mem_norm_g", "norm2_g")}
    dk_sh = dv_sh = dc_sh = dcq_sh = None
    for l in reversed(range(depth)):
        sv = saved[l]
        is_a = l < n_a
        du = _mm(dh, wts["w_mlp2"][l], name=f"d_act_{l}", trans_b=True, epilogue="drelu2", u=sv["u"], out_dtype=BF16)
        g_w["w_mlp2"][l] = _mm_tn(sv["act"], dh, name=f"dw_mlp2_{l}")
        g_w["w_mlp1"][l] = _mm_tn(sv["h2n"], du, name=f"dw_mlp1_{l}")
        dh_mid, g_n["norm2_g"][l] = _mm(du, wts["w_mlp1"][l], name=f"d_h2n_{l}", trans_b=True, norm=(sv["h_mid"], small["norm2_g"][l], dh))
        dmerged = _mm(dh_mid, wts["w_o"][l], name=f"d_merged_{l}", trans_b=True, out_dtype=BF16)
        g_w["w_o"][l] = _mm_tn(sv["merged"], dh_mid, name=f"dw_o_{l}")
        dqm, dmkv = _mem_bwd(sv["proj"], sv["q_block"], sv["mkv"], dmerged, name=f"mem_bwd_{l}")
        if is_a:
            dq, dk, dv = _sb_bwd(sv["proj"], dmerged, sv["stat"], sv["first"], name=f"sb_bwd_{l}")
            dproj = jnp.concatenate([dq, dk.astype(BF16), dv.astype(BF16), dqm], axis=1)
            w_in, key, idx = wts["w_in_a"][l], "w_in_a", l
        else:
            dq, dk, dv, dc, dcq = _fox_bwd(sv["proj"], shared["kv"], shared["c_col"], shared["c_row"], sv["mix32"], dmerged, sv["stat"],
                                           sv["first"], name=f"fox_bwd_{l}")
            dk_sh = dk if dk_sh is None else dk_sh + dk
            dv_sh = dv if dv_sh is None else dv_sh + dv
            dc_sh = dc if dc_sh is None else dc_sh + dc
            dcq_sh = dcq if dcq_sh is None else dcq_sh + dcq
            dproj = jnp.concatenate([dq, dqm], axis=1)
            w_in, key, idx = wts["w_in_b"][l - n_a], "w_in_b", l - n_a
        g_w[key][idx] = _mm_tn(sv["hn"], dproj, name=f"dw_in_{l}")
        dh, g_n["norm1_g"][l] = _mm(dproj, w_in, name=f"d_hn_{l}", trans_b=True, norm=(sv["h"], small["norm1_g"][l], dh_mid))
        g_w["w_mem_kv"][l] = _mm_tn(sv["memn"], dmkv, name=f"dw_mem_kv_{l}")
        _, g_n["mem_norm_g"][l] = _mm(dmkv, wts["w_mem_kv"][l], name=f"d_memn_{l}", trans_b=True, norm=(mem, small["mem_norm_g"][l], None))
        if l == n_a:
            s_len = x.shape[0]
            dc_query = jnp.stack([dcq_sh[:, :, 0], dcq_sh[:, :, HEAD_DIM]], axis=-1).transpose(1, 0, 2).reshape(s_len, N_MIX_HEADS)
            dc_tok = _pad_lanes(dc_sh[:, :2, :].reshape(N_MIX_HEADS, s_len).T + dc_query)
            df, db = _gate_bwd(shared["f"], b_f, dc_tok, name="gate_bwd")
            dkv = jnp.concatenate([dk_sh, dv_sh], axis=1)
            dw_kv_kv = _mm_tn(shared["hs"], dkv, name="dw_kv")
            dw_kv_f = _mm_tn(shared["hs"], df, name="dw_gate")
            dhs = _mm(dkv, w_kv_kv, name="d_hs_kv", trans_b=True)
            dh, dg_kv = _mm(df, w_kv_f, name="d_hs_gate", trans_b=True, res=dhs, norm=(shared["h"], small["kv_norm_g"], dh))
            g_kv = jnp.concatenate([dw_kv_kv, dw_kv_f[:, :N_KV_F]], axis=1)

    grads = {k: jnp.stack(v) for k, v in g_w.items()}
    grads["w_kv_shared"] = g_kv
    d = x.shape[1]
    small_g = dict(
        norm1_g=jnp.concatenate(g_n["norm1_g"], axis=0), mem_norm_g=jnp.concatenate(g_n["mem_norm_g"], axis=0),
        norm2_g=jnp.concatenate(g_n["norm2_g"], axis=0), kv_norm_g=dg_kv.reshape(d), b_f=db[0, :N_KV_F], final_norm_g=dg_final.reshape(d))
    return loss, dh, grads, small_g


def _pack_small(vals, d):
    rows = []
    for name in REPLICATED:
        a = vals[name].astype(F32)
        if name == "b_f":
            a = jnp.pad(a, (0, d - a.shape[0]))
        rows.append(a.reshape(-1, d))
    packed = jnp.concatenate(rows, axis=0)
    pad = -packed.shape[0] % 8
    return jnp.pad(packed, ((0, pad), (0, 0)))


def _unpack_small(packed, shapes):
    out, off = {}, 0
    for name in REPLICATED:
        shape = shapes[name]
        if name == "b_f":
            out[name] = packed[off, :shape[0]]
            off += 1
        else:
            n = math.prod(shape) // packed.shape[1]
            out[name] = packed[off:off + n].reshape(shape)
            off += n
    return out


def kernel(x, mem, norm1_g, w_in_a, w_in_b, w_mem_kv, mem_norm_g, w_o, norm2_g, w_mlp1, w_mlp2, kv_norm_g, w_kv_shared, b_f, final_norm_g, loss_target, m_norm1_g, m_w_in_a, m_w_in_b, m_w_mem_kv, m_mem_norm_g, m_w_o, m_norm2_g, m_w_mlp1, m_w_mlp2, m_kv_norm_g, m_w_kv_shared, m_b_f, m_final_norm_g, v_norm1_g, v_w_in_a, v_w_in_b, v_w_mem_kv, v_mem_norm_g, v_w_o, v_norm2_g, v_w_mlp1, v_w_mlp2, v_kv_norm_g, v_w_kv_shared, v_b_f, v_final_norm_g):
    w = dict(norm1_g=norm1_g, w_in_a=w_in_a, w_in_b=w_in_b, w_mem_kv=w_mem_kv, mem_norm_g=mem_norm_g, w_o=w_o, norm2_g=norm2_g,
             w_mlp1=w_mlp1, w_mlp2=w_mlp2, kv_norm_g=kv_norm_g, w_kv_shared=w_kv_shared, b_f=b_f, final_norm_g=final_norm_g)
    m = dict(norm1_g=m_norm1_g, w_in_a=m_w_in_a, w_in_b=m_w_in_b, w_mem_kv=m_w_mem_kv, mem_norm_g=m_mem_norm_g, w_o=m_w_o,
             norm2_g=m_norm2_g, w_mlp1=m_w_mlp1, w_mlp2=m_w_mlp2, kv_norm_g=m_kv_norm_g, w_kv_shared=m_w_kv_shared, b_f=m_b_f,
             final_norm_g=m_final_norm_g)
    v = dict(norm1_g=v_norm1_g, w_in_a=v_w_in_a, w_in_b=v_w_in_b, w_mem_kv=v_w_mem_kv, mem_norm_g=v_mem_norm_g, w_o=v_w_o,
             norm2_g=v_norm2_g, w_mlp1=v_w_mlp1, w_mlp2=v_w_mlp2, kv_norm_g=v_kv_norm_g, w_kv_shared=v_w_kv_shared, b_f=v_b_f,
             final_norm_g=v_final_norm_g)
    d = x.shape[-1]
    layout, total = _layout({name: w[name].shape for name, _ in SHARDED})

    gathered = _all_gather(_pack_shards(layout, total, w, BF16), name="gather_weights").reshape(N_DEV, total, LANES)
    wts = {name: _unpack_full(layout, gathered, name, axis) for name, axis in SHARDED}
    small = {name: w[name] for name in REPLICATED}

    loss, grad_x, grads, small_g = _forward_backward(x[0], mem[0], loss_target[0], wts, small)

    g_packed = _pack_full(layout, total, grads).astype(BF16)
    theirs, small_parts = _sibling_exchange(g_packed, _pack_small(small_g, d), name="exchange_siblings")
    parts = _chip_exchange(_pair_sum(g_packed, theirs, name="pair_sum"), name="exchange_chips")
    g_flat, d_flat, m_flat, v_flat = _adamw(parts, _pack_shards(layout, total, w, F32), _pack_shards(layout, total, m, F32),
                                            _pack_shards(layout, total, v, F32), name="adamw_sharded")
    gs, ds_, ms, vs = _adamw(small_parts, _pack_small(w, d), _pack_small(m, d), _pack_small(v, d), name="adamw_replicated")

    shapes = {name: w[name].shape for name in REPLICATED}
    out_g, out_d, out_m, out_v = {}, {}, {}, {}
    for flat, small_flat, out in ((g_flat, gs, out_g), (d_flat, ds_, out_d), (m_flat, ms, out_m), (v_flat, vs, out_v)):
        for name, _ in SHARDED:
            out[name] = _unpack_shard(layout, flat, name)
        out.update(_unpack_small(small_flat, shapes))

    loss_total = lax.psum(loss[0, 0], ("x", "y", "c"))
    return (loss_total, grad_x[None], *[out_g[n] for n in WEIGHT_ORDER], *[out_d[n] for n in WEIGHT_ORDER],
            *[out_m[n] for n in WEIGHT_ORDER], *[out_v[n] for n in WEIGHT_ORDER])
```

```python
import functools
import math

import jax
import jax.numpy as jnp
from jax import lax
from jax.experimental import pallas as pl
from jax.experimental.pallas import tpu as pltpu

F32 = jnp.float32
BF16 = jnp.bfloat16

N_DEV = 8
HEAD_DIM = 64
N_MIX_HEADS = 8
N_MEM_HEADS = 4
MIX_W = N_MIX_HEADS * HEAD_DIM
MEM_W = N_MEM_HEADS * HEAD_DIM
N_PAIRS = N_MIX_HEADS // 2
LANES = 128
SB_B = 256
FOX_B = 512
CUM_SUB = 256
SB_CUT = 64.0
FOX_CUT = 45.0
EPS = 1e-6
NEG_INF = -1e30
QK_SCALE = 1.0 / math.sqrt(HEAD_DIM)
LOG2E = 1.4426950408889634
N_KV_F = 8

ADAM_LR = 0.001
ADAM_B1 = 0.9
ADAM_B2 = 0.999
ADAM_EPS = 1e-08
ADAM_WD = 0.01
ADAM_STEP = 10

VMEM_LIMIT = 56 * 1024 * 1024
PACK_ROW_ALIGN = 16
PACK_BLOCK_ROWS = 1024

MESH = pl.DeviceIdType.MESH

SHARDED = (("w_in_a", 2), ("w_in_b", 1), ("w_mem_kv", 1), ("w_o", 2), ("w_mlp1", 2), ("w_mlp2", 1), ("w_kv_shared", 1))
REPLICATED = ("norm1_g", "mem_norm_g", "norm2_g", "kv_norm_g", "b_f", "final_norm_g")
WEIGHT_ORDER = ("norm1_g", "w_in_a", "w_in_b", "w_mem_kv", "mem_norm_g", "w_o", "norm2_g", "w_mlp1", "w_mlp2",
                "kv_norm_g", "w_kv_shared", "b_f", "final_norm_g")


def _cparams(sem=None):
    return pltpu.CompilerParams(dimension_semantics=sem, vmem_limit_bytes=VMEM_LIMIT)


def _resident(shape, index_map):
    return pl.BlockSpec(shape, index_map, pipeline_mode=pl.Buffered(1))


def _pick(n, cands):
    for c in cands:
        if c <= n and n % c == 0:
            return c
    return n


def _dot(a, b, dims):
    return lax.dot_general(a, b, (dims, ((), ())), preferred_element_type=F32)


NN = ((1,), (0,))
NT = ((1,), (1,))
TN = ((0,), (0,))


MM_CHUNK = 512


def _mm(a, b, *, name, trans_b=False, out_dtype=F32, res=None, epilogue=None, u=None, norm=None):
    m, k = a.shape
    n = b.shape[0] if trans_b else b.shape[1]
    tm = _pick(m, (512, 256, 128))
    tn = _pick(n, (MM_CHUNK, 384, 256, 128))
    has_dres = norm is not None and norm[2] is not None

    def body(*refs):
        a_ref, b_ref = refs[0], refs[1]
        pos = 2
        res_ref = u_ref = x_ref = g_ref = dres_ref = None
        if res is not None:
            res_ref = refs[pos]
            pos += 1
        if u is not None:
            u_ref = refs[pos]
            pos += 1
        if norm is not None:
            x_ref, g_ref = refs[pos], refs[pos + 1]
            pos += 2
            if has_dres:
                dres_ref = refs[pos]
                pos += 1
        outs = refs[pos:]
        av = a_ref[...].astype(BF16)
        prods = []
        for c in range(n // tn):
            cols = slice(c * tn, (c + 1) * tn)
            if trans_b:
                acc = _dot(av, b_ref[cols, :].astype(BF16), NT)
            else:
                acc = _dot(av, b_ref[:, cols].astype(BF16), NN)
            if res_ref is not None:
                acc = res_ref[:, cols] + acc
            if norm is not None:
                prods.append(acc)
            elif epilogue == "relu2":
                outs[0][:, cols] = acc.astype(BF16)
                r = jnp.maximum(acc, 0.0)
                outs[1][:, cols] = (r * r).astype(BF16)
            elif epilogue == "drelu2":
                outs[0][:, cols] = (acc * (2.0 * jnp.maximum(u_ref[:, cols], 0.0))).astype(out_dtype)
            else:
                outs[0][:, cols] = acc.astype(out_dtype)
        if norm is not None:
            dyv = prods[0] if len(prods) == 1 else jnp.concatenate(prods, axis=1)
            xf = x_ref[...]
            r = lax.rsqrt(jnp.mean(xf * xf, axis=-1, keepdims=True) + EPS)
            xh = xf * r
            dyg = dyv * g_ref[...]
            dx = r * (dyg - xh * jnp.mean(dyg * xh, axis=-1, keepdims=True))
            outs[0][...] = dx if dres_ref is None else dres_ref[...] + dx

            @pl.when(pl.program_id(0) == 0)
            def _():
                outs[1][...] = jnp.zeros_like(outs[1])

            outs[1][...] += jnp.sum(dyv * xh, axis=0, keepdims=True)

    row = pl.BlockSpec((tm, n), lambda i: (i, 0))
    vec = pl.BlockSpec((1, n), lambda i: (0, 0))
    in_specs = [pl.BlockSpec((tm, k), lambda i: (i, 0)), _resident(b.shape, lambda i: (0, 0))]
    args = [a, b]
    if res is not None:
        in_specs.append(row)
        args.append(res)
    if u is not None:
        in_specs.append(row)
        args.append(u)
    if norm is not None:
        in_specs += [row, vec] + ([row] if has_dres else [])
        args += [norm[0], norm[1].reshape(1, n)] + ([norm[2]] if has_dres else [])
        out_shape = (jax.ShapeDtypeStruct((m, n), F32), jax.ShapeDtypeStruct((1, n), F32))
        out_specs = (row, vec)
    elif epilogue == "relu2":
        out_shape = (jax.ShapeDtypeStruct((m, n), BF16), jax.ShapeDtypeStruct((m, n), BF16))
        out_specs = (row, row)
    else:
        out_shape = (jax.ShapeDtypeStruct((m, n), out_dtype),)
        out_specs = (row,)
    outs = pl.pallas_call(
        body, name=name, grid=(m // tm,), in_specs=in_specs, out_specs=out_specs, out_shape=out_shape,
        compiler_params=_cparams(("arbitrary",) if norm is not None else ("parallel",)),
    )(*args)
    return outs if (epilogue == "relu2" or norm is not None) else outs[0]


def _mm_tn(x, dy, *, name):
    m, k1 = x.shape
    n = dy.shape[1]
    t1 = _pick(k1, (1024, 896, 768, 512, 256, 128))
    tn = _pick(n, (1024, 896, 768, 512, 256, 128))
    tm = _pick(m, (2048, 1024, 512, 256, 128))
    nm = m // tm

    def body(x_ref, dy_ref, o_ref):
        mm = pl.program_id(2)

        @pl.when(mm == 0)
        def _():
            o_ref[...] = jnp.zeros_like(o_ref)

        o_ref[...] += _dot(x_ref[...].astype(BF16), dy_ref[...].astype(BF16), TN)

    return pl.pallas_call(
        body, name=name, grid=(k1 // t1, n // tn, nm),
        in_specs=[pl.BlockSpec((tm, t1), lambda i, j, mm: (mm, i)), pl.BlockSpec((tm, tn), lambda i, j, mm: (mm, j))],
        out_specs=pl.BlockSpec((t1, tn), lambda i, j, mm: (i, j)),
        out_shape=jax.ShapeDtypeStruct((k1, n), F32),
        compiler_params=_cparams(("parallel", "parallel", "arbitrary")),
    )(x, dy)


def _rmsnorm_fwd(x, g, *, name):
    s, d = x.shape
    tm = _pick(s, (1024, 512, 256, 128))

    def body(x_ref, g_ref, o_ref):
        xf = x_ref[...]
        r = lax.rsqrt(jnp.mean(xf * xf, axis=-1, keepdims=True) + EPS)
        o_ref[...] = (xf * r * g_ref[...]).astype(BF16)

    return pl.pallas_call(
        body, name=name, grid=(s // tm,),
        in_specs=[pl.BlockSpec((tm, d), lambda i: (i, 0)), pl.BlockSpec((1, d), lambda i: (0, 0))],
        out_specs=pl.BlockSpec((tm, d), lambda i: (i, 0)),
        out_shape=jax.ShapeDtypeStruct((s, d), BF16),
        compiler_params=_cparams(("parallel",)),
    )(x, g.reshape(1, d))


def _final_loss(h, target, g, *, name):
    s, d = h.shape
    tm = _pick(s, (512, 256, 128))

    def body(h_ref, t_ref, g_ref, loss_ref, dh_ref, dg_ref):
        i = pl.program_id(0)
        xf = h_ref[...]
        gv = g_ref[...]
        r = lax.rsqrt(jnp.mean(xf * xf, axis=-1, keepdims=True) + EPS)
        xh = xf * r
        err = xh * gv - t_ref[...]
        part = 0.5 * jnp.sum(jnp.mean(err * err, axis=-1, keepdims=True), axis=0, keepdims=True)
        dyv = err * (1.0 / d)
        dyg = dyv * gv
        dh_ref[...] = r * (dyg - xh * jnp.mean(dyg * xh, axis=-1, keepdims=True))

        @pl.when(i == 0)
        def _():
            dg_ref[...] = jnp.zeros_like(dg_ref)
            loss_ref[...] = jnp.zeros_like(loss_ref)

        dg_ref[...] += jnp.sum(dyv * xh, axis=0, keepdims=True)
        loss_ref[...] += jnp.broadcast_to(part, loss_ref.shape)

    row = pl.BlockSpec((tm, d), lambda i: (i, 0))
    vec = pl.BlockSpec((1, d), lambda i: (0, 0))
    return pl.pallas_call(
        body, name=name, grid=(s // tm,), in_specs=[row, row, vec],
        out_specs=(pl.BlockSpec((1, LANES), lambda i: (0, 0)), row, vec),
        out_shape=(jax.ShapeDtypeStruct((1, LANES), F32), jax.ShapeDtypeStruct((s, d), F32), jax.ShapeDtypeStruct((1, d), F32)),
        compiler_params=_cparams(("arbitrary",)),
    )(h, target, g.reshape(1, d))


def _head_lanes(rows):
    lane = lax.broadcasted_iota(jnp.int32, (rows, LANES), 1)
    return [lane < HEAD_DIM, lane >= HEAD_DIM]


def _tri(b, cmp):
    row = lax.broadcasted_iota(jnp.int32, (b, b), 0)
    col = lax.broadcasted_iota(jnp.int32, (b, b), 1)
    return cmp(row, col)


def _twice(mask):
    return jnp.concatenate([mask, mask], axis=0)


def _stack_heads(x, heads):
    zero = jnp.zeros_like(x)
    return jnp.concatenate([jnp.where(heads[0], x, zero), jnp.where(heads[1], x, zero)], axis=0)


def _unstack_heads(x2, heads):
    b = x2.shape[0] // 2
    return jnp.where(heads[0], x2[:b], x2[b:])


def _stack_stat(stat):
    return jnp.concatenate([stat[:, 0:1], stat[:, HEAD_DIM:HEAD_DIM + 1]], axis=0)


def _unstack_stat(col, heads):
    b = col.shape[0] // 2
    return jnp.where(heads[0], jnp.broadcast_to(col[:b], (b, LANES)), jnp.broadcast_to(col[b:], (b, LANES)))


def _tri_dot(x, tri_bf16):
    return _dot(x.astype(BF16), tri_bf16, NN)


def _prefix_sums(x, tri_bf16, inclusive):
    sub = tri_bf16.shape[0]
    outs, carry = [], None
    for c in range(x.shape[1] // sub):
        xs = x[:, c * sub:(c + 1) * sub]
        loc = _tri_dot(xs, tri_bf16)
        outs.append(loc if carry is None else loc + carry)
        tot = loc[:, sub - 1:sub] if inclusive else loc[:, sub - 1:sub] + xs[:, sub - 1:sub]
        carry = tot if carry is None else carry + tot
    return (outs[0] if len(outs) == 1 else jnp.concatenate(outs, axis=1)), carry


def _suffix_sums(x, tri_bf16):
    sub = tri_bf16.shape[0]
    n = x.shape[1] // sub
    outs, carry = [None] * n, None
    for c in reversed(range(n)):
        xs = x[:, c * sub:(c + 1) * sub]
        loc = _tri_dot(xs, tri_bf16)
        outs[c] = loc if carry is None else loc + carry
        tot = loc[:, 0:1] + xs[:, 0:1].astype(BF16).astype(F32)
        carry = tot if carry is None else carry + tot
    return (outs[0] if n == 1 else jnp.concatenate(outs, axis=1)), carry


def _softplus2(z):
    z2 = z * LOG2E
    neg_abs = lax.bitcast_convert_type(lax.bitcast_convert_type(z2, jnp.uint32) | jnp.uint32(0x80000000), F32)
    return z2, jnp.maximum(z2, 0.0) + jnp.log2(1.0 + jnp.exp2(neg_abs))


def _block_rows(j, b):
    return pl.ds(pl.multiple_of(j * b, b), b)


def _sb_fwd(proj, *, name):
    s = proj.shape[0]
    b = _pick(s, (SB_B, 128))

    def body(q_ref, k_ref, v_ref, o_ref, tot_ref, first_ref, acc_ref):
        i = pl.program_id(1)
        heads = _head_lanes(b)
        suffix = _tri(min(b, CUM_SUB), lambda r, c: r > c).astype(BF16)
        strict = _twice(_tri(b, lambda r, c: c < r))
        q2 = _stack_heads(q_ref[...] * QK_SCALE, heads)

        def tile(j, a, masked):
            rows = _block_rows(j, b)
            z2, sp = _softplus2(_dot(q2, k_ref[rows, :], NT))
            if masked:
                sp = jnp.where(strict, sp, 0.0)
            rsum, total = _suffix_sums(sp, suffix)
            w = jnp.exp2((z2 - sp) - (a + rsum))
            if masked:
                w = jnp.where(strict, w, 0.0)
            acc_ref[...] += _dot(w.astype(BF16), v_ref[rows, :], NN)
            return a + total

        acc_ref[...] = jnp.zeros_like(acc_ref)
        a = tile(i, jnp.zeros((2 * b, 1), F32), True)

        def more(c):
            return jnp.logical_and(c[0] < i, c[2] < SB_CUT)

        def step(c):
            a = tile(i - 1 - c[0], c[1], False)
            return c[0] + 1, a, jnp.min(a)

        done, a, _ = lax.while_loop(more, step, (jnp.int32(0), a, jnp.min(a)))
        o_ref[...] = _unstack_heads(acc_ref[...], heads).astype(BF16)
        tot_ref[0] = _unstack_stat(a, heads)
        first_ref[pl.program_id(0), i] = i - done

    qblk = pl.BlockSpec((b, LANES), lambda p, i: (i, p))
    return pl.pallas_call(
        body, name=name, grid=(N_PAIRS, s // b),
        in_specs=[qblk, _resident((s, LANES), lambda p, i: (0, N_PAIRS + p)), _resident((s, LANES), lambda p, i: (0, 2 * N_PAIRS + p))],
        out_specs=(qblk, pl.BlockSpec((1, b, LANES), lambda p, i: (p, i, 0)), pl.BlockSpec(memory_space=pltpu.SMEM)),
        out_shape=(jax.ShapeDtypeStruct((s, MIX_W), BF16), jax.ShapeDtypeStruct((N_PAIRS, s, LANES), F32),
                   jax.ShapeDtypeStruct((N_PAIRS, s // b), jnp.int32)),
        scratch_shapes=[pltpu.VMEM((2 * b, LANES), F32)],
        compiler_params=_cparams(("arbitrary", "arbitrary")),
    )(proj, proj, proj)


def _sb_bwd(proj, dmerged, tot, first, *, name):
    s = proj.shape[0]
    b = _pick(s, (SB_B, 128))

    def body(q_ref, k_ref, v_ref, do_ref, tot_ref, first_ref, dq_ref, dk_ref, dv_ref, dq_acc):
        i = pl.program_id(1)

        @pl.when(i == 0)
        def _():
            dk_ref[...] = jnp.zeros_like(dk_ref)
            dv_ref[...] = jnp.zeros_like(dv_ref)

        heads = _head_lanes(b)
        incl = _tri(min(b, CUM_SUB), lambda r, c: r <= c).astype(BF16)
        excl = _tri(min(b, CUM_SUB), lambda r, c: r < c).astype(BF16)
        strict = _twice(_tri(b, lambda r, c: c < r))
        q2 = _stack_heads(q_ref[...] * QK_SCALE, heads)
        do2 = _stack_heads(do_ref[...], heads)
        tot2 = _stack_stat(tot_ref[0])

        def tile(j, pre, gpre, masked):
            rows = _block_rows(j, b)
            kb = k_ref[rows, :]
            z2, sp = _softplus2(_dot(q2, kb, NT))
            oms = jnp.exp2(-sp)
            if masked:
                sp = jnp.where(strict, sp, 0.0)
            pin, ptot = _prefix_sums(sp, incl, True)
            w = jnp.exp2((z2 - sp) + (pin + (pre - tot2)))
            if masked:
                w = jnp.where(strict, w, 0.0)
            gw = _dot(do2, v_ref[rows, :], NT) * w
            gex, gtot = _prefix_sums(gw, excl, False)
            dz = gw * oms - (1.0 - oms) * (gpre + gex)
            if masked:
                dz = jnp.where(strict, dz, 0.0)
            dzb = dz.astype(BF16)
            dq_acc[...] += _dot(dzb, kb, NN)
            dk_ref[rows, :] += _dot(dzb, q2, TN)
            dv_ref[rows, :] += _dot(w.astype(BF16), do2, TN)
            return pre + ptot, gpre + gtot

        dq_acc[...] = jnp.zeros_like(dq_acc)
        zero = jnp.zeros((2 * b, 1), F32)
        pre, gpre = lax.fori_loop(first_ref[pl.program_id(0), i], i, lambda j, c: tile(j, c[0], c[1], False), (zero, zero))
        tile(i, pre, gpre, True)
        dq_ref[...] = (_unstack_heads(dq_acc[...], heads) * QK_SCALE).astype(BF16)

    qblk = pl.BlockSpec((b, LANES), lambda p, i: (i, p))
    full = _resident((s, LANES), lambda p, i: (0, p))
    return pl.pallas_call(
        body, name=name, grid=(N_PAIRS, s // b),
        in_specs=[qblk, _resident((s, LANES), lambda p, i: (0, N_PAIRS + p)), _resident((s, LANES), lambda p, i: (0, 2 * N_PAIRS + p)),
                  qblk, pl.BlockSpec((1, b, LANES), lambda p, i: (p, i, 0)), pl.BlockSpec(memory_space=pltpu.SMEM)],
        out_specs=(qblk, full, full),
        out_shape=(jax.ShapeDtypeStruct((s, MIX_W), BF16), jax.ShapeDtypeStruct((s, MIX_W), F32), jax.ShapeDtypeStruct((s, MIX_W), F32)),
        scratch_shapes=[pltpu.VMEM((2 * b, LANES), F32)],
        compiler_params=_cparams(("parallel", "arbitrary")),
    )(proj, proj, proj, dmerged, tot, first)


def _fox_fwd(proj, kv, c_col, c_row, k_max, *, name):
    s = proj.shape[0]
    b = _pick(s, (FOX_B, 128))

    def body(q_ref, k_ref, v_ref, cc_ref, cr_ref, km_ref, o_ref, o32_ref, lse_ref, first_ref, acc_ref):
        i = pl.program_id(1)
        heads = _head_lanes(b)
        causal = _twice(_tri(b, lambda r, c: c <= r))
        top = lax.broadcasted_iota(jnp.int32, (2 * b, b), 0) < b
        q2 = _stack_heads(q_ref[...] * QK_SCALE, heads)
        c_t = _stack_stat(cc_ref[0])
        qf = q2.astype(F32)
        kmv = km_ref[0]
        z_max = jnp.sqrt(jnp.sum(qf * qf, axis=1, keepdims=True)) * jnp.where(top[:, 0:1], kmv[0:1, 0:1], kmv[1:2, 0:1]) * 1.001

        def tile(j, m, l, masked):
            rows = _block_rows(j, b)
            gate = c_t - jnp.where(top, cr_ref[0, 0:1, rows], cr_ref[0, 1:2, rows])
            sc = _dot(q2, k_ref[rows, :], NT) + gate
            if masked:
                sc = jnp.where(causal, sc, NEG_INF)
            m_new = jnp.maximum(m, jnp.max(sc, axis=1, keepdims=True))
            p = jnp.exp(sc - m_new)
            alpha = jnp.exp(m - m_new)
            acc_ref[...] = alpha * acc_ref[...] + _dot(p.astype(BF16), v_ref[rows, :], NN)
            return m_new, alpha * l + jnp.sum(p, axis=1, keepdims=True), jnp.max(z_max + gate[:, 0:1] - m_new)

        acc_ref[...] = jnp.zeros_like(acc_ref)
        m, l, slack = tile(i, jnp.full((2 * b, 1), NEG_INF, F32), jnp.zeros((2 * b, 1), F32), True)

        def more(c):
            return jnp.logical_and(c[0] < i, c[3] > -FOX_CUT)

        def step(c):
            m, l, slack = tile(i - 1 - c[0], c[1], c[2], False)
            return c[0] + 1, m, l, slack

        done, m, l, _ = lax.while_loop(more, step, (jnp.int32(0), m, l, slack))
        out = _unstack_heads(acc_ref[...] * (1.0 / l), heads)
        o_ref[...] = out.astype(BF16)
        o32_ref[...] = out
        lse_ref[0] = _unstack_stat(m + jnp.log(l), heads)
        first_ref[pl.program_id(0), i] = i - done

    qblk = pl.BlockSpec((b, LANES), lambda p, i: (i, p))
    stat = pl.BlockSpec((1, b, LANES), lambda p, i: (p, i, 0))
    return pl.pallas_call(
        body, name=name, grid=(N_PAIRS, s // b),
        in_specs=[qblk, _resident((s, LANES), lambda p, i: (0, p)), _resident((s, LANES), lambda p, i: (0, N_PAIRS + p)),
                  stat, _resident((1, 8, s), lambda p, i: (p, 0, 0)), pl.BlockSpec((1, 8, LANES), lambda p, i: (p, 0, 0))],
        out_specs=(qblk, qblk, stat, pl.BlockSpec(memory_space=pltpu.SMEM)),
        out_shape=(jax.ShapeDtypeStruct((s, MIX_W), BF16), jax.ShapeDtypeStruct((s, MIX_W), F32), jax.ShapeDtypeStruct((N_PAIRS, s, LANES), F32),
                   jax.ShapeDtypeStruct((N_PAIRS, s // b), jnp.int32)),
        scratch_shapes=[pltpu.VMEM((2 * b, LANES), F32)],
        compiler_params=_cparams(("arbitrary", "arbitrary")),
    )(proj, kv, kv, c_col, c_row, k_max)


def _fox_bwd(proj, kv, c_col, c_row, out32, dmerged, lse, first, *, name):
    s = proj.shape[0]
    b = _pick(s, (FOX_B, 128))

    def body(q_ref, k_ref, v_ref, cc_ref, cr_ref, o_ref, do_ref, lse_ref, first_ref, dq_ref, dk_ref, dv_ref, dc_ref, dcq_ref, dq_acc):
        i = pl.program_id(1)

        @pl.when(i == 0)
        def _():
            dk_ref[...] = jnp.zeros_like(dk_ref)
            dv_ref[...] = jnp.zeros_like(dv_ref)
            dc_ref[...] = jnp.zeros_like(dc_ref)

        heads = _head_lanes(b)
        causal = _twice(_tri(b, lambda r, c: c <= r))
        top = lax.broadcasted_iota(jnp.int32, (2 * b, b), 0) < b
        q2 = _stack_heads(q_ref[...] * QK_SCALE, heads)
        dov = do_ref[...]
        do2 = _stack_heads(dov, heads)
        prod = dov.astype(F32) * o_ref[...]
        delta = jnp.concatenate([jnp.sum(jnp.where(heads[hh], prod, 0.0), axis=1, keepdims=True) for hh in range(2)], axis=0)
        c_t = _stack_stat(cc_ref[0])
        lse_t = _stack_stat(lse_ref[0])

        def tile(j, rsum, masked):
            rows = _block_rows(j, b)
            kb = k_ref[rows, :]
            c_s = jnp.where(top, cr_ref[0, 0:1, rows], cr_ref[0, 1:2, rows])
            sc = _dot(q2, kb, NT) + (c_t - c_s)
            p = jnp.exp(sc - lse_t)
            if masked:
                p = jnp.where(causal, p, 0.0)
            ds = p * (_dot(do2, v_ref[rows, :], NT) - delta)
            dsb = ds.astype(BF16)
            dq_acc[...] += _dot(dsb, kb, NN)
            dk_ref[rows, :] += _dot(dsb, q2, TN)
            dv_ref[rows, :] += _dot(p.astype(BF16), do2, TN)
            dc_ref[0, 0:1, rows] -= jnp.sum(ds[:b], axis=0, keepdims=True)
            dc_ref[0, 1:2, rows] -= jnp.sum(ds[b:], axis=0, keepdims=True)
            return rsum + jnp.sum(ds, axis=1, keepdims=True)

        dq_acc[...] = jnp.zeros_like(dq_acc)
        rsum = lax.fori_loop(first_ref[pl.program_id(0), i], i, lambda j, r: tile(j, r, False), jnp.zeros((2 * b, 1), F32))
        rsum = tile(i, rsum, True)
        dq_ref[...] = (_unstack_heads(dq_acc[...], heads) * QK_SCALE).astype(BF16)
        dcq_ref[0] = _unstack_stat(rsum, heads)

    qblk = pl.BlockSpec((b, LANES), lambda p, i: (i, p))
    stat = pl.BlockSpec((1, b, LANES), lambda p, i: (p, i, 0))
    crow = _resident((1, 8, s), lambda p, i: (p, 0, 0))
    full = _resident((s, LANES), lambda p, i: (0, p))
    return pl.pallas_call(
        body, name=name, grid=(N_PAIRS, s // b),
        in_specs=[qblk, full, _resident((s, LANES), lambda p, i: (0, N_PAIRS + p)), stat, crow, qblk, qblk, stat,
                  pl.BlockSpec(memory_space=pltpu.SMEM)],
        out_specs=(qblk, full, full, crow, stat),
        out_shape=(jax.ShapeDtypeStruct((s, MIX_W), BF16), jax.ShapeDtypeStruct((s, MIX_W), F32), jax.ShapeDtypeStruct((s, MIX_W), F32),
                   jax.ShapeDtypeStruct((N_PAIRS, 8, s), F32), jax.ShapeDtypeStruct((N_PAIRS, s, LANES), F32)),
        scratch_shapes=[pltpu.VMEM((2 * b, LANES), F32)],
        compiler_params=_cparams(("parallel", "arbitrary")),
    )(proj, kv, kv, c_col, c_row, out32, dmerged, lse, first)


MEM_TQ = 1024


def _mem_fwd(proj, q_col_block, mkv, *, name):
    s = proj.shape[0]
    tq = _pick(s, (MEM_TQ, 128))
    n_mem = mkv.shape[0]

    def body(q_ref, mkv_ref, o_ref):
        heads = _head_lanes(tq)
        for pp in range(MEM_W // LANES):
            cols = slice(pp * LANES, (pp + 1) * LANES)
            qv = q_ref[:, cols] * QK_SCALE
            mk = mkv_ref[:, pp * LANES:(pp + 1) * LANES]
            mv = mkv_ref[:, MEM_W + pp * LANES:MEM_W + (pp + 1) * LANES]
            o_sel = None
            for hh in range(2):
                qm = jnp.where(heads[hh], qv, jnp.zeros_like(qv))
                sc = _dot(qm, mk, NT)
                p = jnp.exp(sc - jnp.max(sc, axis=1, keepdims=True))
                p = p / jnp.sum(p, axis=1, keepdims=True)
                out = _dot(p.astype(BF16), mv, NN)
                o_sel = out if hh == 0 else jnp.where(heads[0], o_sel, out)
            o_ref[:, cols] = o_sel.astype(BF16)

    return pl.pallas_call(
        body, name=name, grid=(s // tq,),
        in_specs=[pl.BlockSpec((tq, MEM_W), lambda i: (i, q_col_block)), pl.BlockSpec((n_mem, 2 * MEM_W), lambda i: (0, 0))],
        out_specs=pl.BlockSpec((tq, MEM_W), lambda i: (i, 0)),
        out_shape=jax.ShapeDtypeStruct((s, MEM_W), BF16),
        compiler_params=_cparams(("parallel",)),
    )(proj, mkv)


def _mem_bwd(proj, q_col_block, mkv, dmerged, *, name):
    s = proj.shape[0]
    tq = _pick(s, (MEM_TQ, 128))
    n_mem = mkv.shape[0]

    def body(q_ref, mkv_ref, do_ref, dq_ref, dmkv_ref):
        i = pl.program_id(0)

        @pl.when(i == 0)
        def _():
            dmkv_ref[...] = jnp.zeros_like(dmkv_ref)

        heads = _head_lanes(tq)
        for pp in range(MEM_W // LANES):
            cols = slice(pp * LANES, (pp + 1) * LANES)
            vcols = slice(MEM_W + pp * LANES, MEM_W + (pp + 1) * LANES)
            qv = q_ref[:, cols] * QK_SCALE
            dov = do_ref[:, cols]
            mk = mkv_ref[:, cols]
            mv = mkv_ref[:, vcols]
            dq_sel = None
            for hh in range(2):
                qm = jnp.where(heads[hh], qv, jnp.zeros_like(qv))
                dom = jnp.where(heads[hh], dov, jnp.zeros_like(dov))
                sc = _dot(qm, mk, NT)
                p = jnp.exp(sc - jnp.max(sc, axis=1, keepdims=True))
                p = p / jnp.sum(p, axis=1, keepdims=True)
                dp = _dot(dom, mv, NT)
                ds = p * (dp - jnp.sum(p * dp, axis=1, keepdims=True))
                dsb = ds.astype(BF16)
                dq = _dot(dsb, mk, NN)
                dmkv_ref[:, cols] += _dot(dsb, qm, TN)
                dmkv_ref[:, vcols] += _dot(p.astype(BF16), dom, TN)
                dq_sel = dq if hh == 0 else jnp.where(heads[0], dq_sel, dq)
            dq_ref[:, cols] = (dq_sel * QK_SCALE).astype(BF16)

    return pl.pallas_call(
        body, name=name, grid=(s // tq,),
        in_specs=[pl.BlockSpec((tq, MEM_W), lambda i: (i, q_col_block)), pl.BlockSpec((n_mem, 2 * MEM_W), lambda i: (0, 0)),
                  pl.BlockSpec((tq, MEM_W), lambda i: (i, MIX_W // MEM_W))],
        out_specs=(pl.BlockSpec((tq, MEM_W), lambda i: (i, 0)), pl.BlockSpec((n_mem, 2 * MEM_W), lambda i: (0, 0))),
        out_shape=(jax.ShapeDtypeStruct((s, MEM_W), BF16), jax.ShapeDtypeStruct((n_mem, 2 * MEM_W), F32)),
        compiler_params=_cparams(("arbitrary",)),
    )(proj, mkv, dmerged)


GATE_TB = 512


def _split3_dot(tri_bf16, x):
    x1 = x.astype(BF16)
    r1 = x - x1.astype(F32)
    x2 = r1.astype(BF16)
    x3 = (r1 - x2.astype(F32)).astype(BF16)
    return _dot(tri_bf16, x1, NN) + _dot(tri_bf16, x2, NN) + _dot(tri_bf16, x3, NN)


def _gate_fwd(f, b, *, name):
    s = f.shape[0]
    tb = _pick(s, (GATE_TB, 128))

    def body(f_ref, b_ref, c_ref, carry_ref):
        i = pl.program_id(0)

        @pl.when(i == 0)
        def _():
            carry_ref[...] = jnp.zeros_like(carry_ref)

        x = f_ref[...] + b_ref[...]
        lf = jnp.minimum(x, 0.0) - jnp.log1p(jnp.exp(-jnp.abs(x)))
        row = lax.broadcasted_iota(jnp.int32, (tb, tb), 0)
        col = lax.broadcasted_iota(jnp.int32, (tb, tb), 1)
        lower = (col <= row).astype(BF16)
        c = carry_ref[...] + _split3_dot(lower, lf)
        c_ref[...] = c
        carry_ref[...] = c[tb - 1:tb, :]

    return pl.pallas_call(
        body, name=name, grid=(s // tb,),
        in_specs=[pl.BlockSpec((tb, LANES), lambda i: (i, 0)), pl.BlockSpec((1, LANES), lambda i: (0, 0))],
        out_specs=pl.BlockSpec((tb, LANES), lambda i: (i, 0)),
        out_shape=jax.ShapeDtypeStruct((s, LANES), F32),
        scratch_shapes=[pltpu.VMEM((1, LANES), F32)],
        compiler_params=_cparams(("arbitrary",)),
    )(f, b)


def _gate_bwd(f, b, dc, *, name):
    s = f.shape[0]
    tb = _pick(s, (GATE_TB, 128))
    nb = s // tb

    def body(f_ref, b_ref, dc_ref, df_ref, db_ref, carry_ref):
        i = pl.program_id(0)

        @pl.when(i == 0)
        def _():
            carry_ref[...] = jnp.zeros_like(carry_ref)
            db_ref[...] = jnp.zeros_like(db_ref)

        row = lax.broadcasted_iota(jnp.int32, (tb, tb), 0)
        col = lax.broadcasted_iota(jnp.int32, (tb, tb), 1)
        upper = (col >= row).astype(BF16)
        dlf = carry_ref[...] + _split3_dot(upper, dc_ref[...])
        carry_ref[...] = dlf[0:1, :]
        x = f_ref[...] + b_ref[...]
        e = jnp.exp(-jnp.abs(x))
        one_minus_sig = jnp.where(x >= 0.0, e, 1.0) / (1.0 + e)
        df = dlf * one_minus_sig
        df_ref[...] = df
        db_ref[...] += jnp.sum(df, axis=0, keepdims=True)

    rev = pl.BlockSpec((tb, LANES), lambda i: (nb - 1 - i, 0))
    vec = pl.BlockSpec((1, LANES), lambda i: (0, 0))
    return pl.pallas_call(
        body, name=name, grid=(nb,), in_specs=[rev, vec, rev], out_specs=(rev, vec),
        out_shape=(jax.ShapeDtypeStruct((s, LANES), F32), jax.ShapeDtypeStruct((1, LANES), F32)),
        scratch_shapes=[pltpu.VMEM((1, LANES), F32)],
        compiler_params=_cparams(("arbitrary",)),
    )(f, b, dc)


def _all_gather(x, *, name):
    r, cdim = x.shape

    def body(x_ref, out_ref, send_sems, recv_sems, local_sem):
        mx, my, mc = lax.axis_index("x"), lax.axis_index("y"), lax.axis_index("c")
        me, sibling = (mx, my, mc), (mx, my, 1 - mc)
        chips = [(1 - mx, my), (mx, 1 - my), (1 - mx, 1 - my)]

        def rows(px, py, pc):
            return out_ref.at[pl.ds((4 * px + 2 * py + pc) * r, r), :]

        def copy(k, block, to, src=None):
            return pltpu.make_async_remote_copy(
                src_ref=rows(*block) if src is None else src, dst_ref=rows(*block),
                send_sem=send_sems.at[k], recv_sem=recv_sems.at[k], device_id=to, device_id_type=MESH)

        mine = pltpu.make_async_copy(x_ref, rows(*me), local_sem)
        mine.start()
        first = [copy(0, me, sibling, src=x_ref)]
        first += [copy(1 + j, me, (*chip, mc), src=x_ref) for j, chip in enumerate(chips)]
        for cp in first:
            cp.start()
        passed = [copy(4 + j, (*chip, mc), sibling) for j, chip in enumerate(chips)]
        for j, chip in enumerate(chips):
            copy(1 + j, (*chip, mc), me).wait_recv()
            passed[j].start()
        copy(0, sibling, me).wait_recv()
        for j, chip in enumerate(chips):
            copy(4 + j, (*chip, 1 - mc), me).wait_recv()
        for cp in first + passed:
            cp.wait_send()
        mine.wait()

    return pl.pallas_call(
        body, name=name,
        in_specs=[pl.BlockSpec(memory_space=pl.ANY)], out_specs=pl.BlockSpec(memory_space=pl.ANY),
        out_shape=jax.ShapeDtypeStruct((N_DEV * r, cdim), x.dtype),
        scratch_shapes=[pltpu.SemaphoreType.DMA((7,)), pltpu.SemaphoreType.DMA((7,)), pltpu.SemaphoreType.DMA],
    )(x)


N_CHIPS = N_DEV // 2


def _sibling_exchange(g, small, *, name):
    def body(g_ref, s_ref, out_ref, sout_ref, send_sems, recv_sems, local_sem):
        mx, my, mc = lax.axis_index("x"), lax.axis_index("y"), lax.axis_index("c")
        me = 4 * mx + 2 * my + mc
        sibling = (mx, my, 1 - mc)
        mine = pltpu.make_async_copy(s_ref, sout_ref.at[me], local_sem)
        mine.start()
        copies = []
        for k in range(N_CHIPS):
            copies.append(pltpu.make_async_remote_copy(
                src_ref=g_ref.at[2 * k + (1 - mc)], dst_ref=out_ref.at[k],
                send_sem=send_sems.at[k], recv_sem=recv_sems.at[k], device_id=sibling, device_id_type=MESH))
        for mask in range(1, N_DEV):
            px = 1 - mx if mask & 4 else mx
            py = 1 - my if mask & 2 else my
            pc = 1 - mc if mask & 1 else mc
            copies.append(pltpu.make_async_remote_copy(
                src_ref=s_ref, dst_ref=sout_ref.at[me],
                send_sem=send_sems.at[N_CHIPS - 1 + mask], recv_sem=recv_sems.at[N_CHIPS - 1 + mask], device_id=(px, py, pc), device_id_type=MESH))
        for cp in copies:
            cp.start()
        for cp in copies:
            cp.wait()
        mine.wait()

    n_sem = N_CHIPS + N_DEV - 1
    return pl.pallas_call(
        body, name=name,
        in_specs=[pl.BlockSpec(memory_space=pl.ANY), pl.BlockSpec(memory_space=pl.ANY)],
        out_specs=(pl.BlockSpec(memory_space=pl.ANY), pl.BlockSpec(memory_space=pl.ANY)),
        out_shape=(jax.ShapeDtypeStruct((N_CHIPS,) + g.shape[1:], g.dtype), jax.ShapeDtypeStruct((N_DEV,) + small.shape, small.dtype)),
        scratch_shapes=[pltpu.SemaphoreType.DMA((n_sem,)), pltpu.SemaphoreType.DMA((n_sem,)), pltpu.SemaphoreType.DMA],
    )(g, small)


def _pair_sum(g, theirs, *, name):
    _, r, cdim = g.shape
    tr = _pick(r, (4 * PACK_BLOCK_ROWS, 2 * PACK_BLOCK_ROWS, PACK_BLOCK_ROWS, 512, 256, 128, 64, 32, 16))

    def body(g0_ref, g1_ref, t_ref, o_ref):
        south = lax.axis_index("c") == 0
        mine = jnp.where(south, g0_ref[...].astype(F32), g1_ref[...].astype(F32))
        o_ref[...] = (mine + t_ref[...].astype(F32)).astype(o_ref.dtype)

    return pl.pallas_call(
        body, name=name, grid=(N_CHIPS, r // tr),
        in_specs=[pl.BlockSpec((1, tr, cdim), lambda k, i: (2 * k, i, 0)), pl.BlockSpec((1, tr, cdim), lambda k, i: (2 * k + 1, i, 0)),
                  pl.BlockSpec((1, tr, cdim), lambda k, i: (k, i, 0))],
        out_specs=pl.BlockSpec((1, tr, cdim), lambda k, i: (k, i, 0)),
        out_shape=jax.ShapeDtypeStruct((N_CHIPS, r, cdim), g.dtype),
        compiler_params=_cparams(("parallel", "parallel")),
    )(g, g, theirs)


def _chip_exchange(sums, *, name):
    def body(s_ref, out_ref, send_sems, recv_sems, local_sem):
        mx, my, mc = lax.axis_index("x"), lax.axis_index("y"), lax.axis_index("c")
        chip = 2 * mx + my
        mine = pltpu.make_async_copy(s_ref.at[chip], out_ref.at[chip], local_sem)
        mine.start()
        copies = []
        for mask in range(1, N_CHIPS):
            px = 1 - mx if mask & 2 else mx
            py = 1 - my if mask & 1 else my
            copies.append(pltpu.make_async_remote_copy(
                src_ref=s_ref.at[2 * px + py], dst_ref=out_ref.at[chip],
                send_sem=send_sems.at[mask - 1], recv_sem=recv_sems.at[mask - 1], device_id=(px, py, mc), device_id_type=MESH))
        for cp in copies:
            cp.start()
        for cp in copies:
            cp.wait()
        mine.wait()

    return pl.pallas_call(
        body, name=name,
        in_specs=[pl.BlockSpec(memory_space=pl.ANY)], out_specs=pl.BlockSpec(memory_space=pl.ANY),
        out_shape=jax.ShapeDtypeStruct(sums.shape, sums.dtype),
        scratch_shapes=[pltpu.SemaphoreType.DMA((N_CHIPS - 1,)), pltpu.SemaphoreType.DMA((N_CHIPS - 1,)), pltpu.SemaphoreType.DMA],
    )(sums)


def _adamw(parts, w, m, v, *, name):
    r, cdim = w.shape
    n_parts = parts.shape[0]
    tr = _pick(r, (4 * PACK_BLOCK_ROWS, 2 * PACK_BLOCK_ROWS, PACK_BLOCK_ROWS, 512, 256, 128, 64, 32, 16, 8))
    c1 = 1.0 / (1.0 - ADAM_B1 ** ADAM_STEP)
    c2 = 1.0 / (1.0 - ADAM_B2 ** ADAM_STEP)

    def body(p_ref, w_ref, m_ref, v_ref, g_ref, d_ref, nm_ref, nv_ref):
        g = p_ref[0].astype(F32)
        for part in range(1, n_parts):
            g = g + p_ref[part].astype(F32)
        mn = ADAM_B1 * m_ref[...] + (1.0 - ADAM_B1) * g
        vn = ADAM_B2 * v_ref[...] + (1.0 - ADAM_B2) * (g * g)
        g_ref[...] = g
        nm_ref[...] = mn
        nv_ref[...] = vn
        d_ref[...] = -ADAM_LR * ((mn * c1) / (jnp.sqrt(vn * c2) + ADAM_EPS) + ADAM_WD * w_ref[...])

    blk = pl.BlockSpec((tr, cdim), lambda i: (i, 0))
    shape = jax.ShapeDtypeStruct((r, cdim), F32)
    return pl.pallas_call(
        body, name=name, grid=(r // tr,),
        in_specs=[pl.BlockSpec((n_parts, tr, cdim), lambda i: (0, i, 0)), blk, blk, blk],
        out_specs=(blk, blk, blk, blk), out_shape=(shape, shape, shape, shape),
        compiler_params=_cparams(("parallel",)),
    )(parts, w, m, v)


def _rows_of(shape):
    n = math.prod(shape)
    assert n % LANES == 0, shape
    rows = n // LANES
    return -(-rows // PACK_ROW_ALIGN) * PACK_ROW_ALIGN


def _layout(shard_shapes):
    out, off = {}, 0
    for name, _ in SHARDED:
        rows = _rows_of(shard_shapes[name])
        out[name] = (off, rows, tuple(shard_shapes[name]))
        off += rows
    return out, -(-off // PACK_BLOCK_ROWS) * PACK_BLOCK_ROWS


def _pack_shards(layout, total, arrays, dtype):
    parts = []
    for name, _ in SHARDED:
        _, rows, _ = layout[name]
        flat = arrays[name].astype(dtype).reshape(-1, LANES)
        parts.append(jnp.pad(flat, ((0, rows - flat.shape[0]), (0, 0))))
    used = sum(p.shape[0] for p in parts)
    if total > used:
        parts.append(jnp.zeros((total - used, LANES), dtype))
    return jnp.concatenate(parts, axis=0)


def _unpack_shard(layout, flat, name):
    off, _, shape = layout[name]
    n = math.prod(shape) // LANES
    return flat[off:off + n].reshape(shape)


def _unpack_full(layout, gathered, name, axis):
    off, _, shape = layout[name]
    n = math.prod(shape) // LANES
    blocks = gathered[:, off:off + n].reshape((N_DEV,) + shape)
    blocks = jnp.moveaxis(blocks, 0, axis)
    return blocks.reshape(shape[:axis] + (N_DEV * shape[axis],) + shape[axis + 1:])


def _pack_full(layout, total, grads):
    parts = []
    for name, axis in SHARDED:
        _, rows, shape = layout[name]
        g = grads[name]
        blocks = g.reshape(shape[:axis] + (N_DEV, shape[axis]) + shape[axis + 1:])
        blocks = jnp.moveaxis(blocks, axis, 0).reshape(N_DEV, -1, LANES)
        parts.append(jnp.pad(blocks, ((0, 0), (0, rows - blocks.shape[1]), (0, 0))))
    used = sum(p.shape[1] for p in parts)
    if total > used:
        parts.append(jnp.zeros((N_DEV, total - used, LANES), F32))
    return jnp.concatenate(parts, axis=1)


def _pad_lanes(a):
    return jnp.pad(a, ((0, 0), (0, LANES - a.shape[1])))


def _pair_layouts(c):
    s = c.shape[0]
    by_pair = c.T.reshape(N_PAIRS, 2, s)
    c_col = jnp.repeat(by_pair.transpose(0, 2, 1), HEAD_DIM, axis=2)
    c_row = jnp.pad(by_pair, ((0, 0), (0, 6), (0, 0)))
    return c_col, c_row


def _key_norm_bound(k):
    norms = jnp.sqrt(jnp.max(jnp.sum(jnp.square(k.astype(F32)).reshape(k.shape[0], N_MIX_HEADS, HEAD_DIM), axis=2), axis=0))
    rows = jnp.pad(norms.reshape(N_PAIRS, 2), ((0, 0), (0, 6)))
    return jnp.broadcast_to(rows[:, :, None], (N_PAIRS, 8, LANES))


def _forward_backward(x, mem, target, wts, small):
    n_a = wts["w_in_a"].shape[0]
    n_b = wts["w_in_b"].shape[0]
    depth = n_a + n_b
    w_kv = wts["w_kv_shared"]
    w_kv_kv = w_kv[:, :2 * MIX_W]
    w_kv_f = _pad_lanes(w_kv[:, 2 * MIX_W:])
    b_f = _pad_lanes(small["b_f"].reshape(1, -1))

    saved = []
    shared = None
    h = x
    for l in range(depth):
        is_a = l < n_a
        if l == n_a:
            hs = _rmsnorm_fwd(h, small["kv_norm_g"], name="kv_norm")
            kv = _mm(hs, w_kv_kv, name="kv_proj", out_dtype=BF16)
            f = _mm(hs, w_kv_f, name="gate_proj")
            c = _gate_fwd(f, b_f, name="gate_cumsum")
            c_col, c_row = _pair_layouts(c[:, :N_MIX_HEADS])
            shared = dict(h=h, hs=hs, kv=kv, f=f, c_col=c_col, c_row=c_row, k_max=_key_norm_bound(kv[:, :MIX_W]))
        hn = _rmsnorm_fwd(h, small["norm1_g"][l], name=f"norm1_{l}")
        memn = _rmsnorm_fwd(mem, small["mem_norm_g"][l], name=f"mem_norm_{l}")
        mkv = _mm(memn, wts["w_mem_kv"][l], name=f"mem_kv_{l}", out_dtype=BF16)
        if is_a:
            proj = _mm(hn, wts["w_in_a"][l], name=f"in_proj_{l}", out_dtype=BF16)
            mix, stat, first = _sb_fwd(proj, name=f"sb_fwd_{l}")
            mix32 = None
            q_block = 3 * MIX_W // MEM_W
        else:
            proj = _mm(hn, wts["w_in_b"][l - n_a], name=f"in_proj_{l}", out_dtype=BF16)
            mix, mix32, stat, first = _fox_fwd(proj, shared["kv"], shared["c_col"], shared["c_row"], shared["k_max"], name=f"fox_fwd_{l}")
            q_block = MIX_W // MEM_W
        mem_out = _mem_fwd(proj, q_block, mkv, name=f"mem_fwd_{l}")
        merged = jnp.concatenate([mix, mem_out], axis=1)
        h_mid = _mm(merged, wts["w_o"][l], name=f"o_proj_{l}", res=h)
        h2n = _rmsnorm_fwd(h_mid, small["norm2_g"][l], name=f"norm2_{l}")
        u, act = _mm(h2n, wts["w_mlp1"][l], name=f"mlp1_{l}", epilogue="relu2")
        h_out = _mm(act, wts["w_mlp2"][l], name=f"mlp2_{l}", res=h_mid)
        saved.append(dict(h=h, hn=hn, memn=memn, mkv=mkv, proj=proj, stat=stat, first=first, mix32=mix32, merged=merged, h_mid=h_mid, h2n=h2n, u=u, act=act,
                          q_block=q_block))
        h = h_out

    loss, dh, dg_final = _final_loss(h, target, small["final_norm_g"], name="final_loss")

    g_w = {k: [None] * wts[k].shape[0] for k in ("w_in_a", "w_in_b", "w_mem_kv", "w_o", "w_mlp1", "w_mlp2")}
    g_n = {k: [None] * depth for k in ("norm1_g", "mem_norm_g", "norm2_g")}
    dk_sh = dv_sh = dc_sh = dcq_sh = None
    for l in reversed(range(depth)):
        sv = saved[l]
        is_a = l < n_a
        du = _mm(dh, wts["w_mlp2"][l], name=f"d_act_{l}", trans_b=True, epilogue="drelu2", u=sv["u"], out_dtype=BF16)
        g_w["w_mlp2"][l] = _mm_tn(sv["act"], dh, name=f"dw_mlp2_{l}")
        g_w["w_mlp1"][l] = _mm_tn(sv["h2n"], du, name=f"dw_mlp1_{l}")
        dh_mid, g_n["norm2_g"][l] = _mm(du, wts["w_mlp1"][l], name=f"d_h2n_{l}", trans_b=True, norm=(sv["h_mid"], small["norm2_g"][l], dh))
        dmerged = _mm(dh_mid, wts["w_o"][l], name=f"d_merged_{l}", trans_b=True, out_dtype=BF16)
        g_w["w_o"][l] = _mm_tn(sv["merged"], dh_mid, name=f"dw_o_{l}")
        dqm, dmkv = _mem_bwd(sv["proj"], sv["q_block"], sv["mkv"], dmerged, name=f"mem_bwd_{l}")
        if is_a:
            dq, dk, dv = _sb_bwd(sv["proj"], dmerged, sv["stat"], sv["first"], name=f"sb_bwd_{l}")
            dproj = jnp.concatenate([dq, dk.astype(BF16), dv.astype(BF16), dqm], axis=1)
            w_in, key, idx = wts["w_in_a"][l], "w_in_a", l
        else:
            dq, dk, dv, dc, dcq = _fox_bwd(sv["proj"], shared["kv"], shared["c_col"], shared["c_row"], sv["mix32"], dmerged, sv["stat"],
                                           sv["first"], name=f"fox_bwd_{l}")
            dk_sh = dk if dk_sh is None else dk_sh + dk
            dv_sh = dv if dv_sh is None else dv_sh + dv
            dc_sh = dc if dc_sh is None else dc_sh + dc
            dcq_sh = dcq if dcq_sh is None else dcq_sh + dcq
            dproj = jnp.concatenate([dq, dqm], axis=1)
            w_in, key, idx = wts["w_in_b"][l - n_a], "w_in_b", l - n_a
        g_w[key][idx] = _mm_tn(sv["hn"], dproj, name=f"dw_in_{l}")
        dh, g_n["norm1_g"][l] = _mm(dproj, w_in, name=f"d_hn_{l}", trans_b=True, norm=(sv["h"], small["norm1_g"][l], dh_mid))
        g_w["w_mem_kv"][l] = _mm_tn(sv["memn"], dmkv, name=f"dw_mem_kv_{l}")
        _, g_n["mem_norm_g"][l] = _mm(dmkv, wts["w_mem_kv"][l], name=f"d_memn_{l}", trans_b=True, norm=(mem, small["mem_norm_g"][l], None))
        if l == n_a:
            s_len = x.shape[0]
            dc_query = jnp.stack([dcq_sh[:, :, 0], dcq_sh[:, :, HEAD_DIM]], axis=-1).transpose(1, 0, 2).reshape(s_len, N_MIX_HEADS)
            dc_tok = _pad_lanes(dc_sh[:, :2, :].reshape(N_MIX_HEADS, s_len).T + dc_query)
            df, db = _gate_bwd(shared["f"], b_f, dc_tok, name="gate_bwd")
            dkv = jnp.concatenate([dk_sh, dv_sh], axis=1)
            dw_kv_kv = _mm_tn(shared["hs"], dkv, name="dw_kv")
            dw_kv_f = _mm_tn(shared["hs"], df, name="dw_gate")
            dhs = _mm(dkv, w_kv_kv, name="d_hs_kv", trans_b=True)
            dh, dg_kv = _mm(df, w_kv_f, name="d_hs_gate", trans_b=True, res=dhs, norm=(shared["h"], small["kv_norm_g"], dh))
            g_kv = jnp.concatenate([dw_kv_kv, dw_kv_f[:, :N_KV_F]], axis=1)

    grads = {k: jnp.stack(v) for k, v in g_w.items()}
    grads["w_kv_shared"] = g_kv
    d = x.shape[1]
    small_g = dict(
        norm1_g=jnp.concatenate(g_n["norm1_g"], axis=0), mem_norm_g=jnp.concatenate(g_n["mem_norm_g"], axis=0),
        norm2_g=jnp.concatenate(g_n["norm2_g"], axis=0), kv_norm_g=dg_kv.reshape(d), b_f=db[0, :N_KV_F], final_norm_g=dg_final.reshape(d))
    return loss, dh, grads, small_g


def _pack_small(vals, d):
    rows = []
    for name in REPLICATED:
        a = vals[name].astype(F32)
        if name == "b_f":
            a = jnp.pad(a, (0, d - a.shape[0]))
        rows.append(a.reshape(-1, d))
    packed = jnp.concatenate(rows, axis=0)
    pad = -packed.shape[0] % 8
    return jnp.pad(packed, ((0, pad), (0, 0)))


def _unpack_small(packed, shapes):
    out, off = {}, 0
    for name in REPLICATED:
        shape = shapes[name]
        if name == "b_f":
            out[name] = packed[off, :shape[0]]
            off += 1
        else:
            n = math.prod(shape) // packed.shape[1]
            out[name] = packed[off:off + n].reshape(shape)
            off += n
    return out


def kernel(x, mem, norm1_g, w_in_a, w_in_b, w_mem_kv, mem_norm_g, w_o, norm2_g, w_mlp1, w_mlp2, kv_norm_g, w_kv_shared, b_f, final_norm_g, loss_target, m_norm1_g, m_w_in_a, m_w_in_b, m_w_mem_kv, m_mem_norm_g, m_w_o, m_norm2_g, m_w_mlp1, m_w_mlp2, m_kv_norm_g, m_w_kv_shared, m_b_f, m_final_norm_g, v_norm1_g, v_w_in_a, v_w_in_b, v_w_mem_kv, v_mem_norm_g, v_w_o, v_norm2_g, v_w_mlp1, v_w_mlp2, v_kv_norm_g, v_w_kv_shared, v_b_f, v_final_norm_g):
    w = dict(norm1_g=norm1_g, w_in_a=w_in_a, w_in_b=w_in_b, w_mem_kv=w_mem_kv, mem_norm_g=mem_norm_g, w_o=w_o, norm2_g=norm2_g,
             w_mlp1=w_mlp1, w_mlp2=w_mlp2, kv_norm_g=kv_norm_g, w_kv_shared=w_kv_shared, b_f=b_f, final_norm_g=final_norm_g)
    m = dict(norm1_g=m_norm1_g, w_in_a=m_w_in_a, w_in_b=m_w_in_b, w_mem_kv=m_w_mem_kv, mem_norm_g=m_mem_norm_g, w_o=m_w_o,
             norm2_g=m_norm2_g, w_mlp1=m_w_mlp1, w_mlp2=m_w_mlp2, kv_norm_g=m_kv_norm_g, w_kv_shared=m_w_kv_shared, b_f=m_b_f,
             final_norm_g=m_final_norm_g)
    v = dict(norm1_g=v_norm1_g, w_in_a=v_w_in_a, w_in_b=v_w_in_b, w_mem_kv=v_w_mem_kv, mem_norm_g=v_mem_norm_g, w_o=v_w_o,
             norm2_g=v_norm2_g, w_mlp1=v_w_mlp1, w_mlp2=v_w_mlp2, kv_norm_g=v_kv_norm_g, w_kv_shared=v_w_kv_shared, b_f=v_b_f,
             final_norm_g=v_final_norm_g)
    d = x.shape[-1]
    layout, total = _layout({name: w[name].shape for name, _ in SHARDED})

    gathered = _all_gather(_pack_shards(layout, total, w, BF16), name="gather_weights").reshape(N_DEV, total, LANES)
    wts = {name: _unpack_full(layout, gathered, name, axis) for name, axis in SHARDED}
    small = {name: w[name] for name in REPLICATED}

    loss, grad_x, grads, small_g = _forward_backward(x[0], mem[0], loss_target[0], wts, small)

    g_packed = _pack_full(layout, total, grads).astype(BF16)
    theirs, small_parts = _sibling_exchange(g_packed, _pack_small(small_g, d), name="exchange_siblings")
    parts = _chip_exchange(_pair_sum(g_packed, theirs, name="pair_sum"), name="exchange_chips")
    g_flat, d_flat, m_flat, v_flat = _adamw(parts, _pack_shards(layout, total, w, F32), _pack_shards(layout, total, m, F32),
                                            _pack_shards(layout, total, v, F32), name="adamw_sharded")
    gs, ds_, ms, vs = _adamw(small_parts, _pack_small(w, d), _pack_small(m, d), _pack_small(v, d), name="adamw_replicated")

    shapes = {name: w[name].shape for name in REPLICATED}
    out_g, out_d, out_m, out_v = {}, {}, {}, {}
    for flat, small_flat, out in ((g_flat, gs, out_g), (d_flat, ds_, out_d), (m_flat, ms, out_m), (v_flat, vs, out_v)):
        for name, _ in SHARDED:
            out[name] = _unpack_shard(layout, flat, name)
        out.update(_unpack_small(small_flat, shapes))

    loss_total = lax.psum(loss[0, 0], ("x", "y", "c"))
    return (loss_total, grad_x[None], *[out_g[n] for n in WEIGHT_ORDER], *[out_d[n] for n in WEIGHT_ORDER],
            *[out_m[n] for n in WEIGHT_ORDER], *[out_v[n] for n in WEIGHT_ORDER])
```

```python
import functools
import math

import jax
import jax.numpy as jnp
from jax import lax
from jax.experimental import pallas as pl
from jax.experimental.pallas import tpu as pltpu

F32 = jnp.float32
BF16 = jnp.bfloat16

N_DEV = 8
HEAD_DIM = 64
N_MIX_HEADS = 8
N_MEM_HEADS = 4
MIX_W = N_MIX_HEADS * HEAD_DIM
MEM_W = N_MEM_HEADS * HEAD_DIM
N_PAIRS = N_MIX_HEADS // 2
LANES = 128
SB_B = 256
FOX_B = 512
CUM_SUB = 256
SB_CUT = 64.0
FOX_CUT = 45.0
EPS = 1e-6
NEG_INF = -1e30
QK_SCALE = 1.0 / math.sqrt(HEAD_DIM)
LOG2E = 1.4426950408889634
N_KV_F = 8

ADAM_LR = 0.001
ADAM_B1 = 0.9
ADAM_B2 = 0.999
ADAM_EPS = 1e-08
ADAM_WD = 0.01
ADAM_STEP = 10

VMEM_LIMIT = 56 * 1024 * 1024
PACK_ROW_ALIGN = 16
PACK_BLOCK_ROWS = 1024

MESH = pl.DeviceIdType.MESH

SHARDED = (("w_in_a", 2), ("w_in_b", 1), ("w_mem_kv", 1), ("w_o", 2), ("w_mlp1", 2), ("w_mlp2", 1), ("w_kv_shared", 1))
REPLICATED = ("norm1_g", "mem_norm_g", "norm2_g", "kv_norm_g", "b_f", "final_norm_g")
WEIGHT_ORDER = ("norm1_g", "w_in_a", "w_in_b", "w_mem_kv", "mem_norm_g", "w_o", "norm2_g", "w_mlp1", "w_mlp2",
                "kv_norm_g", "w_kv_shared", "b_f", "final_norm_g")


def _cparams(sem=None):
    return pltpu.CompilerParams(dimension_semantics=sem, vmem_limit_bytes=VMEM_LIMIT)


def _resident(shape, index_map):
    return pl.BlockSpec(shape, index_map, pipeline_mode=pl.Buffered(1))


def _pick(n, cands):
    for c in cands:
        if c <= n and n % c == 0:
            return c
    return n


def _dot(a, b, dims):
    return lax.dot_general(a, b, (dims, ((), ())), preferred_element_type=F32)


NN = ((1,), (0,))
NT = ((1,), (1,))
TN = ((0,), (0,))


MM_CHUNK = 512


def _mm(a, b, *, name, trans_b=False, out_dtype=F32, res=None, epilogue=None, u=None, norm=None):
    m, k = a.shape
    n = b.shape[0] if trans_b else b.shape[1]
    tm = _pick(m, (512, 256, 128))
    tn = _pick(n, (MM_CHUNK, 384, 256, 128))
    has_dres = norm is not None and norm[2] is not None

    def body(*refs):
        a_ref, b_ref = refs[0], refs[1]
        pos = 2
        res_ref = u_ref = x_ref = g_ref = dres_ref = None
        if res is not None:
            res_ref = refs[pos]
            pos += 1
        if u is not None:
            u_ref = refs[pos]
            pos += 1
        if norm is not None:
            x_ref, g_ref = refs[pos], refs[pos + 1]
            pos += 2
            if has_dres:
                dres_ref = refs[pos]
                pos += 1
        outs = refs[pos:]
        av = a_ref[...].astype(BF16)
        prods = []
        for c in range(n // tn):
            cols = slice(c * tn, (c + 1) * tn)
            if trans_b:
                acc = _dot(av, b_ref[cols, :].astype(BF16), NT)
            else:
                acc = _dot(av, b_ref[:, cols].astype(BF16), NN)
            if res_ref is not None:
                acc = res_ref[:, cols] + acc
            if norm is not None:
                prods.append(acc)
            elif epilogue == "relu2":
                outs[0][:, cols] = acc.astype(BF16)
                r = jnp.maximum(acc, 0.0)
                outs[1][:, cols] = (r * r).astype(BF16)
            elif epilogue == "drelu2":
                outs[0][:, cols] = (acc * (2.0 * jnp.maximum(u_ref[:, cols], 0.0))).astype(out_dtype)
            else:
                outs[0][:, cols] = acc.astype(out_dtype)
        if norm is not None:
            dyv = prods[0] if len(prods) == 1 else jnp.concatenate(prods, axis=1)
            xf = x_ref[...]
            r = lax.rsqrt(jnp.mean(xf * xf, axis=-1, keepdims=True) + EPS)
            xh = xf * r
            dyg = dyv * g_ref[...]
            dx = r * (dyg - xh * jnp.mean(dyg * xh, axis=-1, keepdims=True))
            outs[0][...] = dx if dres_ref is None else dres_ref[...] + dx

            @pl.when(pl.program_id(0) == 0)
            def _():
                outs[1][...] = jnp.zeros_like(outs[1])

            outs[1][...] += jnp.sum(dyv * xh, axis=0, keepdims=True)

    row = pl.BlockSpec((tm, n), lambda i: (i, 0))
    vec = pl.BlockSpec((1, n), lambda i: (0, 0))
    in_specs = [pl.BlockSpec((tm, k), lambda i: (i, 0)), _resident(b.shape, lambda i: (0, 0))]
    args = [a, b]
    if res is not None:
        in_specs.append(row)
        args.append(res)
    if u is not None:
        in_specs.append(row)
        args.append(u)
    if norm is not None:
        in_specs += [row, vec] + ([row] if has_dres else [])
        args += [norm[0], norm[1].reshape(1, n)] + ([norm[2]] if has_dres else [])
        out_shape = (jax.ShapeDtypeStruct((m, n), F32), jax.ShapeDtypeStruct((1, n), F32))
        out_specs = (row, vec)
    elif epilogue == "relu2":
        out_shape = (jax.ShapeDtypeStruct((m, n), BF16), jax.ShapeDtypeStruct((m, n), BF16))
        out_specs = (row, row)
    else:
        out_shape = (jax.ShapeDtypeStruct((m, n), out_dtype),)
        out_specs = (row,)
    outs = pl.pallas_call(
        body, name=name, grid=(m // tm,), in_specs=in_specs, out_specs=out_specs, out_shape=out_shape,
        compiler_params=_cparams(("arbitrary",) if norm is not None else ("parallel",)),
    )(*args)
    return outs if (epilogue == "relu2" or norm is not None) else outs[0]


def _mm_tn(x, dy, *, name):
    m, k1 = x.shape
    n = dy.shape[1]
    t1 = _pick(k1, (1024, 896, 768, 512, 256, 128))
    tn = _pick(n, (1024, 896, 768, 512, 256, 128))
    tm = _pick(m, (2048, 1024, 512, 256, 128))
    nm = m // tm

    def body(x_ref, dy_ref, o_ref):
        mm = pl.program_id(2)

        @pl.when(mm == 0)
        def _():
            o_ref[...] = jnp.zeros_like(o_ref)

        o_ref[...] += _dot(x_ref[...].astype(BF16), dy_ref[...].astype(BF16), TN)

    return pl.pallas_call(
        body, name=name, grid=(k1 // t1, n // tn, nm),
        in_specs=[pl.BlockSpec((tm, t1), lambda i, j, mm: (mm, i)), pl.BlockSpec((tm, tn), lambda i, j, mm: (mm, j))],
        out_specs=pl.BlockSpec((t1, tn), lambda i, j, mm: (i, j)),
        out_shape=jax.ShapeDtypeStruct((k1, n), F32),
        compiler_params=_cparams(("parallel", "parallel", "arbitrary")),
    )(x, dy)


def _rmsnorm_fwd(x, g, *, name):
    s, d = x.shape
    tm = _pick(s, (1024, 512, 256, 128))

    def body(x_ref, g_ref, o_ref):
        xf = x_ref[...]
        r = lax.rsqrt(jnp.mean(xf * xf, axis=-1, keepdims=True) + EPS)
        o_ref[...] = (xf * r * g_ref[...]).astype(BF16)

    return pl.pallas_call(
        body, name=name, grid=(s // tm,),
        in_specs=[pl.BlockSpec((tm, d), lambda i: (i, 0)), pl.BlockSpec((1, d), lambda i: (0, 0))],
        out_specs=pl.BlockSpec((tm, d), lambda i: (i, 0)),
        out_shape=jax.ShapeDtypeStruct((s, d), BF16),
        compiler_params=_cparams(("parallel",)),
    )(x, g.reshape(1, d))


def _final_loss(h, target, g, *, name):
    s, d = h.shape
    tm = _pick(s, (512, 256, 128))

    def body(h_ref, t_ref, g_ref, loss_ref, dh_ref, dg_ref):
        i = pl.program_id(0)
        xf = h_ref[...]
        gv = g_ref[...]
        r = lax.rsqrt(jnp.mean(xf * xf, axis=-1, keepdims=True) + EPS)
        xh = xf * r
        err = xh * gv - t_ref[...]
        part = 0.5 * jnp.sum(jnp.mean(err * err, axis=-1, keepdims=True), axis=0, keepdims=True)
        dyv = err * (1.0 / d)
        dyg = dyv * gv
        dh_ref[...] = r * (dyg - xh * jnp.mean(dyg * xh, axis=-1, keepdims=True))

        @pl.when(i == 0)
        def _():
            dg_ref[...] = jnp.zeros_like(dg_ref)
            loss_ref[...] = jnp.zeros_like(loss_ref)

        dg_ref[...] += jnp.sum(dyv * xh, axis=0, keepdims=True)
        loss_ref[...] += jnp.broadcast_to(part, loss_ref.shape)

    row = pl.BlockSpec((tm, d), lambda i: (i, 0))
    vec = pl.BlockSpec((1, d), lambda i: (0, 0))
    return pl.pallas_call(
        body, name=name, grid=(s // tm,), in_specs=[row, row, vec],
        out_specs=(pl.BlockSpec((1, LANES), lambda i: (0, 0)), row, vec),
        out_shape=(jax.ShapeDtypeStruct((1, LANES), F32), jax.ShapeDtypeStruct((s, d), F32), jax.ShapeDtypeStruct((1, d), F32)),
        compiler_params=_cparams(("arbitrary",)),
    )(h, target, g.reshape(1, d))


def _head_lanes(rows):
    lane = lax.broadcasted_iota(jnp.int32, (rows, LANES), 1)
    return [lane < HEAD_DIM, lane >= HEAD_DIM]


def _tri(b, cmp):
    row = lax.broadcasted_iota(jnp.int32, (b, b), 0)
    col = lax.broadcasted_iota(jnp.int32, (b, b), 1)
    return cmp(row, col)


def _twice(mask):
    return jnp.concatenate([mask, mask], axis=0)


def _stack_heads(x, heads):
    zero = jnp.zeros_like(x)
    return jnp.concatenate([jnp.where(heads[0], x, zero), jnp.where(heads[1], x, zero)], axis=0)


def _unstack_heads(x2, heads):
    b = x2.shape[0] // 2
    return jnp.where(heads[0], x2[:b], x2[b:])


def _stack_stat(stat):
    return jnp.concatenate([stat[:, 0:1], stat[:, HEAD_DIM:HEAD_DIM + 1]], axis=0)


def _unstack_stat(col, heads):
    b = col.shape[0] // 2
    return jnp.where(heads[0], jnp.broadcast_to(col[:b], (b, LANES)), jnp.broadcast_to(col[b:], (b, LANES)))


def _tri_dot(x, tri_bf16):
    return _dot(x.astype(BF16), tri_bf16, NN)


def _prefix_sums(x, tri_bf16, inclusive):
    sub = tri_bf16.shape[0]
    outs, carry = [], None
    for c in range(x.shape[1] // sub):
        xs = x[:, c * sub:(c + 1) * sub]
        loc = _tri_dot(xs, tri_bf16)
        outs.append(loc if carry is None else loc + carry)
        tot = loc[:, sub - 1:sub] if inclusive else loc[:, sub - 1:sub] + xs[:, sub - 1:sub]
        carry = tot if carry is None else carry + tot
    return (outs[0] if len(outs) == 1 else jnp.concatenate(outs, axis=1)), carry


def _suffix_sums(x, tri_bf16):
    sub = tri_bf16.shape[0]
    n = x.shape[1] // sub
    outs, carry = [None] * n, None
    for c in reversed(range(n)):
        xs = x[:, c * sub:(c + 1) * sub]
        loc = _tri_dot(xs, tri_bf16)
        outs[c] = loc if carry is None else loc + carry
        tot = loc[:, 0:1] + xs[:, 0:1].astype(BF16).astype(F32)
        carry = tot if carry is None else carry + tot
    return (outs[0] if n == 1 else jnp.concatenate(outs, axis=1)), carry


def _softplus2(z):
    z2 = z * LOG2E
    neg_abs = lax.bitcast_convert_type(lax.bitcast_convert_type(z2, jnp.uint32) | jnp.uint32(0x80000000), F32)
    return z2, jnp.maximum(z2, 0.0) + jnp.log2(1.0 + jnp.exp2(neg_abs))


def _block_rows(j, b):
    return pl.ds(pl.multiple_of(j * b, b), b)


def _sb_fwd(proj, *, name):
    s = proj.shape[0]
    b = _pick(s, (SB_B, 128))

    def body(q_ref, k_ref, v_ref, o_ref, tot_ref, first_ref, acc_ref):
        i = pl.program_id(1)
        heads = _head_lanes(b)
        suffix = _tri(min(b, CUM_SUB), lambda r, c: r > c).astype(BF16)
        strict = _twice(_tri(b, lambda r, c: c < r))
        q2 = _stack_heads(q_ref[...] * QK_SCALE, heads)

        def tile(j, a, masked):
            rows = _block_rows(j, b)
            z2, sp = _softplus2(_dot(q2, k_ref[rows, :], NT))
            if masked:
                sp = jnp.where(strict, sp, 0.0)
            rsum, total = _suffix_sums(sp, suffix)
            w = jnp.exp2((z2 - sp) - (a + rsum))
            if masked:
                w = jnp.where(strict, w, 0.0)
            acc_ref[...] += _dot(w.astype(BF16), v_ref[rows, :], NN)
            return a + total

        acc_ref[...] = jnp.zeros_like(acc_ref)
        a = tile(i, jnp.zeros((2 * b, 1), F32), True)

        def more(c):
            return jnp.logical_and(c[0] < i, c[2] < SB_CUT)

        def step(c):
            a = tile(i - 1 - c[0], c[1], False)
            return c[0] + 1, a, jnp.min(a)

        done, a, _ = lax.while_loop(more, step, (jnp.int32(0), a, jnp.min(a)))
        o_ref[...] = _unstack_heads(acc_ref[...], heads).astype(BF16)
        tot_ref[0] = _unstack_stat(a, heads)
        first_ref[pl.program_id(0), i] = i - done

    qblk = pl.BlockSpec((b, LANES), lambda p, i: (i, p))
    return pl.pallas_call(
        body, name=name, grid=(N_PAIRS, s // b),
        in_specs=[qblk, _resident((s, LANES), lambda p, i: (0, N_PAIRS + p)), _resident((s, LANES), lambda p, i: (0, 2 * N_PAIRS + p))],
        out_specs=(qblk, pl.BlockSpec((1, b, LANES), lambda p, i: (p, i, 0)), pl.BlockSpec(memory_space=pltpu.SMEM)),
        out_shape=(jax.ShapeDtypeStruct((s, MIX_W + MEM_W), BF16), jax.ShapeDtypeStruct((N_PAIRS, s, LANES), F32),
                   jax.ShapeDtypeStruct((N_PAIRS, s // b), jnp.int32)),
        scratch_shapes=[pltpu.VMEM((2 * b, LANES), F32)],
        compiler_params=_cparams(("arbitrary", "arbitrary")),
    )(proj, proj, proj)


def _sb_bwd(proj, dmerged, tot, first, *, name):
    s = proj.shape[0]
    b = _pick(s, (SB_B, 128))

    def body(q_ref, k_ref, v_ref, do_ref, tot_ref, first_ref, dq_ref, dk_ref, dv_ref, dq_acc):
        i = pl.program_id(1)

        @pl.when(i == 0)
        def _():
            dk_ref[...] = jnp.zeros_like(dk_ref)
            dv_ref[...] = jnp.zeros_like(dv_ref)

        heads = _head_lanes(b)
        incl = _tri(min(b, CUM_SUB), lambda r, c: r <= c).astype(BF16)
        excl = _tri(min(b, CUM_SUB), lambda r, c: r < c).astype(BF16)
        strict = _twice(_tri(b, lambda r, c: c < r))
        q2 = _stack_heads(q_ref[...] * QK_SCALE, heads)
        do2 = _stack_heads(do_ref[...], heads)
        tot2 = _stack_stat(tot_ref[0])

        def tile(j, pre, gpre, masked):
            rows = _block_rows(j, b)
            kb = k_ref[rows, :]
            z2, sp = _softplus2(_dot(q2, kb, NT))
            oms = jnp.exp2(-sp)
            if masked:
                sp = jnp.where(strict, sp, 0.0)
            pin, ptot = _prefix_sums(sp, incl, True)
            w = jnp.exp2((z2 - sp) + (pin + (pre - tot2)))
            if masked:
                w = jnp.where(strict, w, 0.0)
            gw = _dot(do2, v_ref[rows, :], NT) * w
            gex, gtot = _prefix_sums(gw, excl, False)
            dz = gw * oms - (1.0 - oms) * (gpre + gex)
            if masked:
                dz = jnp.where(strict, dz, 0.0)
            dzb = dz.astype(BF16)
            dq_acc[...] += _dot(dzb, kb, NN)
            dk_ref[rows, :] += _dot(dzb, q2, TN)
            dv_ref[rows, :] += _dot(w.astype(BF16), do2, TN)
            return pre + ptot, gpre + gtot

        dq_acc[...] = jnp.zeros_like(dq_acc)
        zero = jnp.zeros((2 * b, 1), F32)
        pre, gpre = lax.fori_loop(first_ref[pl.program_id(0), i], i, lambda j, c: tile(j, c[0], c[1], False), (zero, zero))
        tile(i, pre, gpre, True)
        dq_ref[...] = (_unstack_heads(dq_acc[...], heads) * QK_SCALE).astype(BF16)

    qblk = pl.BlockSpec((b, LANES), lambda p, i: (i, p))
    full = _resident((s, LANES), lambda p, i: (0, p))
    return pl.pallas_call(
        body, name=name, grid=(N_PAIRS, s // b),
        in_specs=[qblk, _resident((s, LANES), lambda p, i: (0, N_PAIRS + p)), _resident((s, LANES), lambda p, i: (0, 2 * N_PAIRS + p)),
                  qblk, pl.BlockSpec((1, b, LANES), lambda p, i: (p, i, 0)), pl.BlockSpec(memory_space=pltpu.SMEM)],
        out_specs=(qblk, full, full),
        out_shape=(jax.ShapeDtypeStruct((s, MIX_W), BF16), jax.ShapeDtypeStruct((s, MIX_W), F32), jax.ShapeDtypeStruct((s, MIX_W), F32)),
        scratch_shapes=[pltpu.VMEM((2 * b, LANES), F32)],
        compiler_params=_cparams(("parallel", "arbitrary")),
    )(proj, proj, proj, dmerged, tot, first)


def _fox_fwd(proj, kv, c_col, c_row, k_max, *, name):
    s = proj.shape[0]
    b = _pick(s, (FOX_B, 128))

    def body(q_ref, k_ref, v_ref, cc_ref, cr_ref, km_ref, o_ref, o32_ref, lse_ref, first_ref, acc_ref):
        i = pl.program_id(1)
        heads = _head_lanes(b)
        causal = _twice(_tri(b, lambda r, c: c <= r))
        top = lax.broadcasted_iota(jnp.int32, (2 * b, b), 0) < b
        q2 = _stack_heads(q_ref[...] * QK_SCALE, heads)
        c_t = _stack_stat(cc_ref[0])
        qf = q2.astype(F32)
        kmv = km_ref[0]
        z_max = jnp.sqrt(jnp.sum(qf * qf, axis=1, keepdims=True)) * jnp.where(top[:, 0:1], kmv[0:1, 0:1], kmv[1:2, 0:1]) * 1.001

        def tile(j, m, l, masked):
            rows = _block_rows(j, b)
            gate = c_t - jnp.where(top, cr_ref[0, 0:1, rows], cr_ref[0, 1:2, rows])
            sc = _dot(q2, k_ref[rows, :], NT) + gate
            if masked:
                sc = jnp.where(causal, sc, NEG_INF)
            m_new = jnp.maximum(m, jnp.max(sc, axis=1, keepdims=True))
            p = jnp.exp(sc - m_new)
            alpha = jnp.exp(m - m_new)
            acc_ref[...] = alpha * acc_ref[...] + _dot(p.astype(BF16), v_ref[rows, :], NN)
            return m_new, alpha * l + jnp.sum(p, axis=1, keepdims=True), jnp.max(z_max + gate[:, 0:1] - m_new)

        acc_ref[...] = jnp.zeros_like(acc_ref)
        m, l, slack = tile(i, jnp.full((2 * b, 1), NEG_INF, F32), jnp.zeros((2 * b, 1), F32), True)

        def more(c):
            return jnp.logical_and(c[0] < i, c[3] > -FOX_CUT)

        def step(c):
            m, l, slack = tile(i - 1 - c[0], c[1], c[2], False)
            return c[0] + 1, m, l, slack

        done, m, l, _ = lax.while_loop(more, step, (jnp.int32(0), m, l, slack))
        out = _unstack_heads(acc_ref[...] * (1.0 / l), heads)
        o_ref[...] = out.astype(BF16)
        o32_ref[...] = out
        lse_ref[0] = _unstack_stat(m + jnp.log(l), heads)
        first_ref[pl.program_id(0), i] = i - done

    qblk = pl.BlockSpec((b, LANES), lambda p, i: (i, p))
    stat = pl.BlockSpec((1, b, LANES), lambda p, i: (p, i, 0))
    return pl.pallas_call(
        body, name=name, grid=(N_PAIRS, s // b),
        in_specs=[qblk, _resident((s, LANES), lambda p, i: (0, p)), _resident((s, LANES), lambda p, i: (0, N_PAIRS + p)),
                  stat, _resident((1, 8, s), lambda p, i: (p, 0, 0)), pl.BlockSpec((1, 8, LANES), lambda p, i: (p, 0, 0))],
        out_specs=(qblk, qblk, stat, pl.BlockSpec(memory_space=pltpu.SMEM)),
        out_shape=(jax.ShapeDtypeStruct((s, MIX_W + MEM_W), BF16), jax.ShapeDtypeStruct((s, MIX_W), F32), jax.ShapeDtypeStruct((N_PAIRS, s, LANES), F32),
                   jax.ShapeDtypeStruct((N_PAIRS, s // b), jnp.int32)),
        scratch_shapes=[pltpu.VMEM((2 * b, LANES), F32)],
        compiler_params=_cparams(("arbitrary", "arbitrary")),
    )(proj, kv, kv, c_col, c_row, k_max)


def _fox_bwd(proj, kv, c_col, c_row, out32, dmerged, lse, first, *, name):
    s = proj.shape[0]
    b = _pick(s, (FOX_B, 128))

    def body(q_ref, k_ref, v_ref, cc_ref, cr_ref, o_ref, do_ref, lse_ref, first_ref, dq_ref, dk_ref, dv_ref, dc_ref, dcq_ref, dq_acc):
        i = pl.program_id(1)

        @pl.when(i == 0)
        def _():
            dk_ref[...] = jnp.zeros_like(dk_ref)
            dv_ref[...] = jnp.zeros_like(dv_ref)
            dc_ref[...] = jnp.zeros_like(dc_ref)

        heads = _head_lanes(b)
        causal = _twice(_tri(b, lambda r, c: c <= r))
        top = lax.broadcasted_iota(jnp.int32, (2 * b, b), 0) < b
        q2 = _stack_heads(q_ref[...] * QK_SCALE, heads)
        dov = do_ref[...]
        do2 = _stack_heads(dov, heads)
        prod = dov.astype(F32) * o_ref[...]
        delta = jnp.concatenate([jnp.sum(jnp.where(heads[hh], prod, 0.0), axis=1, keepdims=True) for hh in range(2)], axis=0)
        c_t = _stack_stat(cc_ref[0])
        lse_t = _stack_stat(lse_ref[0])

        def tile(j, rsum, masked):
            rows = _block_rows(j, b)
            kb = k_ref[rows, :]
            c_s = jnp.where(top, cr_ref[0, 0:1, rows], cr_ref[0, 1:2, rows])
            sc = _dot(q2, kb, NT) + (c_t - c_s)
            p = jnp.exp(sc - lse_t)
            if masked:
                p = jnp.where(causal, p, 0.0)
            ds = p * (_dot(do2, v_ref[rows, :], NT) - delta)
            dsb = ds.astype(BF16)
            dq_acc[...] += _dot(dsb, kb, NN)
            dk_ref[rows, :] += _dot(dsb, q2, TN)
            dv_ref[rows, :] += _dot(p.astype(BF16), do2, TN)
            dc_ref[0, 0:1, rows] -= jnp.sum(ds[:b], axis=0, keepdims=True)
            dc_ref[0, 1:2, rows] -= jnp.sum(ds[b:], axis=0, keepdims=True)
            return rsum + jnp.sum(ds, axis=1, keepdims=True)

        dq_acc[...] = jnp.zeros_like(dq_acc)
        rsum = lax.fori_loop(first_ref[pl.program_id(0), i], i, lambda j, r: tile(j, r, False), jnp.zeros((2 * b, 1), F32))
        rsum = tile(i, rsum, True)
        dq_ref[...] = (_unstack_heads(dq_acc[...], heads) * QK_SCALE).astype(BF16)
        dcq_ref[0] = _unstack_stat(rsum, heads)

    qblk = pl.BlockSpec((b, LANES), lambda p, i: (i, p))
    stat = pl.BlockSpec((1, b, LANES), lambda p, i: (p, i, 0))
    crow = _resident((1, 8, s), lambda p, i: (p, 0, 0))
    full = _resident((s, LANES), lambda p, i: (0, p))
    return pl.pallas_call(
        body, name=name, grid=(N_PAIRS, s // b),
        in_specs=[qblk, full, _resident((s, LANES), lambda p, i: (0, N_PAIRS + p)), stat, crow, qblk, qblk, stat,
                  pl.BlockSpec(memory_space=pltpu.SMEM)],
        out_specs=(qblk, full, full, crow, stat),
        out_shape=(jax.ShapeDtypeStruct((s, MIX_W), BF16), jax.ShapeDtypeStruct((s, MIX_W), F32), jax.ShapeDtypeStruct((s, MIX_W), F32),
                   jax.ShapeDtypeStruct((N_PAIRS, 8, s), F32), jax.ShapeDtypeStruct((N_PAIRS, s, LANES), F32)),
        scratch_shapes=[pltpu.VMEM((2 * b, LANES), F32)],
        compiler_params=_cparams(("parallel", "arbitrary")),
    )(proj, kv, kv, c_col, c_row, out32, dmerged, lse, first)


MEM_TQ = 1024


def _mem_fwd(proj, q_col_block, mkv, merged, *, name):
    s = proj.shape[0]
    tq = _pick(s, (MEM_TQ, 128))
    n_mem = mkv.shape[0]

    def body(q_ref, mkv_ref, merged_ref, o_ref):
        del merged_ref
        heads = _head_lanes(tq)
        for pp in range(MEM_W // LANES):
            cols = slice(pp * LANES, (pp + 1) * LANES)
            qv = q_ref[:, cols] * QK_SCALE
            mk = mkv_ref[:, pp * LANES:(pp + 1) * LANES]
            mv = mkv_ref[:, MEM_W + pp * LANES:MEM_W + (pp + 1) * LANES]
            o_sel = None
            for hh in range(2):
                qm = jnp.where(heads[hh], qv, jnp.zeros_like(qv))
                sc = _dot(qm, mk, NT)
                p = jnp.exp(sc - jnp.max(sc, axis=1, keepdims=True))
                p = p / jnp.sum(p, axis=1, keepdims=True)
                out = _dot(p.astype(BF16), mv, NN)
                o_sel = out if hh == 0 else jnp.where(heads[0], o_sel, out)
            o_ref[:, cols] = o_sel.astype(BF16)

    return pl.pallas_call(
        body, name=name, grid=(s // tq,),
        in_specs=[pl.BlockSpec((tq, MEM_W), lambda i: (i, q_col_block)), pl.BlockSpec((n_mem, 2 * MEM_W), lambda i: (0, 0)),
                  pl.BlockSpec(memory_space=pl.ANY)],
        out_specs=pl.BlockSpec((tq, MEM_W), lambda i: (i, MIX_W // MEM_W)),
        out_shape=jax.ShapeDtypeStruct(merged.shape, BF16),
        input_output_aliases={2: 0},
        compiler_params=_cparams(("parallel",)),
    )(proj, mkv, merged)


def _mem_bwd(proj, q_col_block, mkv, dmerged, *, name):
    s = proj.shape[0]
    tq = _pick(s, (MEM_TQ, 128))
    n_mem = mkv.shape[0]

    def body(q_ref, mkv_ref, do_ref, dq_ref, dmkv_ref):
        i = pl.program_id(0)

        @pl.when(i == 0)
        def _():
            dmkv_ref[...] = jnp.zeros_like(dmkv_ref)

        heads = _head_lanes(tq)
        for pp in range(MEM_W // LANES):
            cols = slice(pp * LANES, (pp + 1) * LANES)
            vcols = slice(MEM_W + pp * LANES, MEM_W + (pp + 1) * LANES)
            qv = q_ref[:, cols] * QK_SCALE
            dov = do_ref[:, cols]
            mk = mkv_ref[:, cols]
            mv = mkv_ref[:, vcols]
            dq_sel = None
            for hh in range(2):
                qm = jnp.where(heads[hh], qv, jnp.zeros_like(qv))
                dom = jnp.where(heads[hh], dov, jnp.zeros_like(dov))
                sc = _dot(qm, mk, NT)
                p = jnp.exp(sc - jnp.max(sc, axis=1, keepdims=True))
                p = p / jnp.sum(p, axis=1, keepdims=True)
                dp = _dot(dom, mv, NT)
                ds = p * (dp - jnp.sum(p * dp, axis=1, keepdims=True))
                dsb = ds.astype(BF16)
                dq = _dot(dsb, mk, NN)
                dmkv_ref[:, cols] += _dot(dsb, qm, TN)
                dmkv_ref[:, vcols] += _dot(p.astype(BF16), dom, TN)
                dq_sel = dq if hh == 0 else jnp.where(heads[0], dq_sel, dq)
            dq_ref[:, cols] = (dq_sel * QK_SCALE).astype(BF16)

    return pl.pallas_call(
        body, name=name, grid=(s // tq,),
        in_specs=[pl.BlockSpec((tq, MEM_W), lambda i: (i, q_col_block)), pl.BlockSpec((n_mem, 2 * MEM_W), lambda i: (0, 0)),
                  pl.BlockSpec((tq, MEM_W), lambda i: (i, MIX_W // MEM_W))],
        out_specs=(pl.BlockSpec((tq, MEM_W), lambda i: (i, 0)), pl.BlockSpec((n_mem, 2 * MEM_W), lambda i: (0, 0))),
        out_shape=(jax.ShapeDtypeStruct((s, MEM_W), BF16), jax.ShapeDtypeStruct((n_mem, 2 * MEM_W), F32)),
        compiler_params=_cparams(("arbitrary",)),
    )(proj, mkv, dmerged)


GATE_TB = 512


def _split3_dot(tri_bf16, x):
    x1 = x.astype(BF16)
    r1 = x - x1.astype(F32)
    x2 = r1.astype(BF16)
    x3 = (r1 - x2.astype(F32)).astype(BF16)
    return _dot(tri_bf16, x1, NN) + _dot(tri_bf16, x2, NN) + _dot(tri_bf16, x3, NN)


def _gate_fwd(f, b, *, name):
    s = f.shape[0]
    tb = _pick(s, (GATE_TB, 128))

    def body(f_ref, b_ref, c_ref, carry_ref):
        i = pl.program_id(0)

        @pl.when(i == 0)
        def _():
            carry_ref[...] = jnp.zeros_like(carry_ref)

        x = f_ref[...] + b_ref[...]
        lf = jnp.minimum(x, 0.0) - jnp.log1p(jnp.exp(-jnp.abs(x)))
        row = lax.broadcasted_iota(jnp.int32, (tb, tb), 0)
        col = lax.broadcasted_iota(jnp.int32, (tb, tb), 1)
        lower = (col <= row).astype(BF16)
        c = carry_ref[...] + _split3_dot(lower, lf)
        c_ref[...] = c
        carry_ref[...] = c[tb - 1:tb, :]

    return pl.pallas_call(
        body, name=name, grid=(s // tb,),
        in_specs=[pl.BlockSpec((tb, LANES), lambda i: (i, 0)), pl.BlockSpec((1, LANES), lambda i: (0, 0))],
        out_specs=pl.BlockSpec((tb, LANES), lambda i: (i, 0)),
        out_shape=jax.ShapeDtypeStruct((s, LANES), F32),
        scratch_shapes=[pltpu.VMEM((1, LANES), F32)],
        compiler_params=_cparams(("arbitrary",)),
    )(f, b)


def _gate_bwd(f, b, dc, *, name):
    s = f.shape[0]
    tb = _pick(s, (GATE_TB, 128))
    nb = s // tb

    def body(f_ref, b_ref, dc_ref, df_ref, db_ref, carry_ref):
        i = pl.program_id(0)

        @pl.when(i == 0)
        def _():
            carry_ref[...] = jnp.zeros_like(carry_ref)
            db_ref[...] = jnp.zeros_like(db_ref)

        row = lax.broadcasted_iota(jnp.int32, (tb, tb), 0)
        col = lax.broadcasted_iota(jnp.int32, (tb, tb), 1)
        upper = (col >= row).astype(BF16)
        dlf = carry_ref[...] + _split3_dot(upper, dc_ref[...])
        carry_ref[...] = dlf[0:1, :]
        x = f_ref[...] + b_ref[...]
        e = jnp.exp(-jnp.abs(x))
        one_minus_sig = jnp.where(x >= 0.0, e, 1.0) / (1.0 + e)
        df = dlf * one_minus_sig
        df_ref[...] = df
        db_ref[...] += jnp.sum(df, axis=0, keepdims=True)

    rev = pl.BlockSpec((tb, LANES), lambda i: (nb - 1 - i, 0))
    vec = pl.BlockSpec((1, LANES), lambda i: (0, 0))
    return pl.pallas_call(
        body, name=name, grid=(nb,), in_specs=[rev, vec, rev], out_specs=(rev, vec),
        out_shape=(jax.ShapeDtypeStruct((s, LANES), F32), jax.ShapeDtypeStruct((1, LANES), F32)),
        scratch_shapes=[pltpu.VMEM((1, LANES), F32)],
        compiler_params=_cparams(("arbitrary",)),
    )(f, b, dc)


def _all_gather(x, *, name):
    r, cdim = x.shape

    def body(x_ref, out_ref, send_sems, recv_sems, local_sem):
        mx, my, mc = lax.axis_index("x"), lax.axis_index("y"), lax.axis_index("c")
        me, sibling = (mx, my, mc), (mx, my, 1 - mc)
        chips = [(1 - mx, my), (mx, 1 - my), (1 - mx, 1 - my)]

        def rows(px, py, pc):
            return out_ref.at[pl.ds((4 * px + 2 * py + pc) * r, r), :]

        def copy(k, block, to, src=None):
            return pltpu.make_async_remote_copy(
                src_ref=rows(*block) if src is None else src, dst_ref=rows(*block),
                send_sem=send_sems.at[k], recv_sem=recv_sems.at[k], device_id=to, device_id_type=MESH)

        mine = pltpu.make_async_copy(x_ref, rows(*me), local_sem)
        mine.start()
        first = [copy(0, me, sibling, src=x_ref)]
        first += [copy(1 + j, me, (*chip, mc), src=x_ref) for j, chip in enumerate(chips)]
        for cp in first:
            cp.start()
        passed = [copy(4 + j, (*chip, mc), sibling) for j, chip in enumerate(chips)]
        for j, chip in enumerate(chips):
            copy(1 + j, (*chip, mc), me).wait_recv()
            passed[j].start()
        copy(0, sibling, me).wait_recv()
        for j, chip in enumerate(chips):
            copy(4 + j, (*chip, 1 - mc), me).wait_recv()
        for cp in first + passed:
            cp.wait_send()
        mine.wait()

    return pl.pallas_call(
        body, name=name,
        in_specs=[pl.BlockSpec(memory_space=pl.ANY)], out_specs=pl.BlockSpec(memory_space=pl.ANY),
        out_shape=jax.ShapeDtypeStruct((N_DEV * r, cdim), x.dtype),
        scratch_shapes=[pltpu.SemaphoreType.DMA((7,)), pltpu.SemaphoreType.DMA((7,)), pltpu.SemaphoreType.DMA],
    )(x)


N_CHIPS = N_DEV // 2


def _sibling_exchange(g, small, *, name):
    def body(g_ref, s_ref, out_ref, sout_ref, send_sems, recv_sems, local_sem):
        mx, my, mc = lax.axis_index("x"), lax.axis_index("y"), lax.axis_index("c")
        me = 4 * mx + 2 * my + mc
        sibling = (mx, my, 1 - mc)
        mine = pltpu.make_async_copy(s_ref, sout_ref.at[me], local_sem)
        mine.start()
        copies = []
        for k in range(N_CHIPS):
            copies.append(pltpu.make_async_remote_copy(
                src_ref=g_ref.at[2 * k + (1 - mc)], dst_ref=out_ref.at[k],
                send_sem=send_sems.at[k], recv_sem=recv_sems.at[k], device_id=sibling, device_id_type=MESH))
        for mask in range(1, N_DEV):
            px = 1 - mx if mask & 4 else mx
            py = 1 - my if mask & 2 else my
            pc = 1 - mc if mask & 1 else mc
            copies.append(pltpu.make_async_remote_copy(
                src_ref=s_ref, dst_ref=sout_ref.at[me],
                send_sem=send_sems.at[N_CHIPS - 1 + mask], recv_sem=recv_sems.at[N_CHIPS - 1 + mask], device_id=(px, py, pc), device_id_type=MESH))
        for cp in copies:
            cp.start()
        for cp in copies:
            cp.wait()
        mine.wait()

    n_sem = N_CHIPS + N_DEV - 1
    return pl.pallas_call(
        body, name=name,
        in_specs=[pl.BlockSpec(memory_space=pl.ANY), pl.BlockSpec(memory_space=pl.ANY)],
        out_specs=(pl.BlockSpec(memory_space=pl.ANY), pl.BlockSpec(memory_space=pl.ANY)),
        out_shape=(jax.ShapeDtypeStruct((N_CHIPS,) + g.shape[1:], g.dtype), jax.ShapeDtypeStruct((N_DEV,) + small.shape, small.dtype)),
        scratch_shapes=[pltpu.SemaphoreType.DMA((n_sem,)), pltpu.SemaphoreType.DMA((n_sem,)), pltpu.SemaphoreType.DMA],
    )(g, small)


def _pair_sum(g, theirs, *, name):
    _, r, cdim = g.shape
    tr = _pick(r, (4 * PACK_BLOCK_ROWS, 2 * PACK_BLOCK_ROWS, PACK_BLOCK_ROWS, 512, 256, 128, 64, 32, 16))

    def body(g0_ref, g1_ref, t_ref, o_ref):
        south = lax.axis_index("c") == 0
        mine = jnp.where(south, g0_ref[...].astype(F32), g1_ref[...].astype(F32))
        o_ref[...] = (mine + t_ref[...].astype(F32)).astype(o_ref.dtype)

    return pl.pallas_call(
        body, name=name, grid=(N_CHIPS, r // tr),
        in_specs=[pl.BlockSpec((1, tr, cdim), lambda k, i: (2 * k, i, 0)), pl.BlockSpec((1, tr, cdim), lambda k, i: (2 * k + 1, i, 0)),
                  pl.BlockSpec((1, tr, cdim), lambda k, i: (k, i, 0))],
        out_specs=pl.BlockSpec((1, tr, cdim), lambda k, i: (k, i, 0)),
        out_shape=jax.ShapeDtypeStruct((N_CHIPS, r, cdim), g.dtype),
        compiler_params=_cparams(("parallel", "parallel")),
    )(g, g, theirs)


def _chip_exchange(sums, *, name):
    def body(s_ref, out_ref, send_sems, recv_sems, local_sem):
        mx, my, mc = lax.axis_index("x"), lax.axis_index("y"), lax.axis_index("c")
        chip = 2 * mx + my
        mine = pltpu.make_async_copy(s_ref.at[chip], out_ref.at[chip], local_sem)
        mine.start()
        copies = []
        for mask in range(1, N_CHIPS):
            px = 1 - mx if mask & 2 else mx
            py = 1 - my if mask & 1 else my
            copies.append(pltpu.make_async_remote_copy(
                src_ref=s_ref.at[2 * px + py], dst_ref=out_ref.at[chip],
                send_sem=send_sems.at[mask - 1], recv_sem=recv_sems.at[mask - 1], device_id=(px, py, mc), device_id_type=MESH))
        for cp in copies:
            cp.start()
        for cp in copies:
            cp.wait()
        mine.wait()

    return pl.pallas_call(
        body, name=name,
        in_specs=[pl.BlockSpec(memory_space=pl.ANY)], out_specs=pl.BlockSpec(memory_space=pl.ANY),
        out_shape=jax.ShapeDtypeStruct(sums.shape, sums.dtype),
        scratch_shapes=[pltpu.SemaphoreType.DMA((N_CHIPS - 1,)), pltpu.SemaphoreType.DMA((N_CHIPS - 1,)), pltpu.SemaphoreType.DMA],
    )(sums)


def _adamw(parts, w, m, v, *, name):
    r, cdim = w.shape
    n_parts = parts.shape[0]
    tr = _pick(r, (4 * PACK_BLOCK_ROWS, 2 * PACK_BLOCK_ROWS, PACK_BLOCK_ROWS, 512, 256, 128, 64, 32, 16, 8))
    c1 = 1.0 / (1.0 - ADAM_B1 ** ADAM_STEP)
    c2 = 1.0 / (1.0 - ADAM_B2 ** ADAM_STEP)

    def body(p_ref, w_ref, m_ref, v_ref, g_ref, d_ref, nm_ref, nv_ref):
        g = p_ref[0].astype(F32)
        for part in range(1, n_parts):
            g = g + p_ref[part].astype(F32)
        mn = ADAM_B1 * m_ref[...] + (1.0 - ADAM_B1) * g
        vn = ADAM_B2 * v_ref[...] + (1.0 - ADAM_B2) * (g * g)
        g_ref[...] = g
        nm_ref[...] = mn
        nv_ref[...] = vn
        d_ref[...] = -ADAM_LR * ((mn * c1) / (jnp.sqrt(vn * c2) + ADAM_EPS) + ADAM_WD * w_ref[...])

    blk = pl.BlockSpec((tr, cdim), lambda i: (i, 0))
    shape = jax.ShapeDtypeStruct((r, cdim), F32)
    return pl.pallas_call(
        body, name=name, grid=(r // tr,),
        in_specs=[pl.BlockSpec((n_parts, tr, cdim), lambda i: (0, i, 0)), blk, blk, blk],
        out_specs=(blk, blk, blk, blk), out_shape=(shape, shape, shape, shape),
        compiler_params=_cparams(("parallel",)),
    )(parts, w, m, v)


def _rows_of(shape):
    n = math.prod(shape)
    assert n % LANES == 0, shape
    rows = n // LANES
    return -(-rows // PACK_ROW_ALIGN) * PACK_ROW_ALIGN


def _layout(shard_shapes):
    out, off = {}, 0
    for name, _ in SHARDED:
        rows = _rows_of(shard_shapes[name])
        out[name] = (off, rows, tuple(shard_shapes[name]))
        off += rows
    return out, -(-off // PACK_BLOCK_ROWS) * PACK_BLOCK_ROWS


def _pack_shards(layout, total, arrays, dtype):
    parts = []
    for name, _ in SHARDED:
        _, rows, _ = layout[name]
        flat = arrays[name].astype(dtype).reshape(-1, LANES)
        parts.append(jnp.pad(flat, ((0, rows - flat.shape[0]), (0, 0))))
    used = sum(p.shape[0] for p in parts)
    if total > used:
        parts.append(jnp.zeros((total - used, LANES), dtype))
    return jnp.concatenate(parts, axis=0)


def _unpack_shard(layout, flat, name):
    off, _, shape = layout[name]
    n = math.prod(shape) // LANES
    return flat[off:off + n].reshape(shape)


def _unpack_full(layout, gathered, name, axis):
    off, _, shape = layout[name]
    n = math.prod(shape) // LANES
    blocks = gathered[:, off:off + n].reshape((N_DEV,) + shape)
    blocks = jnp.moveaxis(blocks, 0, axis)
    return blocks.reshape(shape[:axis] + (N_DEV * shape[axis],) + shape[axis + 1:])


def _pack_full(layout, total, grads):
    parts = []
    for name, axis in SHARDED:
        _, rows, shape = layout[name]
        g = grads[name]
        blocks = g.reshape(shape[:axis] + (N_DEV, shape[axis]) + shape[axis + 1:])
        blocks = jnp.moveaxis(blocks, axis, 0).reshape(N_DEV, -1, LANES)
        parts.append(jnp.pad(blocks, ((0, 0), (0, rows - blocks.shape[1]), (0, 0))))
    used = sum(p.shape[1] for p in parts)
    if total > used:
        parts.append(jnp.zeros((N_DEV, total - used, LANES), F32))
    return jnp.concatenate(parts, axis=1)


def _pad_lanes(a):
    return jnp.pad(a, ((0, 0), (0, LANES - a.shape[1])))


def _pair_layouts(c):
    s = c.shape[0]
    by_pair = c.T.reshape(N_PAIRS, 2, s)
    c_col = jnp.repeat(by_pair.transpose(0, 2, 1), HEAD_DIM, axis=2)
    c_row = jnp.pad(by_pair, ((0, 0), (0, 6), (0, 0)))
    return c_col, c_row


def _key_norm_bound(k):
    norms = jnp.sqrt(jnp.max(jnp.sum(jnp.square(k.astype(F32)).reshape(k.shape[0], N_MIX_HEADS, HEAD_DIM), axis=2), axis=0))
    rows = jnp.pad(norms.reshape(N_PAIRS, 2), ((0, 0), (0, 6)))
    return jnp.broadcast_to(rows[:, :, None], (N_PAIRS, 8, LANES))


def _forward_backward(x, mem, target, wts, small):
    n_a = wts["w_in_a"].shape[0]
    n_b = wts["w_in_b"].shape[0]
    depth = n_a + n_b
    w_kv = wts["w_kv_shared"]
    w_kv_kv = w_kv[:, :2 * MIX_W]
    w_kv_f = _pad_lanes(w_kv[:, 2 * MIX_W:])
    b_f = _pad_lanes(small["b_f"].reshape(1, -1))

    saved = []
    shared = None
    h = x
    for l in range(depth):
        is_a = l < n_a
        if l == n_a:
            hs = _rmsnorm_fwd(h, small["kv_norm_g"], name="kv_norm")
            kv = _mm(hs, w_kv_kv, name="kv_proj", out_dtype=BF16)
            f = _mm(hs, w_kv_f, name="gate_proj")
            c = _gate_fwd(f, b_f, name="gate_cumsum")
            c_col, c_row = _pair_layouts(c[:, :N_MIX_HEADS])
            shared = dict(h=h, hs=hs, kv=kv, f=f, c_col=c_col, c_row=c_row, k_max=_key_norm_bound(kv[:, :MIX_W]))
        hn = _rmsnorm_fwd(h, small["norm1_g"][l], name=f"norm1_{l}")
        memn = _rmsnorm_fwd(mem, small["mem_norm_g"][l], name=f"mem_norm_{l}")
        mkv = _mm(memn, wts["w_mem_kv"][l], name=f"mem_kv_{l}", out_dtype=BF16)
        if is_a:
            proj = _mm(hn, wts["w_in_a"][l], name=f"in_proj_{l}", out_dtype=BF16)
            mix, stat, first = _sb_fwd(proj, name=f"sb_fwd_{l}")
            mix32 = None
            q_block = 3 * MIX_W // MEM_W
        else:
            proj = _mm(hn, wts["w_in_b"][l - n_a], name=f"in_proj_{l}", out_dtype=BF16)
            mix, mix32, stat, first = _fox_fwd(proj, shared["kv"], shared["c_col"], shared["c_row"], shared["k_max"], name=f"fox_fwd_{l}")
            q_block = MIX_W // MEM_W
        merged = _mem_fwd(proj, q_block, mkv, mix, name=f"mem_fwd_{l}")
        h_mid = _mm(merged, wts["w_o"][l], name=f"o_proj_{l}", res=h)
        h2n = _rmsnorm_fwd(h_mid, small["norm2_g"][l], name=f"norm2_{l}")
        u, act = _mm(h2n, wts["w_mlp1"][l], name=f"mlp1_{l}", epilogue="relu2")
        h_out = _mm(act, wts["w_mlp2"][l], name=f"mlp2_{l}", res=h_mid)
        saved.append(dict(h=h, hn=hn, memn=memn, mkv=mkv, proj=proj, stat=stat, first=first, mix32=mix32, merged=merged, h_mid=h_mid, h2n=h2n, u=u, act=act,
                          q_block=q_block))
        h = h_out

    loss, dh, dg_final = _final_loss(h, target, small["final_norm_g"], name="final_loss")

    g_w = {k: [None] * wts[k].shape[0] for k in ("w_in_a", "w_in_b", "w_mem_kv", "w_o", "w_mlp1", "w_mlp2")}
    g_n = {k: [None] * depth for k in ("norm1_g", "mem_norm_g", "norm2_g")}
    dk_sh = dv_sh = dc_sh = dcq_sh = None
    for l in reversed(range(depth)):
        sv = saved[l]
        is_a = l < n_a
        du = _mm(dh, wts["w_mlp2"][l], name=f"d_act_{l}", trans_b=True, epilogue="drelu2", u=sv["u"], out_dtype=BF16)
        g_w["w_mlp2"][l] = _mm_tn(sv["act"], dh, name=f"dw_mlp2_{l}")
        g_w["w_mlp1"][l] = _mm_tn(sv["h2n"], du, name=f"dw_mlp1_{l}")
        dh_mid, g_n["norm2_g"][l] = _mm(du, wts["w_mlp1"][l], name=f"d_h2n_{l}", trans_b=True, norm=(sv["h_mid"], small["norm2_g"][l], dh))
        dmerged = _mm(dh_mid, wts["w_o"][l], name=f"d_merged_{l}", trans_b=True, out_dtype=BF16)
        g_w["w_o"][l] = _mm_tn(sv["merged"], dh_mid, name=f"dw_o_{l}")
        dqm, dmkv = _mem_bwd(sv["proj"], sv["q_block"], sv["mkv"], dmerged, name=f"mem_bwd_{l}")
        if is_a:
            dq, dk, dv = _sb_bwd(sv["proj"], dmerged, sv["stat"], sv["first"], name=f"sb_bwd_{l}")
            dproj = jnp.concatenate([dq, dk.astype(BF16), dv.astype(BF16), dqm], axis=1)
            w_in, key, idx = wts["w_in_a"][l], "w_in_a", l
        else:
            dq, dk, dv, dc, dcq = _fox_bwd(sv["proj"], shared["kv"], shared["c_col"], shared["c_row"], sv["mix32"], dmerged, sv["stat"],
                                           sv["first"], name=f"fox_bwd_{l}")
            dk_sh = dk if dk_sh is None else dk_sh + dk
            dv_sh = dv if dv_sh is None else dv_sh + dv
            dc_sh = dc if dc_sh is None else dc_sh + dc
            dcq_sh = dcq if dcq_sh is None else dcq_sh + dcq
            dproj = jnp.concatenate([dq, dqm], axis=1)
            w_in, key, idx = wts["w_in_b"][l - n_a], "w_in_b", l - n_a
        g_w[key][idx] = _mm_tn(sv["hn"], dproj, name=f"dw_in_{l}")
        dh, g_n["norm1_g"][l] = _mm(dproj, w_in, name=f"d_hn_{l}", trans_b=True, norm=(sv["h"], small["norm1_g"][l], dh_mid))
        g_w["w_mem_kv"][l] = _mm_tn(sv["memn"], dmkv, name=f"dw_mem_kv_{l}")
        _, g_n["mem_norm_g"][l] = _mm(dmkv, wts["w_mem_kv"][l], name=f"d_memn_{l}", trans_b=True, norm=(mem, small["mem_norm_g"][l], None))
        if l == n_a:
            s_len = x.shape[0]
            dc_query = jnp.stack([dcq_sh[:, :, 0], dcq_sh[:, :, HEAD_DIM]], axis=-1).transpose(1, 0, 2).reshape(s_len, N_MIX_HEADS)
            dc_tok = _pad_lanes(dc_sh[:, :2, :].reshape(N_MIX_HEADS, s_len).T + dc_query)
            df, db = _gate_bwd(shared["f"], b_f, dc_tok, name="gate_bwd")
            dkv = jnp.concatenate([dk_sh, dv_sh], axis=1)
            dw_kv_kv = _mm_tn(shared["hs"], dkv, name="dw_kv")
            dw_kv_f = _mm_tn(shared["hs"], df, name="dw_gate")
            dhs = _mm(dkv, w_kv_kv, name="d_hs_kv", trans_b=True)
            dh, dg_kv = _mm(df, w_kv_f, name="d_hs_gate", trans_b=True, res=dhs, norm=(shared["h"], small["kv_norm_g"], dh))
            g_kv = jnp.concatenate([dw_kv_kv, dw_kv_f[:, :N_KV_F]], axis=1)

    grads = {k: jnp.stack(v) for k, v in g_w.items()}
    grads["w_kv_shared"] = g_kv
    d = x.shape[1]
    small_g = dict(
        norm1_g=jnp.concatenate(g_n["norm1_g"], axis=0), mem_norm_g=jnp.concatenate(g_n["mem_norm_g"], axis=0),
        norm2_g=jnp.concatenate(g_n["norm2_g"], axis=0), kv_norm_g=dg_kv.reshape(d), b_f=db[0, :N_KV_F], final_norm_g=dg_final.reshape(d))
    return loss, dh, grads, small_g


def _pack_small(vals, d):
    rows = []
    for name in REPLICATED:
        a = vals[name].astype(F32)
        if name == "b_f":
            a = jnp.pad(a, (0, d - a.shape[0]))
        rows.append(a.reshape(-1, d))
    packed = jnp.concatenate(rows, axis=0)
    pad = -packed.shape[0] % 8
    return jnp.pad(packed, ((0, pad), (0, 0)))


def _unpack_small(packed, shapes):
    out, off = {}, 0
    for name in REPLICATED:
        shape = shapes[name]
        if name == "b_f":
            out[name] = packed[off, :shape[0]]
            off += 1
        else:
            n = math.prod(shape) // packed.shape[1]
            out[name] = packed[off:off + n].reshape(shape)
            off += n
    return out


def kernel(x, mem, norm1_g, w_in_a, w_in_b, w_mem_kv, mem_norm_g, w_o, norm2_g, w_mlp1, w_mlp2, kv_norm_g, w_kv_shared, b_f, final_norm_g, loss_target, m_norm1_g, m_w_in_a, m_w_in_b, m_w_mem_kv, m_mem_norm_g, m_w_o, m_norm2_g, m_w_mlp1, m_w_mlp2, m_kv_norm_g, m_w_kv_shared, m_b_f, m_final_norm_g, v_norm1_g, v_w_in_a, v_w_in_b, v_w_mem_kv, v_mem_norm_g, v_w_o, v_norm2_g, v_w_mlp1, v_w_mlp2, v_kv_norm_g, v_w_kv_shared, v_b_f, v_final_norm_g):
    w = dict(norm1_g=norm1_g, w_in_a=w_in_a, w_in_b=w_in_b, w_mem_kv=w_mem_kv, mem_norm_g=mem_norm_g, w_o=w_o, norm2_g=norm2_g,
             w_mlp1=w_mlp1, w_mlp2=w_mlp2, kv_norm_g=kv_norm_g, w_kv_shared=w_kv_shared, b_f=b_f, final_norm_g=final_norm_g)
    m = dict(norm1_g=m_norm1_g, w_in_a=m_w_in_a, w_in_b=m_w_in_b, w_mem_kv=m_w_mem_kv, mem_norm_g=m_mem_norm_g, w_o=m_w_o,
             norm2_g=m_norm2_g, w_mlp1=m_w_mlp1, w_mlp2=m_w_mlp2, kv_norm_g=m_kv_norm_g, w_kv_shared=m_w_kv_shared, b_f=m_b_f,
             final_norm_g=m_final_norm_g)
    v = dict(norm1_g=v_norm1_g, w_in_a=v_w_in_a, w_in_b=v_w_in_b, w_mem_kv=v_w_mem_kv, mem_norm_g=v_mem_norm_g, w_o=v_w_o,
             norm2_g=v_norm2_g, w_mlp1=v_w_mlp1, w_mlp2=v_w_mlp2, kv_norm_g=v_kv_norm_g, w_kv_shared=v_w_kv_shared, b_f=v_b_f,
             final_norm_g=v_final_norm_g)
    d = x.shape[-1]
    layout, total = _layout({name: w[name].shape for name, _ in SHARDED})

    gathered = _all_gather(_pack_shards(layout, total, w, BF16), name="gather_weights").reshape(N_DEV, total, LANES)
    wts = {name: _unpack_full(layout, gathered, name, axis) for name, axis in SHARDED}
    small = {name: w[name] for name in REPLICATED}

    loss, grad_x, grads, small_g = _forward_backward(x[0], mem[0], loss_target[0], wts, small)

    g_packed = _pack_full(layout, total, grads).astype(BF16)
    theirs, small_parts = _sibling_exchange(g_packed, _pack_small(small_g, d), name="exchange_siblings")
    parts = _chip_exchange(_pair_sum(g_packed, theirs, name="pair_sum"), name="exchange_chips")
    g_flat, d_flat, m_flat, v_flat = _adamw(parts, _pack_shards(layout, total, w, F32), _pack_shards(layout, total, m, F32),
                                            _pack_shards(layout, total, v, F32), name="adamw_sharded")
    gs, ds_, ms, vs = _adamw(small_parts, _pack_small(w, d), _pack_small(m, d), _pack_small(v, d), name="adamw_replicated")

    shapes = {name: w[name].shape for name in REPLICATED}
    out_g, out_d, out_m, out_v = {}, {}, {}, {}
    for flat, small_flat, out in ((g_flat, gs, out_g), (d_flat, ds_, out_d), (m_flat, ms, out_m), (v_flat, vs, out_v)):
        for name, _ in SHARDED:
            out[name] = _unpack_shard(layout, flat, name)
        out.update(_unpack_small(small_flat, shapes))

    loss_total = lax.psum(loss[0, 0], ("x", "y", "c"))
    return (loss_total, grad_x[None], *[out_g[n] for n in WEIGHT_ORDER], *[out_d[n] for n in WEIGHT_ORDER],
            *[out_m[n] for n in WEIGHT_ORDER], *[out_v[n] for n in WEIGHT_ORDER])
```
